```python
import math
import jax, jax.numpy as jnp
from jax import lax
import numpy as np

D_MODEL = 1024
BATCH = 8
SEQ = 2048
DEPTH = 2
DEC_BATCH = 128
DEC_SEQ = 1
PAST_LEN = 16384
PAGE_SIZE = 128

NORM_EPS = 1e-6
CONV_W = 4
D_FF = 2816
FFN_RES = 0.5

GLA_HEADS = 4
GLA_DK = D_MODEL // 8
GLA_DV = D_MODEL // 4
GLA_RANK = 16
GLA_GATE_NORM = 16.0
GLA_CHUNK = 16

SSD_D_INNER = 2 * D_MODEL
SSD_HEAD_DIM = 64
SSD_HEADS = SSD_D_INNER // SSD_HEAD_DIM
SSD_GROUPS = 4
SSD_HPG = SSD_HEADS // SSD_GROUPS
SSD_D_STATE = 128
SSD_CHUNK = 64
SSD_CONV_DIM = SSD_D_INNER + 2 * SSD_GROUPS * SSD_D_STATE

LRU_WIDTH = D_MODEL * 5 // 4
LRU_BLOCKS = 10
LRU_BLOCK_W = LRU_WIDTH // LRU_BLOCKS
LRU_C = 8.0

GDN_HEADS = 8
GDN_DK = 128
GDN_DV = 128
GDN_CHUNK = 64
GDN_CONV_DIM = GDN_HEADS * (2 * GDN_DK + GDN_DV)

L0_SIZES = (GLA_HEADS * GLA_DK, GLA_HEADS * GLA_DK, GLA_HEADS * GLA_DV, GLA_HEADS * GLA_DV, GLA_RANK,
            SSD_D_INNER, SSD_CONV_DIM, SSD_HEADS)
L0_IN = sum(L0_SIZES)
L0_MIX = GLA_HEADS * GLA_DV + SSD_D_INNER
L1_SIZES = (LRU_WIDTH, LRU_WIDTH, GDN_CONV_DIM, GDN_HEADS * GDN_DV, GDN_HEADS, GDN_HEADS)
L1_IN = sum(L1_SIZES)
L1_MIX = LRU_WIDTH + GDN_HEADS * GDN_DV

kernel_name = 'hybrid_gla_ssd_rglru_gdn_macaron_step'


def rmsnorm(x, g):
    xf = x.astype(jnp.float32)
    y = xf * lax.rsqrt(jnp.mean(xf * xf, axis=-1, keepdims=True) + NORM_EPS)
    return (y * g.astype(jnp.float32)).astype(x.dtype)


def l2norm(x):
    return x * lax.rsqrt(jnp.sum(x * x, axis=-1, keepdims=True) + NORM_EPS)


def swiglu(x, w_in, w_out):
    gate, up = jnp.split(x @ w_in, 2, axis=-1)
    return (jax.nn.silu(gate) * up) @ w_out


def split_cols(h, sizes):
    return jnp.split(h, [int(i) for i in np.cumsum(sizes)[:-1]], axis=-1)


def causal_conv(x, buf, w, b=None):
    L = x.shape[1]
    xx = jnp.concatenate([buf.astype(x.dtype), x], axis=1)
    y = xx[:, 0:L] * w[0]
    for j in range(1, CONV_W):
        y = y + xx[:, j:j + L] * w[j]
    if b is not None:
        y = y + b
    return y, xx[:, L:]


def to_chunks(x, c):
    nb, L = x.shape[0], x.shape[1]
    n = -(-L // c)
    x = jnp.pad(x, [(0, 0), (0, n * c - L)] + [(0, 0)] * (x.ndim - 2))
    return jnp.moveaxis(x.reshape((nb, n, c) + x.shape[2:]), 1, 0)


def from_chunks(y, L):
    y = jnp.moveaxis(y, 0, 1)
    return y.reshape((y.shape[0], -1) + y.shape[3:])[:, :L]


def gla_chunked(q, k, v, log_a, s0):
    L = q.shape[1]
    c = min(GLA_CHUNK, L)
    mask = jnp.tril(jnp.ones((c, c), bool))[None, :, :, None, None]

    def step(S, inp):
        qc, kc, vc, ac = inp
        b = jnp.cumsum(ac, axis=1)
        dec = jnp.exp(jnp.where(mask, b[:, :, None] - b[:, None], -jnp.inf))
        att = jnp.einsum('bthd,bshd,btshd->bhts', qc, kc, dec)
        o = jnp.einsum('bhts,bshv->bthv', att, vc) + jnp.einsum('bthd,bhdv->bthv', qc * jnp.exp(b), S)
        b_last = b[:, -1]
        S = jnp.exp(b_last)[..., None] * S + jnp.einsum('bshd,bshv->bhdv', kc * jnp.exp(b_last[:, None] - b), vc)
        return S, o

    S, o = lax.scan(step, s0, (to_chunks(q, c), to_chunks(k, c), to_chunks(v, c), to_chunks(log_a, c)))
    return from_chunks(o, L), S


def ssd_chunked(x, dt, bm, cm, a, s0):
    L = x.shape[1]
    c = min(SSD_CHUNK, L)
    mask = jnp.tril(jnp.ones((c, c), bool))[None, :, :, None, None]

    def step(S, inp):
        xc, dtc, bc, cc = inp
        cum = jnp.cumsum(dtc * a, axis=1)
        seg = jnp.exp(jnp.where(mask, cum[:, :, None] - cum[:, None], -jnp.inf))
        cb = jnp.einsum('btgn,bsgn->btsg', cc, bc)
        y = jnp.einsum('btsg,btsgh,bsgh,bsghp->btghp', cb, seg, dtc, xc)
        y = y + jnp.einsum('btgn,btgh,bghpn->btghp', cc, jnp.exp(cum), S)
        last = cum[:, -1]
        w = jnp.exp(last[:, None] - cum) * dtc
        S = jnp.exp(last)[..., None, None] * S + jnp.einsum('bsgh,bsghp,bsgn->bghpn', w, xc, bc)
        return S, y

    S, y = lax.scan(step, s0, (to_chunks(x, c), to_chunks(dt, c), to_chunks(bm, c), to_chunks(cm, c)))
    return from_chunks(y, L), S


def gdn_chunked(q, k, v, g, beta, s0):
    L = q.shape[1]
    c = min(GDN_CHUNK, L)
    qh, kh, vh = (jnp.moveaxis(to_chunks(t, c), 3, 2) for t in (q, k, v))
    gh = jnp.cumsum(jnp.moveaxis(to_chunks(g, c), 3, 2), axis=-1)
    bh = jnp.moveaxis(to_chunks(beta, c), 3, 2)
    incl = jnp.tril(jnp.ones((c, c), bool))
    strict = jnp.tril(jnp.ones((c, c), bool), -1)
    dec = jnp.exp(jnp.where(incl, gh[..., :, None] - gh[..., None, :], -jnp.inf))
    m = bh[..., :, None] * jnp.where(strict, dec, 0.0) * jnp.einsum('nbhtd,nbhsd->nbhts', kh, kh)
    amat = jnp.eye(c, dtype=m.dtype) + m
    rhs = jnp.concatenate([vh * bh[..., None], kh * (bh * jnp.exp(gh))[..., None]], axis=-1)
    sol = lax.linalg.triangular_solve(amat, rhs, left_side=True, lower=True, unit_diagonal=True)
    u, w = sol[..., :GDN_DV], sol[..., GDN_DV:]
    aqk = dec * jnp.einsum('nbhtd,nbhsd->nbhts', qh, kh)
    kdec = kh * jnp.exp(gh[..., -1:] - gh)[..., None]

    def step(S, inp):
        qn, un, wn, an, gn, kn = inp
        delta = un - jnp.einsum('bhtk,bhkv->bhtv', wn, S)
        o = jnp.exp(gn)[..., None] * jnp.einsum('bhtk,bhkv->bhtv', qn, S) + jnp.einsum('bhts,bhsv->bhtv', an, delta)
        S = jnp.exp(gn[..., -1])[..., None, None] * S + jnp.einsum('bhtk,bhtv->bhkv', kn, delta)
        return S, o

    S, o = lax.scan(step, s0, (qh, u, w, aqk, gh, kdec))
    return from_chunks(jnp.moveaxis(o, 2, 3), L), S


def rglru(x, r_logit, i_logit, lam, h0):
    log_a = -LRU_C * jax.nn.sigmoid(r_logit) * jax.nn.softplus(-lam)
    a = jnp.exp(log_a)
    b = jnp.sqrt(-jnp.expm1(2.0 * log_a)) * (jax.nn.sigmoid(i_logit) * x)
    b = b.at[:, 0].add(a[:, 0] * h0)

    def combine(l, r):
        return (l[0] * r[0], r[0] * l[1] + r[1])

    _, h = lax.associative_scan(combine, (a, b), axis=1)
    return h, h[:, -1]


def mixer_ab(h, s_gla, s_ssd, s_ssd_conv, l0_w_in, l0_w_out, gla_w_gate2, gla_b_gate, gla_norm,
             ssd_conv_w, ssd_conv_b, ssd_dt_bias, ssd_a_log, ssd_d, ssd_norm):
    nb, L = h.shape[0], h.shape[1]
    f32 = jnp.float32
    gq, gk, gv, gg, glr, sz, sxbc, sdt = split_cols(h @ l0_w_in, L0_SIZES)
    q = gq.reshape(nb, L, GLA_HEADS, GLA_DK).astype(f32) * (GLA_DK ** -0.5)
    k = gk.reshape(nb, L, GLA_HEADS, GLA_DK).astype(f32)
    v = gv.reshape(nb, L, GLA_HEADS, GLA_DV).astype(f32)
    log_a = jax.nn.log_sigmoid((glr @ gla_w_gate2 + gla_b_gate).astype(f32)) / GLA_GATE_NORM
    o, new_gla = gla_chunked(q, k, v, log_a.reshape(nb, L, GLA_HEADS, GLA_DK), s_gla.astype(f32))
    o = rmsnorm(o, gla_norm) * jax.nn.silu(gg.astype(f32)).reshape(nb, L, GLA_HEADS, GLA_DV)
    o_a = o.reshape(nb, L, GLA_HEADS * GLA_DV)
    xbc, new_conv = causal_conv(sxbc, s_ssd_conv, ssd_conv_w, ssd_conv_b)
    xbc = jax.nn.silu(xbc.astype(f32))
    sx, sb, sc = split_cols(xbc, (SSD_D_INNER, SSD_GROUPS * SSD_D_STATE, SSD_GROUPS * SSD_D_STATE))
    sx = sx.reshape(nb, L, SSD_GROUPS, SSD_HPG, SSD_HEAD_DIM)
    sb = sb.reshape(nb, L, SSD_GROUPS, SSD_D_STATE)
    sc = sc.reshape(nb, L, SSD_GROUPS, SSD_D_STATE)
    dt = jax.nn.softplus(sdt.astype(f32) + ssd_dt_bias.astype(f32)).reshape(nb, L, SSD_GROUPS, SSD_HPG)
    a = -jnp.exp(ssd_a_log.astype(f32)).reshape(SSD_GROUPS, SSD_HPG)
    s0 = s_ssd.astype(f32).reshape(nb, SSD_GROUPS, SSD_HPG, SSD_HEAD_DIM, SSD_D_STATE)
    y, new_ssd = ssd_chunked(sx, dt, sb, sc, a, s0)
    y = y + ssd_d.astype(f32).reshape(SSD_GROUPS, SSD_HPG, 1) * sx
    y = y.reshape(nb, L, SSD_D_INNER) * jax.nn.silu(sz.astype(f32))
    y = rmsnorm(y.reshape(nb, L, SSD_GROUPS, -1), ssd_norm.reshape(SSD_GROUPS, -1)).reshape(nb, L, SSD_D_INNER)
    out = jnp.concatenate([o_a, y], axis=-1).astype(h.dtype) @ l0_w_out
    new_ssd = new_ssd.reshape(nb, SSD_HEADS, SSD_HEAD_DIM, SSD_D_STATE)
    return out, new_gla.astype(h.dtype), new_ssd.astype(h.dtype), new_conv


def mixer_cd(h, s_lru, s_lru_conv, s_gdn, s_gdn_conv, l1_w_in, l1_w_out, lru_conv_w, lru_conv_b,
             lru_w_a, lru_b_a, lru_w_x, lru_b_x, lru_lambda, gdn_conv_w, gdn_a_log, gdn_dt_bias, gdn_norm):
    nb, L = h.shape[0], h.shape[1]
    f32 = jnp.float32
    rg, rx, dqkv, dz, da, db = split_cols(h @ l1_w_in, L1_SIZES)
    xc, new_lru_conv = causal_conv(rx, s_lru_conv, lru_conv_w, lru_conv_b)
    xc = xc.astype(f32)
    xb = xc.reshape(nb, L, LRU_BLOCKS, LRU_BLOCK_W)
    r_logit = jnp.einsum('blni,nij->blnj', xb, lru_w_a.astype(f32)).reshape(nb, L, LRU_WIDTH) + lru_b_a.astype(f32)
    i_logit = jnp.einsum('blni,nij->blnj', xb, lru_w_x.astype(f32)).reshape(nb, L, LRU_WIDTH) + lru_b_x.astype(f32)
    hs, new_lru = rglru(xc, r_logit, i_logit, lru_lambda.astype(f32), s_lru.astype(f32))
    o_c = jax.nn.gelu(rg.astype(f32), approximate=True) * hs
    qkv, new_gdn_conv = causal_conv(dqkv, s_gdn_conv, gdn_conv_w)
    qkv = jax.nn.silu(qkv.astype(f32))
    q, k, v = split_cols(qkv, (GDN_HEADS * GDN_DK, GDN_HEADS * GDN_DK, GDN_HEADS * GDN_DV))
    q = l2norm(q.reshape(nb, L, GDN_HEADS, GDN_DK)) * (GDN_DK ** -0.5)
    k = l2norm(k.reshape(nb, L, GDN_HEADS, GDN_DK))
    v = v.reshape(nb, L, GDN_HEADS, GDN_DV)
    beta = jax.nn.sigmoid(db.astype(f32))
    g = -jnp.exp(gdn_a_log.astype(f32)) * jax.nn.softplus(da.astype(f32) + gdn_dt_bias.astype(f32))
    o, new_gdn = gdn_chunked(q, k, v, g, beta, s_gdn.astype(f32))
    o = rmsnorm(o, gdn_norm) * jax.nn.silu(dz.astype(f32)).reshape(nb, L, GDN_HEADS, GDN_DV)
    out = jnp.concatenate([o_c, o.reshape(nb, L, GDN_HEADS * GDN_DV)], axis=-1).astype(h.dtype) @ l1_w_out
    return out, new_lru.astype(h.dtype), new_lru_conv, new_gdn.astype(h.dtype), new_gdn_conv


def setup_inputs(seed: int = 0) -> dict:
    key = jax.random.key(seed)
    ks = list(jax.random.split(key, 40))
    f32 = jnp.float32

    def nrm(shape, scale):
        return jax.random.normal(ks.pop(), shape, f32) * scale

    def gain(shape):
        return 1.0 + nrm(shape, 0.02)

    def unif(shape, lo, hi):
        return jax.random.uniform(ks.pop(), shape, f32, lo, hi)

    def dt_bias(n):
        dt = jnp.exp(unif((n,), math.log(1e-3), math.log(1e-1)))
        return dt + jnp.log(-jnp.expm1(-dt))

    x_prompt = nrm((BATCH, SEQ, D_MODEL), 1.0)
    x_sample = nrm((DEC_BATCH, DEC_SEQ, D_MODEL), 1.0)
    state_gla = nrm((DEC_BATCH, GLA_HEADS, GLA_DK, GLA_DV), 0.1)
    state_ssd = nrm((DEC_BATCH, SSD_HEADS, SSD_HEAD_DIM, SSD_D_STATE), 0.1)
    state_ssd_conv = nrm((DEC_BATCH, CONV_W - 1, SSD_CONV_DIM), 1.0)
    state_lru = nrm((DEC_BATCH, LRU_WIDTH), 0.5)
    state_lru_conv = nrm((DEC_BATCH, CONV_W - 1, LRU_WIDTH), 1.0)
    state_gdn = nrm((DEC_BATCH, GDN_HEADS, GDN_DK, GDN_DV), 0.1)
    state_gdn_conv = nrm((DEC_BATCH, CONV_W - 1, GDN_CONV_DIM), 1.0)
    norms = gain((DEPTH, 3, D_MODEL))
    final_norm = gain((D_MODEL,))
    ffn_w_in = nrm((DEPTH, 2, D_MODEL, 2 * D_FF), D_MODEL ** -0.5)
    ffn_w_out = nrm((DEPTH, 2, D_FF, D_MODEL), D_FF ** -0.5)
    l0_w_in = nrm((D_MODEL, L0_IN), D_MODEL ** -0.5)
    l0_w_out = nrm((L0_MIX, D_MODEL), L0_MIX ** -0.5)
    gla_w_gate2 = nrm((GLA_RANK, GLA_HEADS * GLA_DK), GLA_RANK ** -0.5)
    gla_b_gate = nrm((GLA_HEADS * GLA_DK,), 0.01)
    gla_norm = gain((GLA_DV,))
    ssd_conv_w = nrm((CONV_W, SSD_CONV_DIM), CONV_W ** -0.5)
    ssd_conv_b = nrm((SSD_CONV_DIM,), 0.01)
    ssd_dt_bias = dt_bias(SSD_HEADS)
    ssd_a_log = jnp.log(unif((SSD_HEADS,), 1.0, 16.0))
    ssd_d = gain((SSD_HEADS,))
    ssd_norm = gain((SSD_D_INNER,))
    l1_w_in = nrm((D_MODEL, L1_IN), D_MODEL ** -0.5)
    l1_w_out = nrm((L1_MIX, D_MODEL), L1_MIX ** -0.5)
    lru_conv_w = nrm((CONV_W, LRU_WIDTH), CONV_W ** -0.5)
    lru_conv_b = nrm((LRU_WIDTH,), 0.01)
    lru_w_a = nrm((LRU_BLOCKS, LRU_BLOCK_W, LRU_BLOCK_W), LRU_BLOCK_W ** -0.5)
    lru_b_a = nrm((LRU_WIDTH,), 0.01)
    lru_w_x = nrm((LRU_BLOCKS, LRU_BLOCK_W, LRU_BLOCK_W), LRU_BLOCK_W ** -0.5)
    lru_b_x = nrm((LRU_WIDTH,), 0.01)
    s = unif((LRU_WIDTH,), 0.9, 0.999) ** (1.0 / LRU_C)
    lru_lambda = jnp.log(s) - jnp.log1p(-s)
    gdn_conv_w = nrm((CONV_W, GDN_CONV_DIM), CONV_W ** -0.5)
    gdn_a_log = jnp.log(unif((GDN_HEADS,), 1.0, 16.0))
    gdn_dt_bias = dt_bias(GDN_HEADS)
    gdn_norm = gain((GDN_DV,))
    return {
        'x_prompt': x_prompt, 'x_sample': x_sample,
        'state_gla': state_gla, 'state_ssd': state_ssd, 'state_ssd_conv': state_ssd_conv,
        'state_lru': state_lru, 'state_lru_conv': state_lru_conv,
        'state_gdn': state_gdn, 'state_gdn_conv': state_gdn_conv,
        'norms': norms, 'final_norm': final_norm, 'ffn_w_in': ffn_w_in, 'ffn_w_out': ffn_w_out,
        'l0_w_in': l0_w_in, 'l0_w_out': l0_w_out,
        'gla_w_gate2': gla_w_gate2, 'gla_b_gate': gla_b_gate, 'gla_norm': gla_norm,
        'ssd_conv_w': ssd_conv_w, 'ssd_conv_b': ssd_conv_b, 'ssd_dt_bias': ssd_dt_bias,
        'ssd_a_log': ssd_a_log, 'ssd_d': ssd_d, 'ssd_norm': ssd_norm,
        'l1_w_in': l1_w_in, 'l1_w_out': l1_w_out,
        'lru_conv_w': lru_conv_w, 'lru_conv_b': lru_conv_b, 'lru_w_a': lru_w_a, 'lru_b_a': lru_b_a,
        'lru_w_x': lru_w_x, 'lru_b_x': lru_b_x, 'lru_lambda': lru_lambda,
        'gdn_conv_w': gdn_conv_w, 'gdn_a_log': gdn_a_log, 'gdn_dt_bias': gdn_dt_bias, 'gdn_norm': gdn_norm,
    }


def reference(x_prompt, x_sample, state_gla, state_ssd, state_ssd_conv, state_lru, state_lru_conv,
              state_gdn, state_gdn_conv, norms, final_norm, ffn_w_in, ffn_w_out, l0_w_in, l0_w_out,
              gla_w_gate2, gla_b_gate, gla_norm, ssd_conv_w, ssd_conv_b, ssd_dt_bias, ssd_a_log, ssd_d,
              ssd_norm, l1_w_in, l1_w_out, lru_conv_w, lru_conv_b, lru_w_a, lru_b_a, lru_w_x, lru_b_x,
              lru_lambda, gdn_conv_w, gdn_a_log, gdn_dt_bias, gdn_norm):
    def trunk(x, s_gla, s_ssd, s_ssd_conv, s_lru, s_lru_conv, s_gdn, s_gdn_conv):
        h = x
        for layer in range(DEPTH):
            h = h + FFN_RES * swiglu(rmsnorm(h, norms[layer, 0]), ffn_w_in[layer, 0], ffn_w_out[layer, 0])
            hn = rmsnorm(h, norms[layer, 1])
            if layer % 2 == 0:
                mix, s_gla, s_ssd, s_ssd_conv = mixer_ab(
                    hn, s_gla, s_ssd, s_ssd_conv, l0_w_in, l0_w_out, gla_w_gate2, gla_b_gate, gla_norm,
                    ssd_conv_w, ssd_conv_b, ssd_dt_bias, ssd_a_log, ssd_d, ssd_norm)
            else:
                mix, s_lru, s_lru_conv, s_gdn, s_gdn_conv = mixer_cd(
                    hn, s_lru, s_lru_conv, s_gdn, s_gdn_conv, l1_w_in, l1_w_out, lru_conv_w, lru_conv_b,
                    lru_w_a, lru_b_a, lru_w_x, lru_b_x, lru_lambda, gdn_conv_w, gdn_a_log, gdn_dt_bias, gdn_norm)
            h = h + mix
            h = h + FFN_RES * swiglu(rmsnorm(h, norms[layer, 2]), ffn_w_in[layer, 1], ffn_w_out[layer, 1])
        return rmsnorm(h, final_norm), s_gla, s_ssd, s_ssd_conv, s_lru, s_lru_conv, s_gdn, s_gdn_conv

    nbp = x_prompt.shape[0]

    def empty(s):
        return jnp.zeros((nbp,) + s.shape[1:], x_prompt.dtype)

    (y_prompt, p_gla, p_ssd, p_ssd_conv, p_lru, p_lru_conv, p_gdn, p_gdn_conv) = trunk(
        x_prompt, empty(state_gla), empty(state_ssd), empty(state_ssd_conv), empty(state_lru),
        empty(state_lru_conv), empty(state_gdn), empty(state_gdn_conv))
    (y_sample, s_gla, s_ssd, s_ssd_conv, s_lru, s_lru_conv, s_gdn, s_gdn_conv) = trunk(
        x_sample, state_gla, state_ssd, state_ssd_conv, state_lru, state_lru_conv, state_gdn, state_gdn_conv)
    return (y_prompt, y_sample, p_gla, p_ssd, p_ssd_conv, p_lru, p_lru_conv, p_gdn, p_gdn_conv,
            s_gla, s_ssd, s_ssd_conv, s_lru, s_lru_conv, s_gdn, s_gdn_conv)
```

```python
import functools
import math

import jax
import jax.numpy as jnp
from jax import lax
from jax.experimental import pallas as pl
from jax.experimental.pallas import tpu as pltpu

F32 = jnp.float32
BF16 = jnp.bfloat16

D_MODEL = 1024
NORM_EPS = 1e-6
CONV_W = 4
D_FF = 2816
FFN_RES = 0.5

GLA_HEADS = 4
GLA_DK = 128
GLA_DV = 256
GLA_RANK = 16
GLA_GATE_NORM = 16.0
GLA_SUB = 16

SSD_D_INNER = 2048
SSD_HEAD_DIM = 64
SSD_HEADS = 32
SSD_GROUPS = 4
SSD_HPG = 8
SSD_D_STATE = 128
SSD_GROUP_W = SSD_HPG * SSD_HEAD_DIM

LRU_WIDTH = 1280
LRU_BLOCKS = 10
LRU_BLOCK_W = 128
LRU_C = 8.0

GDN_HEADS = 8
GDN_DK = 128
GDN_DV = 128

LANES = 128
SUBLANES = 8
NEG_BIG = -1e30
VMEM_LIMIT = 56 * 1024 * 1024

L0_W = 8320
L0_Z, L0_X, L0_GV, L0_GG, L0_GQ, L0_GK, L0_B, L0_C, L0_SM = 0, 2048, 4096, 5120, 6144, 6656, 7168, 7680, 8192
L1_W = 7168
L1_RG, L1_RX, L1_SM, L1_Q, L1_K, L1_V, L1_Z = 0, 1280, 2560, 3072, 4096, 5120, 6144


def _sigmoid(x):
    return 1.0 / (1.0 + jnp.exp(-x))


def _silu(x):
    return x * _sigmoid(x)


def _softplus(x):
    return jnp.maximum(x, 0.0) + jnp.log(1.0 + jnp.exp(-jnp.abs(x)))


def _rms(x, g):
    return x * lax.rsqrt(jnp.mean(x * x, axis=-1, keepdims=True) + NORM_EPS) * g


def _dot(a, b):
    return jnp.dot(a, b, preferred_element_type=F32)


def _dot_nt(a, b):
    return lax.dot_general(a, b, (((1,), (1,)), ((), ())), preferred_element_type=F32)


def _dot_tn(a, b):
    return lax.dot_general(a, b, (((0,), (0,)), ((), ())), preferred_element_type=F32)


def _split2(x):
    hi = x.astype(BF16)
    lo = (x - hi.astype(F32)).astype(BF16)
    return hi, lo


def _split3(x):
    hi = x.astype(BF16)
    r = x - hi.astype(F32)
    mid = r.astype(BF16)
    lo = (r - mid.astype(F32)).astype(BF16)
    return hi, mid, lo


def _dot01_exact(m01, x):
    hi, mid, lo = _split3(x)
    return _dot(m01, hi) + _dot(m01, mid) + _dot(m01, lo)


def _dot_x01(x, m01):
    hi, lo = _split2(x)
    return _dot(hi, m01) + _dot(lo, m01)


def _dot_f32(a, b):
    ah, al = _split2(a)
    bh, bl = _split2(b)
    return _dot(ah, bh) + _dot(ah, bl) + _dot(al, bh)


def _lower_mask(n, strict=False):
    row = lax.broadcasted_iota(jnp.int32, (n, n), 0)
    col = lax.broadcasted_iota(jnp.int32, (n, n), 1)
    return row > col if strict else row >= col


def _params(sem):
    return pltpu.CompilerParams(dimension_semantics=sem, vmem_limit_bytes=VMEM_LIMIT)


def _ffn_body(*refs, nf, final):
    if final:
        x_ref, g_ref, wg_ref, wu_ref, wo_ref, fg_ref, o_ref, hn_ref, acc_ref = refs
    else:
        x_ref, g_ref, wg_ref, wu_ref, wo_ref, o_ref, hn_ref, acc_ref = refs
        fg_ref = None
    j = pl.program_id(1)

    @pl.when(j == 0)
    def _():
        hn_ref[...] = _rms(x_ref[...], g_ref[...]).astype(BF16)

    hn = hn_ref[...]
    gate = _dot(hn, wg_ref[...])
    up = _dot(hn, wu_ref[...])
    part = _dot((_silu(gate) * up).astype(BF16), wo_ref[...])

    @pl.when(j == 0)
    def _():
        acc_ref[...] = part

    @pl.when(j > 0)
    def _():
        acc_ref[...] += part

    @pl.when(j == nf - 1)
    def _():
        y = x_ref[...] + FFN_RES * acc_ref[...]
        if final:
            y = _rms(y, fg_ref[...])
        o_ref[...] = y


def _ffn(x, g, w_in, w_out, final_g=None, *, tm, tf):
    t = x.shape[0]
    nf = D_FF // tf
    final = final_g is not None
    in_specs = [
        pl.BlockSpec((tm, D_MODEL), lambda i, j: (i, 0)),
        pl.BlockSpec((1, D_MODEL), lambda i, j: (0, 0)),
        pl.BlockSpec((D_MODEL, tf), lambda i, j: (0, j)),
        pl.BlockSpec((D_MODEL, tf), lambda i, j: (0, j + nf)),
        pl.BlockSpec((tf, D_MODEL), lambda i, j: (j, 0)),
    ]
    args = [x, g.reshape(1, D_MODEL), w_in, w_in, w_out]
    if final:
        in_specs.append(pl.BlockSpec((1, D_MODEL), lambda i, j: (0, 0)))
        args.append(final_g.reshape(1, D_MODEL))
    return pl.pallas_call(
        functools.partial(_ffn_body, nf=nf, final=final),
        grid=(t // tm, nf),
        in_specs=in_specs,
        out_specs=pl.BlockSpec((tm, D_MODEL), lambda i, j: (i, 0)),
        out_shape=jax.ShapeDtypeStruct((t, D_MODEL), F32),
        scratch_shapes=[pltpu.VMEM((tm, D_MODEL), BF16), pltpu.VMEM((tm, D_MODEL), F32)],
        compiler_params=_params(("parallel", "arbitrary")),
        name="ffn",
    )(*args)


def _proj_body(x_ref, g_ref, w_ref, o_ref, hn_ref):
    @pl.when(pl.program_id(1) == 0)
    def _():
        hn_ref[...] = _rms(x_ref[...], g_ref[...]).astype(BF16)

    o_ref[...] = _dot(hn_ref[...], w_ref[...])


def _proj(x, g, w, *, tm, tn):
    t, n = x.shape[0], w.shape[1]
    return pl.pallas_call(
        _proj_body,
        grid=(t // tm, n // tn),
        in_specs=[
            pl.BlockSpec((tm, D_MODEL), lambda i, j: (i, 0)),
            pl.BlockSpec((1, D_MODEL), lambda i, j: (0, 0)),
            pl.BlockSpec((D_MODEL, tn), lambda i, j: (0, j)),
        ],
        out_specs=pl.BlockSpec((tm, tn), lambda i, j: (i, j)),
        out_shape=jax.ShapeDtypeStruct((t, n), F32),
        scratch_shapes=[pltpu.VMEM((tm, D_MODEL), BF16)],
        compiler_params=_params(("parallel", "arbitrary")),
        name="proj",
    )(x, g.reshape(1, D_MODEL), w)


def _outproj_body(r_ref, a_ref, b_ref, wa_ref, wb_ref, o_ref):
    o_ref[...] = (r_ref[...] + _dot(a_ref[...].astype(BF16), wa_ref[...])
                  + _dot(b_ref[...].astype(BF16), wb_ref[...]))


def _outproj(res, a, b, wa, wb, *, tm):
    t = res.shape[0]
    ka, kb = a.shape[1], b.shape[1]
    return pl.pallas_call(
        _outproj_body,
        grid=(t // tm,),
        in_specs=[
            pl.BlockSpec((tm, D_MODEL), lambda i: (i, 0)),
            pl.BlockSpec((tm, ka), lambda i: (i, 0)),
            pl.BlockSpec((tm, kb), lambda i: (i, 0)),
            pl.BlockSpec((ka, D_MODEL), lambda i: (0, 0)),
            pl.BlockSpec((kb, D_MODEL), lambda i: (0, 0)),
        ],
        out_specs=pl.BlockSpec((tm, D_MODEL), lambda i: (i, 0)),
        out_shape=jax.ShapeDtypeStruct((t, D_MODEL), F32),
        compiler_params=_params(("parallel",)),
        name="outproj",
    )(res, a, b, wa, wb)


def _conv_chunk(buf, src, w_ref, bias_ref, c):
    buf[pl.ds(SUBLANES, c), :] = src
    acc = w_ref[0:1, :] * buf[pl.ds(5, c), :]
    for j in range(1, CONV_W):
        acc = acc + w_ref[j:j + 1, :] * buf[pl.ds(5 + j, c), :]
    if bias_ref is not None:
        acc = acc + bias_ref[...]
    buf[pl.ds(5, 3), :] = buf[pl.ds(c + 5, 3), :]
    return acc


def _gla_body(q_ref, k_ref, v_ref, gg_ref, sm_ref, wg2_ref, bg_ref, nw_ref, o_ref, s_out, st_ref, b_ref, *, lg, nc):
    c = pl.program_id(1)

    @pl.when(c == 0)
    def _():
        st_ref[...] = jnp.zeros(st_ref.shape, F32)

    pre = _dot(sm_ref[0].astype(BF16), wg2_ref[...]) + bg_ref[...]
    log_a = -_softplus(-pre) * (1.0 / GLA_GATE_NORM)
    row = lax.broadcasted_iota(jnp.int32, (lg, lg), 0)
    col = lax.broadcasted_iota(jnp.int32, (lg, lg), 1)
    tri = jnp.where((row >= col) & ((row // GLA_SUB) == (col // GLA_SUB)), 1.0, 0.0).astype(BF16)
    b_ref[...] = _dot01_exact(tri, log_a)

    rowi = lax.broadcasted_iota(jnp.int32, (GLA_SUB, 1), 0)
    scale = GLA_DK ** -0.5

    def block(i, carry):
        base = pl.multiple_of(i * GLA_SUB, GLA_SUB)
        q = q_ref[0, pl.ds(base, GLA_SUB), :] * scale
        k = k_ref[0, pl.ds(base, GLA_SUB), :]
        v = v_ref[0, pl.ds(base, GLA_SUB), :]
        b = b_ref[pl.ds(base, GLA_SUB), :]
        o = [jnp.zeros((GLA_SUB, GLA_DV), F32) for _ in range(GLA_HEADS)]
        for s in range(GLA_SUB):
            ks = k_ref[0, pl.ds(base + s, 1), :]
            bs = b_ref[pl.ds(base + s, 1), :]
            vs = v_ref[0, pl.ds(base + s, 1), :]
            p = q * ks * jnp.exp(b - bs)
            for h in range(GLA_HEADS):
                att = jnp.sum(p[:, h * GLA_DK:(h + 1) * GLA_DK], axis=-1, keepdims=True)
                att = jnp.where(rowi >= s, att, 0.0)
                o[h] = o[h] + att * vs[:, h * GLA_DV:(h + 1) * GLA_DV]
        qe = (q * jnp.exp(b)).astype(BF16)
        blast = b_ref[pl.ds(base + GLA_SUB - 1, 1), :]
        kd = (k * jnp.exp(blast - b)).astype(BF16)
        elast = jnp.exp(blast)
        vb = v.astype(BF16)
        gg = gg_ref[0, pl.ds(base, GLA_SUB), :]
        for h in range(GLA_HEADS):
            dk = slice(h * GLA_DK, (h + 1) * GLA_DK)
            dv = slice(h * GLA_DV, (h + 1) * GLA_DV)
            st = st_ref[h]
            oh = o[h] + _dot_nt(qe[:, dk], st.astype(BF16))
            st_ref[h] = st * elast[:, dk] + _dot_tn(vb[:, dv], kd[:, dk])
            o_ref[0, pl.ds(base, GLA_SUB), dv] = _rms(oh, nw_ref[...]) * _silu(gg[:, dv])
        return carry

    lax.fori_loop(0, lg // GLA_SUB, block, 0)

    @pl.when(c == nc - 1)
    def _():
        for h in range(GLA_HEADS):
            s_out[0, h] = st_ref[h].T


def _gla_prompt(p0, wg2, bgate, normw, *, lg):
    nb, l = p0.shape[0], p0.shape[1]
    nc = l // lg

    def seg(width, off):
        return pl.BlockSpec((1, lg, width), lambda b, c: (b, c, off // width))

    def full(shape):
        return pl.BlockSpec(shape, lambda b, c: (0,) * len(shape))

    return pl.pallas_call(
        functools.partial(_gla_body, lg=lg, nc=nc),
        grid=(nb, nc),
        in_specs=[seg(512, L0_GQ), seg(512, L0_GK), seg(1024, L0_GV), seg(1024, L0_GG), seg(LANES, L0_SM),
                  full((LANES, 512)), full((1, 512)), full((1, GLA_DV))],
        out_specs=[pl.BlockSpec((1, lg, 1024), lambda b, c: (b, c, 0)),
                   pl.BlockSpec((1, GLA_HEADS, GLA_DK, GLA_DV), lambda b, c: (b, 0, 0, 0))],
        out_shape=[jax.ShapeDtypeStruct((nb, l, 1024), F32),
                   jax.ShapeDtypeStruct((nb, GLA_HEADS, GLA_DK, GLA_DV), F32)],
        scratch_shapes=[pltpu.VMEM((GLA_HEADS, GLA_DV, GLA_DK), F32), pltpu.VMEM((lg, 512), F32)],
        compiler_params=_params(("parallel", "arbitrary")),
        name="gla_prompt",
    )(p0, p0, p0, p0, p0, wg2, bgate, normw)


def _ssd_body(z_ref, x_ref, b_ref, c_ref, sm_ref, cwx, cbx, cwb, cbb, cwc, cbc, dtb_ref, alog_ref, e_ref, et_ref,
              dvec_ref, nw_ref, y_ref, s_out, cx_out, cb_out, cc_out, s_ref, bufx, bufb, bufc, *, cs, nc):
    c = pl.program_id(1)

    @pl.when(c == 0)
    def _():
        s_ref[...] = jnp.zeros(s_ref.shape, F32)
        bufx[pl.ds(0, SUBLANES), :] = jnp.zeros((SUBLANES, bufx.shape[1]), F32)
        bufb[pl.ds(0, SUBLANES), :] = jnp.zeros((SUBLANES, bufb.shape[1]), F32)
        bufc[pl.ds(0, SUBLANES), :] = jnp.zeros((SUBLANES, bufc.shape[1]), F32)

    xs = _silu(_conv_chunk(bufx, x_ref[0], cwx, cbx, cs))
    bm = _silu(_conv_chunk(bufb, b_ref[0], cwb, cbb, cs)).astype(BF16)
    cm = _silu(_conv_chunk(bufc, c_ref[0], cwc, cbc, cs)).astype(BF16)

    dt = _softplus(sm_ref[0] + dtb_ref[...])
    da = dt * (-jnp.exp(alog_ref[...]))
    lower = _lower_mask(cs)
    tri = jnp.where(lower, 1.0, 0.0).astype(BF16)
    cum = _dot01_exact(tri, da)
    cum_t = cum.T
    dt_t = dt.T
    last = cum[cs - 1:cs, :]
    ecum_x = _dot_x01(jnp.exp(cum), e_ref[...])
    w_x = _dot_x01(jnp.exp(last - cum) * dt, e_ref[...])
    elast = jnp.broadcast_to(jnp.exp(last), (SUBLANES, LANES))
    eh, el = _split2(elast)
    ecol = _dot_nt(et_ref[...], eh) + _dot_nt(et_ref[...], el)

    xs_bf = xs.astype(BF16)
    xw = (xs * w_x).astype(BF16)
    z = z_ref[0]
    lane = lax.broadcasted_iota(jnp.int32, (cs, LANES), 1)
    for g in range(SSD_GROUPS):
        gn = slice(g * SSD_D_STATE, (g + 1) * SSD_D_STATE)
        gw = slice(g * SSD_GROUP_W, (g + 1) * SSD_GROUP_W)
        cg, bg = cm[:, gn], bm[:, gn]
        cb = _dot_nt(cg, bg)
        sg = s_ref[pl.ds(g * SSD_GROUP_W, SSD_GROUP_W), :]
        y_inter = _dot_nt(cg, sg.astype(BF16)) * ecum_x[:, gw]
        ss = jnp.zeros((cs, 1), F32)
        for p in range(SSD_HPG // 2):
            pair = slice(g * SSD_GROUP_W + p * LANES, g * SSD_GROUP_W + (p + 1) * LANES)
            xp = xs_bf[:, pair]
            ys = []
            for h in (g * SSD_HPG + 2 * p, g * SSD_HPG + 2 * p + 1):
                seg = jnp.exp(jnp.where(lower, cum[:, h:h + 1] - cum_t[h:h + 1, :], NEG_BIG))
                m = cb * seg * dt_t[h:h + 1, :]
                ys.append(_dot(m.astype(BF16), xp))
            yp = jnp.where(lane < SSD_HEAD_DIM, ys[0], ys[1])
            yp = yp + y_inter[:, p * LANES:(p + 1) * LANES] + dvec_ref[:, pair] * xs[:, pair]
            yp = yp * _silu(z[:, pair])
            ss = ss + jnp.sum(yp * yp, axis=-1, keepdims=True)
            y_ref[0, :, pair] = yp
        inv = lax.rsqrt(ss * (1.0 / SSD_GROUP_W) + NORM_EPS)
        y_ref[0, :, gw] = y_ref[0, :, gw] * inv * nw_ref[:, gw]
        s_ref[pl.ds(g * SSD_GROUP_W, SSD_GROUP_W), :] = (
            ecol[g * SSD_GROUP_W:(g + 1) * SSD_GROUP_W, 0:1] * sg + _dot_tn(xw[:, gw], bg))

    @pl.when(c == nc - 1)
    def _():
        s_out[0] = s_ref[...]
        cx_out[0] = bufx[pl.ds(5, 3), :]
        cb_out[0] = bufb[pl.ds(5, 3), :]
        cc_out[0] = bufc[pl.ds(5, 3), :]


def _ssd_prompt(p0, cw, cb, dtb, alog, emat, emat_t, dvec, normw, *, cs):
    nb, l = p0.shape[0], p0.shape[1]
    nc = l // cs

    def seg(width, off):
        return pl.BlockSpec((1, cs, width), lambda b, c: (b, c, off // width))

    def full(shape):
        return pl.BlockSpec(shape, lambda b, c: (0,) * len(shape))

    def tail(width):
        return pl.BlockSpec((1, CONV_W - 1, width), lambda b, c: (b, 0, 0))

    cwx, cwb, cwc = cw[:, :2048], cw[:, 2048:2560], cw[:, 2560:]
    cbx, cbb, cbc = cb[:, :2048], cb[:, 2048:2560], cb[:, 2560:]
    return pl.pallas_call(
        functools.partial(_ssd_body, cs=cs, nc=nc),
        grid=(nb, nc),
        in_specs=[seg(2048, L0_Z), seg(2048, L0_X), seg(512, L0_B), seg(512, L0_C), seg(LANES, L0_SM),
                  full((CONV_W, 2048)), full((1, 2048)), full((CONV_W, 512)), full((1, 512)),
                  full((CONV_W, 512)), full((1, 512)), full((1, LANES)), full((1, LANES)),
                  full((LANES, 2048)), full((2048, LANES)), full((1, 2048)), full((1, 2048))],
        out_specs=[pl.BlockSpec((1, cs, 2048), lambda b, c: (b, c, 0)),
                   pl.BlockSpec((1, 2048, SSD_D_STATE), lambda b, c: (b, 0, 0)),
                   tail(2048), tail(512), tail(512)],
        out_shape=[jax.ShapeDtypeStruct((nb, l, 2048), F32),
                   jax.ShapeDtypeStruct((nb, 2048, SSD_D_STATE), F32),
                   jax.ShapeDtypeStruct((nb, CONV_W - 1, 2048), F32),
                   jax.ShapeDtypeStruct((nb, CONV_W - 1, 512), F32),
                   jax.ShapeDtypeStruct((nb, CONV_W - 1, 512), F32)],
        scratch_shapes=[pltpu.VMEM((2048, SSD_D_STATE), F32),
                        pltpu.VMEM((cs + SUBLANES, 2048), F32),
                        pltpu.VMEM((cs + SUBLANES, 512), F32),
                        pltpu.VMEM((cs + SUBLANES, 512), F32)],
        compiler_params=_params(("parallel", "arbitrary")),
        name="ssd_prompt",
    )(p0, p0, p0, p0, p0, cwx, cbx, cwb, cbb, cwc, cbc, dtb, alog, emat, emat_t, dvec, normw)


def _lru_gates(xc, wa_ref, ba_ref, wx_ref, bx_ref, lam_ref):
    xb = xc.astype(BF16)
    rl, il = [], []
    for n in range(LRU_BLOCKS):
        blk = xb[:, n * LRU_BLOCK_W:(n + 1) * LRU_BLOCK_W]
        rl.append(_dot(blk, wa_ref[n]))
        il.append(_dot(blk, wx_ref[n]))
    r_logit = jnp.concatenate(rl, axis=-1) + ba_ref[...]
    i_logit = jnp.concatenate(il, axis=-1) + bx_ref[...]
    log_a = -LRU_C * _sigmoid(r_logit) * _softplus(-lam_ref[...])
    a = jnp.exp(log_a)
    b = jnp.sqrt(1.0 - jnp.exp(2.0 * log_a)) * (_sigmoid(i_logit) * xc)
    return a, b


def _gelu_tanh(x):
    return 0.5 * x * (1.0 + jnp.tanh(math.sqrt(2.0 / math.pi) * (x + 0.044715 * (x * x * x))))


def _lru_body(rg_ref, rx_ref, cw, cbias, wa_ref, ba_ref, wx_ref, bx_ref, lam_ref, o_ref, h_out, c_out,
              h_ref, buf, a_scr, b_scr, *, cs, nc):
    c = pl.program_id(1)

    @pl.when(c == 0)
    def _():
        h_ref[...] = jnp.zeros(h_ref.shape, F32)
        buf[pl.ds(0, SUBLANES), :] = jnp.zeros((SUBLANES, LRU_WIDTH), F32)

    xc = _conv_chunk(buf, rx_ref[0], cw, cbias, cs)
    a, b = _lru_gates(xc, wa_ref, ba_ref, wx_ref, bx_ref, lam_ref)
    rowm = lax.broadcasted_iota(jnp.int32, (cs, LRU_WIDTH), 0) % SUBLANES
    for sh in (1, 2, 4):
        keep = rowm >= sh
        a_prev = jnp.where(keep, pltpu.roll(a, sh, 0), 1.0)
        b_prev = jnp.where(keep, pltpu.roll(b, sh, 0), 0.0)
        b = a * b_prev + b
        a = a * a_prev
    a_scr[...] = a
    b_scr[...] = b
    h = h_ref[...]
    for t in range(cs // SUBLANES):
        rows = pl.ds(t * SUBLANES, SUBLANES)
        ht = a_scr[rows, :] * h + b_scr[rows, :]
        o_ref[0, rows, :] = _gelu_tanh(rg_ref[0, rows, :]) * ht
        h = jnp.broadcast_to(ht[SUBLANES - 1:SUBLANES, :], (SUBLANES, LRU_WIDTH))
    h_ref[...] = h

    @pl.when(c == nc - 1)
    def _():
        h_out[0] = h[0:1, :]
        c_out[0] = buf[pl.ds(5, 3), :]


def _lru_prompt(p1, cw, cbias, wa, ba, wx, bx, lam, *, cs):
    nb, l = p1.shape[0], p1.shape[1]
    nc = l // cs

    def full(shape):
        return pl.BlockSpec(shape, lambda b, c: (0,) * len(shape))

    return pl.pallas_call(
        functools.partial(_lru_body, cs=cs, nc=nc),
        grid=(nb, nc),
        in_specs=[pl.BlockSpec((1, cs, LRU_WIDTH), lambda b, c: (b, c, L1_RG // LRU_WIDTH)),
                  pl.BlockSpec((1, cs, LRU_WIDTH), lambda b, c: (b, c, L1_RX // LRU_WIDTH)),
                  full((CONV_W, LRU_WIDTH)), full((1, LRU_WIDTH)),
                  full((LRU_BLOCKS, LRU_BLOCK_W, LRU_BLOCK_W)), full((1, LRU_WIDTH)),
                  full((LRU_BLOCKS, LRU_BLOCK_W, LRU_BLOCK_W)), full((1, LRU_WIDTH)), full((1, LRU_WIDTH))],
        out_specs=[pl.BlockSpec((1, cs, LRU_WIDTH), lambda b, c: (b, c, 0)),
                   pl.BlockSpec((1, 1, LRU_WIDTH), lambda b, c: (b, 0, 0)),
                   pl.BlockSpec((1, CONV_W - 1, LRU_WIDTH), lambda b, c: (b, 0, 0))],
        out_shape=[jax.ShapeDtypeStruct((nb, l, LRU_WIDTH), F32),
                   jax.ShapeDtypeStruct((nb, 1, LRU_WIDTH), F32),
                   jax.ShapeDtypeStruct((nb, CONV_W - 1, LRU_WIDTH), F32)],
        scratch_shapes=[pltpu.VMEM((SUBLANES, LRU_WIDTH), F32),
                        pltpu.VMEM((cs + SUBLANES, LRU_WIDTH), F32),
                        pltpu.VMEM((cs, LRU_WIDTH), F32),
                        pltpu.VMEM((cs, LRU_WIDTH), F32)],
        compiler_params=_params(("parallel", "arbitrary")),
        name="lru_prompt",
    )(p1, p1, cw, cbias, wa, ba, wx, bx, lam)


def _l2norm(x):
    return x * lax.rsqrt(jnp.sum(x * x, axis=-1, keepdims=True) + NORM_EPS)


def _gdn_body(q_ref, k_ref, v_ref, z_ref, sm_ref, cwq, cwk, cwv, alog_ref, dtb_ref, nw_ref,
              o_ref, s_out, cq_out, ck_out, cv_out, s_ref, bufq, bufk, bufv, *, cs, nc):
    c = pl.program_id(1)

    @pl.when(c == 0)
    def _():
        s_ref[...] = jnp.zeros(s_ref.shape, F32)
        for buf in (bufq, bufk, bufv):
            buf[pl.ds(0, SUBLANES), :] = jnp.zeros((SUBLANES, buf.shape[1]), F32)

    q = _silu(_conv_chunk(bufq, q_ref[0], cwq, None, cs))
    k = _silu(_conv_chunk(bufk, k_ref[0], cwk, None, cs))
    v = _silu(_conv_chunk(bufv, v_ref[0], cwv, None, cs))
    sm = sm_ref[0]
    g_all = -jnp.exp(alog_ref[...]) * _softplus(sm + dtb_ref[...])
    beta_all = _sigmoid(sm)
    lower = _lower_mask(cs)
    strict = _lower_mask(cs, strict=True)
    eye = jnp.where(lower & jnp.logical_not(strict), 1.0, 0.0)
    tri = jnp.where(lower, 1.0, 0.0).astype(BF16)
    gc = _dot01_exact(tri, g_all)
    gc_t = gc.T
    z = z_ref[0]
    for h in range(GDN_HEADS):
        hd = slice(h * GDN_DK, (h + 1) * GDN_DK)
        qh = _l2norm(q[:, hd]) * (GDN_DK ** -0.5)
        kh = _l2norm(k[:, hd])
        vh = v[:, hd]
        beta = beta_all[:, GDN_HEADS + h:GDN_HEADS + h + 1]
        gcol = gc[:, h:h + 1]
        dec = jnp.exp(jnp.where(lower, gcol - gc_t[h:h + 1, :], NEG_BIG))
        qb, kb = qh.astype(BF16), kh.astype(BF16)
        kk = _dot_nt(kb, kb)
        qk = _dot_nt(qb, kb)
        p = -(beta * jnp.where(strict, dec, 0.0) * kk)
        tinv = eye + p
        for _ in range(int(math.log2(cs)) - 1):
            p = _dot_f32(p, p)
            tinv = tinv + _dot_f32(tinv, p)
        egc = jnp.exp(gcol)
        rhs = jnp.concatenate([vh * beta, kh * (beta * egc)], axis=-1)
        sol = _dot_f32(tinv, rhs)
        u, w = sol[:, :GDN_DV], sol[:, GDN_DV:]
        glast = gc[cs - 1:cs, h:h + 1]
        kdec = (kh * jnp.exp(glast - gcol)).astype(BF16)
        sh = s_ref[h]
        sb = sh.astype(BF16)
        delta = u - _dot(w.astype(BF16), sb)
        db = delta.astype(BF16)
        o = egc * _dot(qb, sb) + _dot((dec * qk).astype(BF16), db)
        s_ref[h] = jnp.exp(glast) * sh + _dot_tn(kdec, db)
        o_ref[0, :, hd] = _rms(o, nw_ref[...]) * _silu(z[:, hd])

    @pl.when(c == nc - 1)
    def _():
        s_out[0] = s_ref[...]
        cq_out[0] = bufq[pl.ds(5, 3), :]
        ck_out[0] = bufk[pl.ds(5, 3), :]
        cv_out[0] = bufv[pl.ds(5, 3), :]


def _gdn_prompt(p1, cw, alog, dtb, normw, *, cs):
    nb, l = p1.shape[0], p1.shape[1]
    nc = l // cs

    def seg(width, off):
        return pl.BlockSpec((1, cs, width), lambda b, c: (b, c, off // width))

    def full(shape):
        return pl.BlockSpec(shape, lambda b, c: (0,) * len(shape))

    def tail(width):
        return pl.BlockSpec((1, CONV_W - 1, width), lambda b, c: (b, 0, 0))

    return pl.pallas_call(
        functools.partial(_gdn_body, cs=cs, nc=nc),
        grid=(nb, nc),
        in_specs=[seg(1024, L1_Q), seg(1024, L1_K), seg(1024, L1_V), seg(1024, L1_Z), seg(LANES, L1_SM),
                  full((CONV_W, 1024)), full((CONV_W, 1024)), full((CONV_W, 1024)),
                  full((1, LANES)), full((1, LANES)), full((1, GDN_DV))],
        out_specs=[pl.BlockSpec((1, cs, 1024), lambda b, c: (b, c, 0)),
                   pl.BlockSpec((1, GDN_HEADS, GDN_DK, GDN_DV), lambda b, c: (b, 0, 0, 0)),
                   tail(1024), tail(1024), tail(1024)],
        out_shape=[jax.ShapeDtypeStruct((nb, l, 1024), F32),
                   jax.ShapeDtypeStruct((nb, GDN_HEADS, GDN_DK, GDN_DV), F32)]
        + [jax.ShapeDtypeStruct((nb, CONV_W - 1, 1024), F32)] * 3,
        scratch_shapes=[pltpu.VMEM((GDN_HEADS, GDN_DK, GDN_DV), F32)]
        + [pltpu.VMEM((cs + SUBLANES, 1024), F32)] * 3,
        compiler_params=_params(("parallel", "arbitrary")),
        name="gdn_prompt",
    )(p1, p1, p1, p1, p1, cw[:, :1024], cw[:, 1024:2048], cw[:, 2048:], alog, dtb, normw)


def _pad_lanes(v, width, offset=0):
    return jnp.zeros((1, width), F32).at[0, offset:offset + v.shape[0]].set(v.astype(F32))


def _prep(ffn_w_in, ffn_w_out, l0_w_in, l0_w_out, gla_w_gate2, gla_b_gate, gla_norm, ssd_conv_w, ssd_conv_b,
          ssd_dt_bias, ssd_a_log, ssd_d, ssd_norm, l1_w_in, l1_w_out, lru_conv_w, lru_conv_b, lru_w_a, lru_b_a,
          lru_w_x, lru_b_x, lru_lambda, gdn_conv_w, gdn_a_log, gdn_dt_bias, gdn_norm):
    w = {}
    w["ffn_in"] = [[ffn_w_in[l, i].astype(BF16) for i in range(2)] for l in range(2)]
    w["ffn_out"] = [[ffn_w_out[l, i].astype(BF16) for i in range(2)] for l in range(2)]
    c = l0_w_in
    small0 = jnp.concatenate([c[:, 8208:8240], c[:, 3072:3088], jnp.zeros((D_MODEL, LANES - 48), F32)], axis=1)
    w["l0_in"] = jnp.concatenate(
        [c[:, 3088:5136], c[:, 5136:7184], c[:, 1024:2048], c[:, 2048:3072], c[:, 0:512], c[:, 512:1024],
         c[:, 7184:7696], c[:, 7696:8208], small0], axis=1).astype(BF16)
    w["l0_out_a"] = l0_w_out[:1024].astype(BF16)
    w["l0_out_b"] = l0_w_out[1024:].astype(BF16)
    w["gla_wg2"] = jnp.zeros((LANES, 512), F32).at[32:48].set(gla_w_gate2).astype(BF16)
    w["gla_bg"] = gla_b_gate.reshape(1, 512)
    w["gla_norm"] = gla_norm.reshape(1, GLA_DV)
    w["ssd_cw"] = ssd_conv_w
    w["ssd_cb"] = ssd_conv_b.reshape(1, -1)
    w["ssd_dtb"] = _pad_lanes(ssd_dt_bias, LANES)
    w["ssd_alog"] = _pad_lanes(ssd_a_log, LANES)
    head_of_lane = jnp.arange(SSD_D_INNER) // SSD_HEAD_DIM
    emat = (jnp.arange(LANES)[:, None] == head_of_lane[None, :])
    w["ssd_e"] = emat.astype(BF16)
    w["ssd_et"] = emat.T.astype(BF16)
    w["ssd_dvec"] = jnp.repeat(ssd_d, SSD_HEAD_DIM).reshape(1, SSD_D_INNER)
    w["ssd_norm"] = ssd_norm.reshape(1, SSD_D_INNER)
    c = l1_w_in
    small1 = jnp.concatenate([c[:, 6656:6672], jnp.zeros((D_MODEL, 512 - 16), F32)], axis=1)
    w["l1_in"] = jnp.concatenate(
        [c[:, 0:1280], c[:, 1280:2560], small1, c[:, 2560:3584], c[:, 3584:4608], c[:, 4608:5632],
         c[:, 5632:6656]], axis=1).astype(BF16)
    w["l1_out_a"] = l1_w_out[:LRU_WIDTH].astype(BF16)
    w["l1_out_b"] = l1_w_out[LRU_WIDTH:].astype(BF16)
    w["lru_cw"] = lru_conv_w
    w["lru_cb"] = lru_conv_b.reshape(1, -1)
    w["lru_wa"] = lru_w_a.astype(BF16)
    w["lru_ba"] = lru_b_a.reshape(1, -1)
    w["lru_wx"] = lru_w_x.astype(BF16)
    w["lru_bx"] = lru_b_x.reshape(1, -1)
    w["lru_lam"] = lru_lambda.reshape(1, -1)
    w["gdn_cw"] = gdn_conv_w
    w["gdn_alog"] = _pad_lanes(gdn_a_log, LANES)
    w["gdn_dtb"] = _pad_lanes(gdn_dt_bias, LANES)
    w["gdn_norm"] = gdn_norm.reshape(1, GDN_DV)
    return w


def _trunk_prompt(x, norms, final_norm, w, *, tm, tf, cs):
    nb, l = x.shape[0], x.shape[1]
    t = nb * l
    h = x.reshape(t, D_MODEL)
    h = _ffn(h, norms[0, 0], w["ffn_in"][0][0], w["ffn_out"][0][0], tm=tm, tf=tf)
    p0 = _proj(h, norms[0, 1], w["l0_in"], tm=tm, tn=1664).reshape(nb, l, L0_W)
    o_a, s_gla = _gla_prompt(p0, w["gla_wg2"], w["gla_bg"], w["gla_norm"], lg=cs)
    y, s_ssd, cx, cb, cc = _ssd_prompt(p0, w["ssd_cw"], w["ssd_cb"], w["ssd_dtb"], w["ssd_alog"], w["ssd_e"],
                                       w["ssd_et"], w["ssd_dvec"], w["ssd_norm"], cs=cs)
    h = _outproj(h, o_a.reshape(t, 1024), y.reshape(t, 2048), w["l0_out_a"], w["l0_out_b"], tm=tm)
    h = _ffn(h, norms[0, 2], w["ffn_in"][0][1], w["ffn_out"][0][1], tm=tm, tf=tf)
    h = _ffn(h, norms[1, 0], w["ffn_in"][1][0], w["ffn_out"][1][0], tm=tm, tf=tf)
    p1 = _proj(h, norms[1, 1], w["l1_in"], tm=tm, tn=1024).reshape(nb, l, L1_W)
    o_c, s_lru, s_lru_conv = _lru_prompt(p1, w["lru_cw"], w["lru_cb"], w["lru_wa"], w["lru_ba"], w["lru_wx"],
                                         w["lru_bx"], w["lru_lam"], cs=cs)
    o_d, s_gdn, cq, ck, cv = _gdn_prompt(p1, w["gdn_cw"], w["gdn_alog"], w["gdn_dtb"], w["gdn_norm"], cs=cs)
    h = _outproj(h, o_c.reshape(t, LRU_WIDTH), o_d.reshape(t, 1024), w["l1_out_a"], w["l1_out_b"], tm=tm)
    h = _ffn(h, norms[1, 2], w["ffn_in"][1][1], w["ffn_out"][1][1], final_norm, tm=tm, tf=tf)
    return (h.reshape(nb, l, D_MODEL), s_gla, s_ssd.reshape(nb, SSD_HEADS, SSD_HEAD_DIM, SSD_D_STATE),
            jnp.concatenate([cx, cb, cc], axis=-1), s_lru.reshape(nb, LRU_WIDTH), s_lru_conv, s_gdn,
            jnp.concatenate([cq, ck, cv], axis=-1))


def _conv_step(cst_ref, cst_out, cur, w_ref, bias_ref, lo, hi):
    acc = w_ref[CONV_W - 1:CONV_W, lo:hi] * cur
    for j in range(CONV_W - 1):
        acc = acc + w_ref[j:j + 1, lo:hi] * cst_ref[:, j, lo:hi]
    if bias_ref is not None:
        acc = acc + bias_ref[:, lo:hi]
    for j in range(CONV_W - 2):
        cst_out[:, j, lo:hi] = cst_ref[:, j + 1, lo:hi]
    cst_out[:, CONV_W - 2, lo:hi] = cur
    return acc


def _gla_dec_body(q_ref, k_ref, v_ref, gg_ref, sm_ref, wg2_ref, bg_ref, nw_ref, s_in, o_ref, s_out, o_scr, *, nb):
    pre = _dot(sm_ref[...].astype(BF16), wg2_ref[...]) + bg_ref[...]
    a = jnp.exp(-_softplus(-pre) * (1.0 / GLA_GATE_NORM))
    q = q_ref[...] * (GLA_DK ** -0.5)
    v = v_ref[...]
    a_t, q_t, k_t = a.T, q.T, k_ref[...].T
    for b in range(nb):
        for h in range(GLA_HEADS):
            dk = slice(h * GLA_DK, (h + 1) * GLA_DK)
            dv = slice(h * GLA_DV, (h + 1) * GLA_DV)
            sn = a_t[dk, b:b + 1] * s_in[b, h] + k_t[dk, b:b + 1] * v[b:b + 1, dv]
            s_out[b, h] = sn
            o_scr[b:b + 1, dv] = jnp.sum(q_t[dk, b:b + 1] * sn, axis=0, keepdims=True)
    o = o_scr[...]
    gg = gg_ref[...]
    for h in range(GLA_HEADS):
        dv = slice(h * GLA_DV, (h + 1) * GLA_DV)
        o_ref[:, dv] = _rms(o[:, dv], nw_ref[...]) * _silu(gg[:, dv])


def _gla_step(p0, state, wg2, bgate, normw, *, nb):
    n = p0.shape[0]

    def seg(width, off):
        return pl.BlockSpec((nb, width), lambda i: (i, off // width))

    def full(shape):
        return pl.BlockSpec(shape, lambda i: (0,) * len(shape))

    st = pl.BlockSpec((nb, GLA_HEADS, GLA_DK, GLA_DV), lambda i: (i, 0, 0, 0))
    return pl.pallas_call(
        functools.partial(_gla_dec_body, nb=nb),
        grid=(n // nb,),
        in_specs=[seg(512, L0_GQ), seg(512, L0_GK), seg(1024, L0_GV), seg(1024, L0_GG), seg(LANES, L0_SM),
                  full((LANES, 512)), full((1, 512)), full((1, GLA_DV)), st],
        out_specs=[pl.BlockSpec((nb, 1024), lambda i: (i, 0)), st],
        out_shape=[jax.ShapeDtypeStruct((n, 1024), F32), jax.ShapeDtypeStruct(state.shape, F32)],
        scratch_shapes=[pltpu.VMEM((nb, 1024), F32)],
        compiler_params=_params(("parallel",)),
        name="gla_step",
    )(p0, p0, p0, p0, p0, wg2, bgate, normw, state)


def _ssd_dec_body(z_ref, x_ref, b_ref, c_ref, sm_ref, cst_ref, cw, cbias, dtb_ref, alog_ref, e_ref, dvec_ref,
                  nw_ref, s_in, y_ref, s_out, cst_out, yt_scr, *, nb):
    xs = _silu(_conv_step(cst_ref, cst_out, x_ref[...], cw, cbias, 0, 2048))
    bm = _silu(_conv_step(cst_ref, cst_out, b_ref[...], cw, cbias, 2048, 2560))
    cm = _silu(_conv_step(cst_ref, cst_out, c_ref[...], cw, cbias, 2560, 3072)).astype(BF16)
    dt = _softplus(sm_ref[...] + dtb_ref[...])
    decay = jnp.exp(dt * (-jnp.exp(alog_ref[...])))
    c1_t = _dot_x01(decay, e_ref[...]).T
    c2_t = (_dot_x01(dt, e_ref[...]) * xs).T
    lane = lax.broadcasted_iota(jnp.int32, (SSD_GROUP_W, nb), 1)
    for g in range(SSD_GROUPS):
        gn = slice(g * SSD_D_STATE, (g + 1) * SSD_D_STATE)
        gw = slice(g * SSD_GROUP_W, (g + 1) * SSD_GROUP_W)
        rows = pl.ds(g * SSD_GROUP_W, SSD_GROUP_W)
        ycol = jnp.zeros((SSD_GROUP_W, nb), F32)
        for b in range(nb):
            sn = c1_t[gw, b:b + 1] * s_in[b, rows, :] + c2_t[gw, b:b + 1] * bm[b:b + 1, gn]
            s_out[b, rows, :] = sn
            ycol = jnp.where(lane == b, _dot_nt(sn.astype(BF16), cm[:, gn]), ycol)
        yt_scr[rows, :] = ycol
    y = yt_scr[...].T + dvec_ref[...] * xs
    y = y * _silu(z_ref[...])
    for g in range(SSD_GROUPS):
        gw = slice(g * SSD_GROUP_W, (g + 1) * SSD_GROUP_W)
        y_ref[:, gw] = _rms(y[:, gw], nw_ref[:, gw])


def _ssd_step(p0, state, cstate, cw, cb, dtb, alog, emat, dvec, normw, *, nb):
    n = p0.shape[0]

    def seg(width, off):
        return pl.BlockSpec((nb, width), lambda i: (i, off // width))

    def full(shape):
        return pl.BlockSpec(shape, lambda i: (0,) * len(shape))

    st = pl.BlockSpec((nb, SSD_D_INNER, SSD_D_STATE), lambda i: (i, 0, 0))
    cst = pl.BlockSpec((nb, CONV_W - 1, 3072), lambda i: (i, 0, 0))
    return pl.pallas_call(
        functools.partial(_ssd_dec_body, nb=nb),
        grid=(n // nb,),
        in_specs=[seg(2048, L0_Z), seg(2048, L0_X), seg(512, L0_B), seg(512, L0_C), seg(LANES, L0_SM), cst,
                  full((CONV_W, 3072)), full((1, 3072)), full((1, LANES)), full((1, LANES)), full((LANES, 2048)),
                  full((1, 2048)), full((1, 2048)), st],
        out_specs=[pl.BlockSpec((nb, 2048), lambda i: (i, 0)), st, cst],
        out_shape=[jax.ShapeDtypeStruct((n, 2048), F32), jax.ShapeDtypeStruct(state.shape, F32),
                   jax.ShapeDtypeStruct(cstate.shape, F32)],
        scratch_shapes=[pltpu.VMEM((SSD_D_INNER, nb), F32)],
        compiler_params=_params(("parallel",)),
        name="ssd_step",
    )(p0, p0, p0, p0, p0, cstate, cw, cb, dtb, alog, emat, dvec, normw, state)


def _lru_dec_body(rg_ref, rx_ref, cst_ref, cw, cbias, wa_ref, ba_ref, wx_ref, bx_ref, lam_ref, h_in,
                  o_ref, h_out, cst_out):
    xc = _conv_step(cst_ref, cst_out, rx_ref[...], cw, cbias, 0, LRU_WIDTH)
    a, b = _lru_gates(xc, wa_ref, ba_ref, wx_ref, bx_ref, lam_ref)
    h = a * h_in[...] + b
    h_out[...] = h
    o_ref[...] = _gelu_tanh(rg_ref[...]) * h


def _lru_step(p1, hstate, cstate, cw, cbias, wa, ba, wx, bx, lam):
    n = p1.shape[0]

    def full(shape):
        return pl.BlockSpec(shape, lambda i: (0,) * len(shape))

    return pl.pallas_call(
        _lru_dec_body,
        grid=(1,),
        in_specs=[pl.BlockSpec((n, LRU_WIDTH), lambda i: (0, L1_RG // LRU_WIDTH)),
                  pl.BlockSpec((n, LRU_WIDTH), lambda i: (0, L1_RX // LRU_WIDTH)),
                  full((n, CONV_W - 1, LRU_WIDTH)), full((CONV_W, LRU_WIDTH)), full((1, LRU_WIDTH)),
                  full((LRU_BLOCKS, LRU_BLOCK_W, LRU_BLOCK_W)), full((1, LRU_WIDTH)),
                  full((LRU_BLOCKS, LRU_BLOCK_W, LRU_BLOCK_W)), full((1, LRU_WIDTH)), full((1, LRU_WIDTH)),
                  full((n, LRU_WIDTH))],
        out_specs=[full((n, LRU_WIDTH)), full((n, LRU_WIDTH)), full((n, CONV_W - 1, LRU_WIDTH))],
        out_shape=[jax.ShapeDtypeStruct((n, LRU_WIDTH), F32), jax.ShapeDtypeStruct((n, LRU_WIDTH), F32),
                   jax.ShapeDtypeStruct(cstate.shape, F32)],
        compiler_params=_params(("arbitrary",)),
        name="lru_step",
    )(p1, p1, cstate, cw, cbias, wa, ba, wx, bx, lam, hstate)


def _gdn_dec_body(q_ref, k_ref, v_ref, z_ref, sm_ref, cst_ref, cw, alog_ref, dtb_ref, nw_ref, s_in,
                  o_ref, s_out, cst_out, o_scr, *, nb):
    q = _silu(_conv_step(cst_ref, cst_out, q_ref[...], cw, None, 0, 1024))
    k = _silu(_conv_step(cst_ref, cst_out, k_ref[...], cw, None, 1024, 2048))
    v = _silu(_conv_step(cst_ref, cst_out, v_ref[...], cw, None, 2048, 3072))
    sm = sm_ref[...]
    eg_all = jnp.exp(-jnp.exp(alog_ref[...]) * _softplus(sm + dtb_ref[...]))
    beta_all = _sigmoid(sm)
    qn = jnp.concatenate([_l2norm(q[:, h * GDN_DK:(h + 1) * GDN_DK]) for h in range(GDN_HEADS)], axis=-1)
    qn = qn * (GDN_DK ** -0.5)
    kn = jnp.concatenate([_l2norm(k[:, h * GDN_DK:(h + 1) * GDN_DK]) for h in range(GDN_HEADS)], axis=-1)
    q_t, k_t = qn.T, kn.T
    for b in range(nb):
        for h in range(GDN_HEADS):
            hd = slice(h * GDN_DK, (h + 1) * GDN_DK)
            s = s_in[b, h]
            kcol, qcol = k_t[hd, b:b + 1], q_t[hd, b:b + 1]
            eg = eg_all[b:b + 1, h:h + 1]
            beta = beta_all[b:b + 1, GDN_HEADS + h:GDN_HEADS + h + 1]
            ks = jnp.sum(kcol * s, axis=0, keepdims=True)
            qs = jnp.sum(qcol * s, axis=0, keepdims=True)
            delta = beta * (v[b:b + 1, hd] - eg * ks)
            qk = jnp.sum(qn[b:b + 1, hd] * kn[b:b + 1, hd], axis=-1, keepdims=True)
            o_scr[b:b + 1, hd] = eg * qs + qk * delta
            s_out[b, h] = eg * s + kcol * delta
    o = o_scr[...]
    z = z_ref[...]
    for h in range(GDN_HEADS):
        hd = slice(h * GDN_DK, (h + 1) * GDN_DK)
        o_ref[:, hd] = _rms(o[:, hd], nw_ref[...]) * _silu(z[:, hd])


def _gdn_step(p1, state, cstate, cw, alog, dtb, normw, *, nb):
    n = p1.shape[0]

    def seg(width, off):
        return pl.BlockSpec((nb, width), lambda i: (i, off // width))

    def full(shape):
        return pl.BlockSpec(shape, lambda i: (0,) * len(shape))

    st = pl.BlockSpec((nb, GDN_HEADS, GDN_DK, GDN_DV), lambda i: (i, 0, 0, 0))
    cst = pl.BlockSpec((nb, CONV_W - 1, 3072), lambda i: (i, 0, 0))
    return pl.pallas_call(
        functools.partial(_gdn_dec_body, nb=nb),
        grid=(n // nb,),
        in_specs=[seg(1024, L1_Q), seg(1024, L1_K), seg(1024, L1_V), seg(1024, L1_Z), seg(LANES, L1_SM), cst,
                  full((CONV_W, 3072)), full((1, LANES)), full((1, LANES)), full((1, GDN_DV)), st],
        out_specs=[pl.BlockSpec((nb, 1024), lambda i: (i, 0)), st, cst],
        out_shape=[jax.ShapeDtypeStruct((n, 1024), F32), jax.ShapeDtypeStruct(state.shape, F32),
                   jax.ShapeDtypeStruct(cstate.shape, F32)],
        scratch_shapes=[pltpu.VMEM((nb, 1024), F32)],
        compiler_params=_params(("parallel",)),
        name="gdn_step",
    )(p1, p1, p1, p1, p1, cstate, cw, alog, dtb, normw, state)


def _trunk_sample(x, s_gla, s_ssd, s_ssd_conv, s_lru, s_lru_conv, s_gdn, s_gdn_conv, norms, final_norm, w,
                  *, tf=256, nb=8):
    n = x.shape[0]
    h = x.reshape(n, D_MODEL)
    h = _ffn(h, norms[0, 0], w["ffn_in"][0][0], w["ffn_out"][0][0], tm=n, tf=tf)
    p0 = _proj(h, norms[0, 1], w["l0_in"], tm=n, tn=1664)
    o_a, n_gla = _gla_step(p0, s_gla, w["gla_wg2"], w["gla_bg"], w["gla_norm"], nb=nb)
    y, n_ssd, n_ssd_conv = _ssd_step(p0, s_ssd.reshape(n, SSD_D_INNER, SSD_D_STATE), s_ssd_conv, w["ssd_cw"],
                                     w["ssd_cb"], w["ssd_dtb"], w["ssd_alog"], w["ssd_e"], w["ssd_dvec"],
                                     w["ssd_norm"], nb=nb)
    h = _outproj(h, o_a, y, w["l0_out_a"], w["l0_out_b"], tm=n)
    h = _ffn(h, norms[0, 2], w["ffn_in"][0][1], w["ffn_out"][0][1], tm=n, tf=tf)
    h = _ffn(h, norms[1, 0], w["ffn_in"][1][0], w["ffn_out"][1][0], tm=n, tf=tf)
    p1 = _proj(h, norms[1, 1], w["l1_in"], tm=n, tn=1024)
    o_c, n_lru, n_lru_conv = _lru_step(p1, s_lru, s_lru_conv, w["lru_cw"], w["lru_cb"], w["lru_wa"], w["lru_ba"],
                                       w["lru_wx"], w["lru_bx"], w["lru_lam"])
    o_d, n_gdn, n_gdn_conv = _gdn_step(p1, s_gdn, s_gdn_conv, w["gdn_cw"], w["gdn_alog"], w["gdn_dtb"],
                                       w["gdn_norm"], nb=nb)
    h = _outproj(h, o_c, o_d, w["l1_out_a"], w["l1_out_b"], tm=n)
    h = _ffn(h, norms[1, 2], w["ffn_in"][1][1], w["ffn_out"][1][1], final_norm, tm=n, tf=tf)
    return (h.reshape(n, 1, D_MODEL), n_gla, n_ssd.reshape(s_ssd.shape), n_ssd_conv, n_lru, n_lru_conv, n_gdn,
            n_gdn_conv)


def kernel(x_prompt, x_sample, state_gla, state_ssd, state_ssd_conv, state_lru, state_lru_conv, state_gdn,
           state_gdn_conv, norms, final_norm, ffn_w_in, ffn_w_out, l0_w_in, l0_w_out, gla_w_gate2, gla_b_gate,
           gla_norm, ssd_conv_w, ssd_conv_b, ssd_dt_bias, ssd_a_log, ssd_d, ssd_norm, l1_w_in, l1_w_out,
           lru_conv_w, lru_conv_b, lru_w_a, lru_b_a, lru_w_x, lru_b_x, lru_lambda, gdn_conv_w, gdn_a_log,
           gdn_dt_bias, gdn_norm):
    w = _prep(ffn_w_in, ffn_w_out, l0_w_in, l0_w_out, gla_w_gate2, gla_b_gate, gla_norm, ssd_conv_w, ssd_conv_b,
              ssd_dt_bias, ssd_a_log, ssd_d, ssd_norm, l1_w_in, l1_w_out, lru_conv_w, lru_conv_b, lru_w_a, lru_b_a,
              lru_w_x, lru_b_x, lru_lambda, gdn_conv_w, gdn_a_log, gdn_dt_bias, gdn_norm)
    prompt = _trunk_prompt(x_prompt, norms, final_norm, w, tm=512, tf=256, cs=128)
    sample = _trunk_sample(x_sample, state_gla, state_ssd, state_ssd_conv, state_lru, state_lru_conv, state_gdn,
                           state_gdn_conv, norms, final_norm, w)
    return (prompt[0], sample[0]) + tuple(prompt[1:]) + tuple(sample[1:])
```

```python
import functools
import math

import jax
import jax.numpy as jnp
from jax import lax
from jax.experimental import pallas as pl
from jax.experimental.pallas import tpu as pltpu

F32 = jnp.float32
BF16 = jnp.bfloat16

D_MODEL = 1024
NORM_EPS = 1e-6
CONV_W = 4
D_FF = 2816
FFN_RES = 0.5

GLA_HEADS = 4
GLA_DK = 128
GLA_DV = 256
GLA_RANK = 16
GLA_GATE_NORM = 16.0
GLA_SUB = 16

SSD_D_INNER = 2048
SSD_HEAD_DIM = 64
SSD_HEADS = 32
SSD_GROUPS = 4
SSD_HPG = 8
SSD_D_STATE = 128
SSD_GROUP_W = SSD_HPG * SSD_HEAD_DIM

LRU_WIDTH = 1280
LRU_BLOCKS = 10
LRU_BLOCK_W = 128
LRU_C = 8.0

GDN_HEADS = 8
GDN_DK = 128
GDN_DV = 128

LANES = 128
SUBLANES = 8
NEG_BIG = -1e30
VMEM_LIMIT = 56 * 1024 * 1024

GLA_W = 3072
GLA_GV, GLA_GG, GLA_GQ, GLA_GK = 0, 1024, 2048, 2560
SSD_W = 5120
SSD_Z, SSD_X, SSD_B, SSD_C = 0, 2048, 4096, 4608
GDN_W = 4096
GDN_Q, GDN_K, GDN_V, GDN_Z = 0, 1024, 2048, 3072
LRU_W2 = 2560
LRU_RG, LRU_RX = 0, 1280


def _sigmoid(x):
    return 1.0 / (1.0 + jnp.exp(-x))


def _silu(x):
    return x * _sigmoid(x)


def _softplus(x):
    return jnp.maximum(x, 0.0) + jnp.log(1.0 + jnp.exp(-jnp.abs(x)))


def _rms(x, g):
    return x * lax.rsqrt(jnp.mean(x * x, axis=-1, keepdims=True) + NORM_EPS) * g


def _dot(a, b):
    return jnp.dot(a, b, preferred_element_type=F32)


def _dot_nt(a, b):
    return lax.dot_general(a, b, (((1,), (1,)), ((), ())), preferred_element_type=F32)


def _dot_tn(a, b):
    return lax.dot_general(a, b, (((0,), (0,)), ((), ())), preferred_element_type=F32)


def _split2(x):
    hi = x.astype(BF16)
    lo = (x - hi.astype(F32)).astype(BF16)
    return hi, lo


def _split3(x):
    hi = x.astype(BF16)
    r = x - hi.astype(F32)
    mid = r.astype(BF16)
    lo = (r - mid.astype(F32)).astype(BF16)
    return hi, mid, lo


def _dot01_exact(m01, x):
    hi, mid, lo = _split3(x)
    return _dot(m01, hi) + _dot(m01, mid) + _dot(m01, lo)


def _dot_x01(x, m01):
    hi, lo = _split2(x)
    return _dot(hi, m01) + _dot(lo, m01)


def _lower_mask(n, strict=False):
    row = lax.broadcasted_iota(jnp.int32, (n, n), 0)
    col = lax.broadcasted_iota(jnp.int32, (n, n), 1)
    return row > col if strict else row >= col


def _params(sem):
    return pltpu.CompilerParams(dimension_semantics=sem, vmem_limit_bytes=VMEM_LIMIT)


FFN_CHUNK = 256


def _resident(shape):
    return pl.BlockSpec(shape, lambda i: (0,) * len(shape), pipeline_mode=pl.Buffered(1))


def _ffn_body(*refs, final, pre):
    refs = list(refs)
    x_ref = refs.pop(0)
    if pre:
        a_ref, b_ref, wa_ref, wb_ref = refs[:4]
        refs = refs[4:]
    g_ref, wi_ref, wo_ref = refs[:3]
    refs = refs[3:]
    fg_ref = refs.pop(0) if final else None
    o_ref, act_ref = refs
    x = x_ref[...]
    if pre:
        x = x + _dot(a_ref[...].astype(BF16), wa_ref[...]) + _dot(b_ref[...].astype(BF16), wb_ref[...])
    hn = _rms(x, g_ref[...]).astype(BF16)
    for c in range(D_FF // FFN_CHUNK):
        cols = slice(c * FFN_CHUNK, (c + 1) * FFN_CHUNK)
        gate = _dot(hn, wi_ref[:, cols])
        up = _dot(hn, wi_ref[:, D_FF + c * FFN_CHUNK:D_FF + (c + 1) * FFN_CHUNK])
        act_ref[:, cols] = (_silu(gate) * up).astype(BF16)
    y = x + FFN_RES * _dot(act_ref[...], wo_ref[...])
    if final:
        y = _rms(y, fg_ref[...])
    o_ref[...] = y


def _ffn(x, g, w_in, w_out, final_g=None, pre=None, *, tm):
    t = x.shape[0]
    final = final_g is not None
    in_specs = [pl.BlockSpec((tm, D_MODEL), lambda i: (i, 0))]
    args = [x]
    if pre is not None:
        a, b, wa, wb = pre
        in_specs += [pl.BlockSpec((tm, a.shape[1]), lambda i: (i, 0)), pl.BlockSpec((tm, b.shape[1]), lambda i: (i, 0)),
                     _resident(wa.shape), _resident(wb.shape)]
        args += [a, b, wa, wb]
    in_specs += [_resident((1, D_MODEL)), _resident(w_in.shape), _resident(w_out.shape)]
    args += [g.reshape(1, D_MODEL), w_in, w_out]
    if final:
        in_specs.append(_resident((1, D_MODEL)))
        args.append(final_g.reshape(1, D_MODEL))
    return pl.pallas_call(
        functools.partial(_ffn_body, final=final, pre=pre is not None),
        grid=(t // tm,),
        in_specs=in_specs,
        out_specs=pl.BlockSpec((tm, D_MODEL), lambda i: (i, 0)),
        out_shape=jax.ShapeDtypeStruct((t, D_MODEL), F32),
        scratch_shapes=[pltpu.VMEM((tm, D_FF), BF16)],
        compiler_params=_params(("parallel",)),
        name="ffn",
    )(*args)


PROJ_CHUNK = 1024


def _proj_body(x_ref, g_ref, w_ref, *o_refs, widths):
    hn = _rms(x_ref[...], g_ref[...]).astype(BF16)
    off = 0
    for o_ref, width in zip(o_refs, widths):
        for c in range(0, width, PROJ_CHUNK):
            n = min(PROJ_CHUNK, width - c)
            o_ref[:, c:c + n] = _dot(hn, w_ref[:, off + c:off + c + n]).astype(o_ref.dtype)
        off += width


def _proj(x, g, w, widths, dtypes, *, tm):
    t = x.shape[0]
    return pl.pallas_call(
        functools.partial(_proj_body, widths=tuple(widths)),
        grid=(t // tm,),
        in_specs=[pl.BlockSpec((tm, D_MODEL), lambda i: (i, 0)), _resident((1, D_MODEL)), _resident(w.shape)],
        out_specs=[pl.BlockSpec((tm, n), lambda i: (i, 0)) for n in widths],
        out_shape=[jax.ShapeDtypeStruct((t, n), dt) for n, dt in zip(widths, dtypes)],
        compiler_params=_params(("parallel",)),
        name="proj",
    )(x, g.reshape(1, D_MODEL), w)


def _conv_chunk(buf, src, w_ref, bias_ref, c):
    buf[pl.ds(SUBLANES, c), :] = src.astype(F32)
    acc = w_ref[0:1, :] * buf[pl.ds(5, c), :]
    for j in range(1, CONV_W):
        acc = acc + w_ref[j:j + 1, :] * buf[pl.ds(5 + j, c), :]
    if bias_ref is not None:
        acc = acc + bias_ref[...]
    buf[pl.ds(5, 3), :] = buf[pl.ds(c + 5, 3), :]
    return acc


def _gla_body(q_in, k_in, v_in, gg_ref, sm_ref, wg2_ref, bg_ref, nw_ref, o_ref, s_out, st_ref, b_ref,
              q_ref, k_ref, v_ref, *, lg, nc):
    c = pl.program_id(1)

    @pl.when(c == 0)
    def _():
        st_ref[...] = jnp.zeros(st_ref.shape, F32)

    q_ref[...] = q_in[0].astype(F32) * (GLA_DK ** -0.5)
    k_ref[...] = k_in[0].astype(F32)
    v_ref[...] = v_in[0].astype(F32)
    pre = _dot(sm_ref[0].astype(BF16), wg2_ref[...]) + bg_ref[...]
    log_a = -_softplus(-pre) * (1.0 / GLA_GATE_NORM)
    row = lax.broadcasted_iota(jnp.int32, (lg, lg), 0)
    col = lax.broadcasted_iota(jnp.int32, (lg, lg), 1)
    tri = jnp.where((row >= col) & ((row // GLA_SUB) == (col // GLA_SUB)), 1.0, 0.0).astype(BF16)
    b_ref[...] = _dot01_exact(tri, log_a)

    rowi = lax.broadcasted_iota(jnp.int32, (GLA_SUB, 1), 0)

    def block(i, carry):
        base = pl.multiple_of(i * GLA_SUB, GLA_SUB)
        q = q_ref[pl.ds(base, GLA_SUB), :]
        k = k_ref[pl.ds(base, GLA_SUB), :]
        v = v_ref[pl.ds(base, GLA_SUB), :]
        b = b_ref[pl.ds(base, GLA_SUB), :]
        o = [jnp.zeros((GLA_SUB, GLA_DV), F32) for _ in range(GLA_HEADS)]
        for s in range(GLA_SUB):
            ks = k_ref[pl.ds(base + s, 1), :]
            bs = b_ref[pl.ds(base + s, 1), :]
            vs = v_ref[pl.ds(base + s, 1), :]
            p = q * ks * jnp.exp(b - bs)
            for h in range(GLA_HEADS):
                att = jnp.sum(p[:, h * GLA_DK:(h + 1) * GLA_DK], axis=-1, keepdims=True)
                att = jnp.where(rowi >= s, att, 0.0)
                o[h] = o[h] + att * vs[:, h * GLA_DV:(h + 1) * GLA_DV]
        qe = (q * jnp.exp(b)).astype(BF16)
        blast = b_ref[pl.ds(base + GLA_SUB - 1, 1), :]
        kd = (k * jnp.exp(blast - b)).astype(BF16)
        elast = jnp.exp(blast)
        vb = v.astype(BF16)
        gg = gg_ref[0, pl.ds(base, GLA_SUB), :].astype(F32)
        for h in range(GLA_HEADS):
            dk = slice(h * GLA_DK, (h + 1) * GLA_DK)
            dv = slice(h * GLA_DV, (h + 1) * GLA_DV)
            st = st_ref[h]
            oh = o[h] + _dot_nt(qe[:, dk], st.astype(BF16))
            st_ref[h] = st * elast[:, dk] + _dot_tn(vb[:, dv], kd[:, dk])
            o_ref[0, pl.ds(base, GLA_SUB), dv] = (_rms(oh, nw_ref[...]) * _silu(gg[:, dv])).astype(BF16)
        return carry

    lax.fori_loop(0, lg // GLA_SUB, block, 0)

    @pl.when(c == nc - 1)
    def _():
        for h in range(GLA_HEADS):
            s_out[0, h] = st_ref[h].T


def _gla_prompt(xa, sm, wg2, bgate, normw, *, lg):
    nb, l = xa.shape[0], xa.shape[1]
    nc = l // lg

    def seg(width, off):
        return pl.BlockSpec((1, lg, width), lambda b, c: (b, c, off // width))

    def full(shape):
        return pl.BlockSpec(shape, lambda b, c: (0,) * len(shape))

    return pl.pallas_call(
        functools.partial(_gla_body, lg=lg, nc=nc),
        grid=(nb, nc),
        in_specs=[seg(512, GLA_GQ), seg(512, GLA_GK), seg(1024, GLA_GV), seg(1024, GLA_GG),
                  pl.BlockSpec((1, lg, LANES), lambda b, c: (b, c, 0)),
                  full((LANES, 512)), full((1, 512)), full((1, GLA_DV))],
        out_specs=[pl.BlockSpec((1, lg, 1024), lambda b, c: (b, c, 0)),
                   pl.BlockSpec((1, GLA_HEADS, GLA_DK, GLA_DV), lambda b, c: (b, 0, 0, 0))],
        out_shape=[jax.ShapeDtypeStruct((nb, l, 1024), BF16),
                   jax.ShapeDtypeStruct((nb, GLA_HEADS, GLA_DK, GLA_DV), F32)],
        scratch_shapes=[pltpu.VMEM((GLA_HEADS, GLA_DV, GLA_DK), F32), pltpu.VMEM((lg, 512), F32),
                        pltpu.VMEM((lg, 512), F32), pltpu.VMEM((lg, 512), F32), pltpu.VMEM((lg, 1024), F32)],
        compiler_params=_params(("parallel", "arbitrary")),
        name="gla_prompt",
    )(xa, xa, xa, xa, sm, wg2, bgate, normw)


def _ssd_body(z_ref, x_ref, b_ref, c_ref, sm_ref, cwx, cbx, cwb, cbb, cwc, cbc, dtb_ref, alog_ref, e_ref, et_ref,
              dvec_ref, nw_ref, y_ref, s_out, cx_out, cb_out, cc_out, s_ref, bufx, bufb, bufc, y_scr, *, cs, nc):
    c = pl.program_id(1)

    @pl.when(c == 0)
    def _():
        s_ref[...] = jnp.zeros(s_ref.shape, F32)
        bufx[pl.ds(0, SUBLANES), :] = jnp.zeros((SUBLANES, bufx.shape[1]), F32)
        bufb[pl.ds(0, SUBLANES), :] = jnp.zeros((SUBLANES, bufb.shape[1]), F32)
        bufc[pl.ds(0, SUBLANES), :] = jnp.zeros((SUBLANES, bufc.shape[1]), F32)

    xs = _silu(_conv_chunk(bufx, x_ref[0], cwx, cbx, cs))
    bm = _silu(_conv_chunk(bufb, b_ref[0], cwb, cbb, cs)).astype(BF16)
    cm = _silu(_conv_chunk(bufc, c_ref[0], cwc, cbc, cs)).astype(BF16)

    dt = _softplus(sm_ref[0] + dtb_ref[...])
    da = dt * (-jnp.exp(alog_ref[...]))
    lower = _lower_mask(cs)
    tri = jnp.where(lower, 1.0, 0.0).astype(BF16)
    cum = _dot01_exact(tri, da)
    cum_t = cum.T
    dt_t = dt.T
    last = cum[cs - 1:cs, :]
    ecum_x = _dot_x01(jnp.exp(cum), e_ref[...])
    w_x = _dot_x01(jnp.exp(last - cum) * dt, e_ref[...])
    elast = jnp.broadcast_to(jnp.exp(last), (SUBLANES, LANES))
    eh, el = _split2(elast)
    ecol = _dot_nt(et_ref[...], eh) + _dot_nt(et_ref[...], el)

    xs_bf = xs.astype(BF16)
    xw = (xs * w_x).astype(BF16)
    z = z_ref[0].astype(F32)
    lane = lax.broadcasted_iota(jnp.int32, (cs, LANES), 1)
    for g in range(SSD_GROUPS):
        gn = slice(g * SSD_D_STATE, (g + 1) * SSD_D_STATE)
        gw = slice(g * SSD_GROUP_W, (g + 1) * SSD_GROUP_W)
        cg, bg = cm[:, gn], bm[:, gn]
        cb = _dot_nt(cg, bg)
        sg = s_ref[pl.ds(g * SSD_GROUP_W, SSD_GROUP_W), :]
        y_inter = _dot_nt(cg, sg.astype(BF16)) * ecum_x[:, gw]
        ss = jnp.zeros((cs, 1), F32)
        for p in range(SSD_HPG // 2):
            pair = slice(g * SSD_GROUP_W + p * LANES, g * SSD_GROUP_W + (p + 1) * LANES)
            xp = xs_bf[:, pair]
            ys = []
            for h in (g * SSD_HPG + 2 * p, g * SSD_HPG + 2 * p + 1):
                seg = jnp.exp(jnp.where(lower, cum[:, h:h + 1] - cum_t[h:h + 1, :], NEG_BIG))
                m = cb * seg * dt_t[h:h + 1, :]
                ys.append(_dot(m.astype(BF16), xp))
            yp = jnp.where(lane < SSD_HEAD_DIM, ys[0], ys[1])
            yp = yp + y_inter[:, p * LANES:(p + 1) * LANES] + dvec_ref[:, pair] * xs[:, pair]
            yp = yp * _silu(z[:, pair])
            ss = ss + jnp.sum(yp * yp, axis=-1, keepdims=True)
            y_scr[:, pair] = yp
        inv = lax.rsqrt(ss * (1.0 / SSD_GROUP_W) + NORM_EPS)
        y_ref[0, :, gw] = (y_scr[:, gw] * inv * nw_ref[:, gw]).astype(BF16)
        s_ref[pl.ds(g * SSD_GROUP_W, SSD_GROUP_W), :] = (
            ecol[g * SSD_GROUP_W:(g + 1) * SSD_GROUP_W, 0:1] * sg + _dot_tn(xw[:, gw], bg))

    @pl.when(c == nc - 1)
    def _():
        s_out[0] = s_ref[...]
        cx_out[0] = bufx[pl.ds(5, 3), :]
        cb_out[0] = bufb[pl.ds(5, 3), :]
        cc_out[0] = bufc[pl.ds(5, 3), :]


def _ssd_prompt(xb, sm, cw, cb, dtb, alog, emat, emat_t, dvec, normw, *, cs):
    nb, l = xb.shape[0], xb.shape[1]
    nc = l // cs

    def seg(width, off):
        return pl.BlockSpec((1, cs, width), lambda b, c: (b, c, off // width))

    def full(shape):
        return pl.BlockSpec(shape, lambda b, c: (0,) * len(shape))

    def tail(width):
        return pl.BlockSpec((1, CONV_W - 1, width), lambda b, c: (b, 0, 0))

    cwx, cwb, cwc = cw[:, :2048], cw[:, 2048:2560], cw[:, 2560:]
    cbx, cbb, cbc = cb[:, :2048], cb[:, 2048:2560], cb[:, 2560:]
    return pl.pallas_call(
        functools.partial(_ssd_body, cs=cs, nc=nc),
        grid=(nb, nc),
        in_specs=[seg(2048, SSD_Z), seg(2048, SSD_X), seg(512, SSD_B), seg(512, SSD_C),
                  pl.BlockSpec((1, cs, LANES), lambda b, c: (b, c, 0)),
                  full((CONV_W, 2048)), full((1, 2048)), full((CONV_W, 512)), full((1, 512)),
                  full((CONV_W, 512)), full((1, 512)), full((1, LANES)), full((1, LANES)),
                  full((LANES, 2048)), full((2048, LANES)), full((1, 2048)), full((1, 2048))],
        out_specs=[pl.BlockSpec((1, cs, 2048), lambda b, c: (b, c, 0)),
                   pl.BlockSpec((1, 2048, SSD_D_STATE), lambda b, c: (b, 0, 0)),
                   tail(2048), tail(512), tail(512)],
        out_shape=[jax.ShapeDtypeStruct((nb, l, 2048), BF16),
                   jax.ShapeDtypeStruct((nb, 2048, SSD_D_STATE), F32),
                   jax.ShapeDtypeStruct((nb, CONV_W - 1, 2048), F32),
                   jax.ShapeDtypeStruct((nb, CONV_W - 1, 512), F32),
                   jax.ShapeDtypeStruct((nb, CONV_W - 1, 512), F32)],
        scratch_shapes=[pltpu.VMEM((2048, SSD_D_STATE), F32),
                        pltpu.VMEM((cs + SUBLANES, 2048), F32),
                        pltpu.VMEM((cs + SUBLANES, 512), F32),
                        pltpu.VMEM((cs + SUBLANES, 512), F32),
                        pltpu.VMEM((cs, 2048), F32)],
        compiler_params=_params(("parallel", "arbitrary")),
        name="ssd_prompt",
    )(xb, xb, xb, xb, sm, cwx, cbx, cwb, cbb, cwc, cbc, dtb, alog, emat, emat_t, dvec, normw)


def _lru_gates(xc, wa_ref, ba_ref, wx_ref, bx_ref, lam_ref):
    xb = xc.astype(BF16)
    rl, il = [], []
    for n in range(LRU_BLOCKS):
        blk = xb[:, n * LRU_BLOCK_W:(n + 1) * LRU_BLOCK_W]
        rl.append(_dot(blk, wa_ref[n]))
        il.append(_dot(blk, wx_ref[n]))
    r_logit = jnp.concatenate(rl, axis=-1) + ba_ref[...]
    i_logit = jnp.concatenate(il, axis=-1) + bx_ref[...]
    log_a = -LRU_C * _sigmoid(r_logit) * _softplus(-lam_ref[...])
    a = jnp.exp(log_a)
    b = jnp.sqrt(1.0 - jnp.exp(2.0 * log_a)) * (_sigmoid(i_logit) * xc)
    return a, b


def _gelu_tanh(x):
    return 0.5 * x * (1.0 + jnp.tanh(math.sqrt(2.0 / math.pi) * (x + 0.044715 * (x * x * x))))


def _lru_body(rg_ref, rx_ref, cw, cbias, wa_ref, ba_ref, wx_ref, bx_ref, lam_ref, o_ref, h_out, c_out,
              h_ref, buf, a_scr, b_scr, o_scr, *, cs, nc):
    c = pl.program_id(1)

    @pl.when(c == 0)
    def _():
        h_ref[...] = jnp.zeros(h_ref.shape, F32)
        buf[pl.ds(0, SUBLANES), :] = jnp.zeros((SUBLANES, LRU_WIDTH), F32)

    xc = _conv_chunk(buf, rx_ref[0], cw, cbias, cs)
    a, b = _lru_gates(xc, wa_ref, ba_ref, wx_ref, bx_ref, lam_ref)
    rowm = lax.broadcasted_iota(jnp.int32, (cs, LRU_WIDTH), 0) % SUBLANES
    for sh in (1, 2, 4):
        keep = rowm >= sh
        a_prev = jnp.where(keep, pltpu.roll(a, sh, 0), 1.0)
        b_prev = jnp.where(keep, pltpu.roll(b, sh, 0), 0.0)
        b = a * b_prev + b
        a = a * a_prev
    a_scr[...] = a
    b_scr[...] = b
    h = h_ref[...]
    for t in range(cs // SUBLANES):
        rows = pl.ds(t * SUBLANES, SUBLANES)
        ht = a_scr[rows, :] * h + b_scr[rows, :]
        o_scr[rows, :] = ht
        h = jnp.broadcast_to(ht[SUBLANES - 1:SUBLANES, :], (SUBLANES, LRU_WIDTH))
    h_ref[...] = h
    o_ref[0] = (_gelu_tanh(rg_ref[0].astype(F32)) * o_scr[...]).astype(BF16)

    @pl.when(c == nc - 1)
    def _():
        h_out[0] = h[0:1, :]
        c_out[0] = buf[pl.ds(5, 3), :]


def _lru_prompt(xl, cw, cbias, wa, ba, wx, bx, lam, *, cs):
    nb, l = xl.shape[0], xl.shape[1]
    nc = l // cs

    def full(shape):
        return pl.BlockSpec(shape, lambda b, c: (0,) * len(shape))

    return pl.pallas_call(
        functools.partial(_lru_body, cs=cs, nc=nc),
        grid=(nb, nc),
        in_specs=[pl.BlockSpec((1, cs, LRU_WIDTH), lambda b, c: (b, c, LRU_RG // LRU_WIDTH)),
                  pl.BlockSpec((1, cs, LRU_WIDTH), lambda b, c: (b, c, LRU_RX // LRU_WIDTH)),
                  full((CONV_W, LRU_WIDTH)), full((1, LRU_WIDTH)),
                  full((LRU_BLOCKS, LRU_BLOCK_W, LRU_BLOCK_W)), full((1, LRU_WIDTH)),
                  full((LRU_BLOCKS, LRU_BLOCK_W, LRU_BLOCK_W)), full((1, LRU_WIDTH)), full((1, LRU_WIDTH))],
        out_specs=[pl.BlockSpec((1, cs, LRU_WIDTH), lambda b, c: (b, c, 0)),
                   pl.BlockSpec((1, 1, LRU_WIDTH), lambda b, c: (b, 0, 0)),
                   pl.BlockSpec((1, CONV_W - 1, LRU_WIDTH), lambda b, c: (b, 0, 0))],
        out_shape=[jax.ShapeDtypeStruct((nb, l, LRU_WIDTH), BF16),
                   jax.ShapeDtypeStruct((nb, 1, LRU_WIDTH), F32),
                   jax.ShapeDtypeStruct((nb, CONV_W - 1, LRU_WIDTH), F32)],
        scratch_shapes=[pltpu.VMEM((SUBLANES, LRU_WIDTH), F32),
                        pltpu.VMEM((cs + SUBLANES, LRU_WIDTH), F32),
                        pltpu.VMEM((cs, LRU_WIDTH), F32),
                        pltpu.VMEM((cs, LRU_WIDTH), F32),
                        pltpu.VMEM((cs, LRU_WIDTH), F32)],
        compiler_params=_params(("parallel", "arbitrary")),
        name="lru_prompt",
    )(xl, xl, cw, cbias, wa, ba, wx, bx, lam)


def _l2norm(x):
    return x * lax.rsqrt(jnp.sum(x * x, axis=-1, keepdims=True) + NORM_EPS)


def _cat3(a, b, c, axis):
    return jnp.concatenate([a, b, c], axis=axis)


def _gdn_body(q_ref, k_ref, v_ref, z_ref, sm_ref, cwq, cwk, cwv, alog_ref, dtb_ref, nw_ref,
              o_ref, s_out, cq_out, ck_out, cv_out, s_ref, bufq, bufk, bufv,
              ph_scr, pl_scr, t_scr, qb_scr, kd_scr, aqk_scr, rh_scr, rl_scr, *, cs, nc):
    c = pl.program_id(1)

    @pl.when(c == 0)
    def _():
        s_ref[...] = jnp.zeros(s_ref.shape, F32)
        for buf in (bufq, bufk, bufv):
            buf[pl.ds(0, SUBLANES), :] = jnp.zeros((SUBLANES, buf.shape[1]), F32)

    q = _silu(_conv_chunk(bufq, q_ref[0], cwq, None, cs))
    k = _silu(_conv_chunk(bufk, k_ref[0], cwk, None, cs))
    v = _silu(_conv_chunk(bufv, v_ref[0], cwv, None, cs))
    sm = sm_ref[0]
    g_all = -jnp.exp(alog_ref[...]) * _softplus(sm + dtb_ref[...])
    beta_all = _sigmoid(sm)
    lower = _lower_mask(cs)
    strict = _lower_mask(cs, strict=True)
    diag = lower & jnp.logical_not(strict)
    tri = jnp.where(lower, 1.0, 0.0).astype(BF16)
    gc = _dot01_exact(tri, g_all)
    gc_t = gc.T
    egc_all = jnp.exp(gc)

    for h in range(GDN_HEADS):
        hd = slice(h * GDN_DK, (h + 1) * GDN_DK)
        qh = _l2norm(q[:, hd]) * (GDN_DK ** -0.5)
        kh = _l2norm(k[:, hd])
        beta = beta_all[:, GDN_HEADS + h:GDN_HEADS + h + 1]
        gcol = gc[:, h:h + 1]
        dec = jnp.exp(jnp.where(lower, gcol - gc_t[h:h + 1, :], NEG_BIG))
        qb, kb = qh.astype(BF16), kh.astype(BF16)
        both = _dot_nt(jnp.concatenate([kb, qb], axis=0), kb)
        p = -(beta * jnp.where(strict, dec, 0.0) * both[:cs])
        hi, lo = _split2(p)
        ph_scr[h] = hi
        pl_scr[h] = lo
        t_scr[h] = jnp.where(diag, 1.0, p)
        qb_scr[h] = qb
        aqk_scr[h] = (dec * both[cs:]).astype(BF16)
        glast = gc[cs - 1:cs, h:h + 1]
        kd_scr[h] = (kh * jnp.exp(glast - gcol)).astype(BF16)
        rhs = jnp.concatenate([v[:, hd] * beta, kh * (beta * egc_all[:, h:h + 1])], axis=-1)
        hi, lo = _split2(rhs)
        rh_scr[h] = hi
        rl_scr[h] = lo

    levels = int(math.log2(cs))
    for lvl in range(levels):
        for h in range(GDN_HEADS):
            hi, lo = ph_scr[h], pl_scr[h]
            lhs = _cat3(hi, hi, lo, 1)
            if lvl == 0:
                p2 = _dot(lhs, _cat3(hi, lo, hi, 0))
                t_new = None
            else:
                t = t_scr[h]
                th, tl = _split2(t)
                if lvl < levels - 1:
                    rhs = _cat3(jnp.concatenate([hi, th], axis=1), jnp.concatenate([lo, tl], axis=1),
                                jnp.concatenate([hi, th], axis=1), 0)
                    both = _dot(lhs, rhs)
                    p2, t_new = both[:, :cs], t + both[:, cs:]
                else:
                    p2, t_new = None, t + _dot(lhs, _cat3(th, tl, th, 0))
            if p2 is not None:
                hi2, lo2 = _split2(p2)
                ph_scr[h] = hi2
                pl_scr[h] = lo2
            if t_new is not None:
                t_scr[h] = t_new

    z = z_ref[0].astype(F32)
    for h in range(GDN_HEADS):
        hd = slice(h * GDN_DK, (h + 1) * GDN_DK)
        th, tl = _split2(t_scr[h])
        rh, rl = rh_scr[h], rl_scr[h]
        sol = _dot(_cat3(th, th, tl, 1), _cat3(rh, rl, rh, 0))
        u, w = sol[:, :GDN_DV], sol[:, GDN_DV:]
        sh = s_ref[h]
        sb = sh.astype(BF16)
        qb = qb_scr[h]
        ws_qs = _dot(jnp.concatenate([w.astype(BF16), qb], axis=0), sb)
        db = (u - ws_qs[:cs]).astype(BF16)
        o = egc_all[:, h:h + 1] * ws_qs[cs:] + _dot(aqk_scr[h], db)
        s_ref[h] = jnp.exp(gc[cs - 1:cs, h:h + 1]) * sh + _dot_tn(kd_scr[h], db)
        o_ref[0, :, hd] = (_rms(o, nw_ref[...]) * _silu(z[:, hd])).astype(BF16)

    @pl.when(c == nc - 1)
    def _():
        s_out[0] = s_ref[...]
        cq_out[0] = bufq[pl.ds(5, 3), :]
        ck_out[0] = bufk[pl.ds(5, 3), :]
        cv_out[0] = bufv[pl.ds(5, 3), :]


def _gdn_prompt(xg, sm, cw, alog, dtb, normw, *, cs):
    nb, l = xg.shape[0], xg.shape[1]
    nc = l // cs

    def seg(width, off):
        return pl.BlockSpec((1, cs, width), lambda b, c: (b, c, off // width))

    def full(shape):
        return pl.BlockSpec(shape, lambda b, c: (0,) * len(shape))

    def tail(width):
        return pl.BlockSpec((1, CONV_W - 1, width), lambda b, c: (b, 0, 0))

    hm = (GDN_HEADS, cs, cs)
    return pl.pallas_call(
        functools.partial(_gdn_body, cs=cs, nc=nc),
        grid=(nb, nc),
        in_specs=[seg(1024, GDN_Q), seg(1024, GDN_K), seg(1024, GDN_V), seg(1024, GDN_Z),
                  pl.BlockSpec((1, cs, LANES), lambda b, c: (b, c, 0)),
                  full((CONV_W, 1024)), full((CONV_W, 1024)), full((CONV_W, 1024)),
                  full((1, LANES)), full((1, LANES)), full((1, GDN_DV))],
        out_specs=[pl.BlockSpec((1, cs, 1024), lambda b, c: (b, c, 0)),
                   pl.BlockSpec((1, GDN_HEADS, GDN_DK, GDN_DV), lambda b, c: (b, 0, 0, 0)),
                   tail(1024), tail(1024), tail(1024)],
        out_shape=[jax.ShapeDtypeStruct((nb, l, 1024), BF16),
                   jax.ShapeDtypeStruct((nb, GDN_HEADS, GDN_DK, GDN_DV), F32)]
        + [jax.ShapeDtypeStruct((nb, CONV_W - 1, 1024), F32)] * 3,
        scratch_shapes=[pltpu.VMEM((GDN_HEADS, GDN_DK, GDN_DV), F32)]
        + [pltpu.VMEM((cs + SUBLANES, 1024), F32)] * 3
        + [pltpu.VMEM(hm, BF16), pltpu.VMEM(hm, BF16), pltpu.VMEM(hm, F32),
           pltpu.VMEM((GDN_HEADS, cs, GDN_DK), BF16), pltpu.VMEM((GDN_HEADS, cs, GDN_DK), BF16),
           pltpu.VMEM(hm, BF16), pltpu.VMEM((GDN_HEADS, cs, 2 * GDN_DV), BF16),
           pltpu.VMEM((GDN_HEADS, cs, 2 * GDN_DV), BF16)],
        compiler_params=_params(("parallel", "arbitrary")),
        name="gdn_prompt",
    )(xg, xg, xg, xg, sm, cw[:, :1024], cw[:, 1024:2048], cw[:, 2048:], alog, dtb, normw)


def _pad_lanes(v, width, offset=0):
    return jnp.zeros((1, width), F32).at[0, offset:offset + v.shape[0]].set(v.astype(F32))


def _prep(ffn_w_in, ffn_w_out, l0_w_in, l0_w_out, gla_w_gate2, gla_b_gate, gla_norm, ssd_conv_w, ssd_conv_b,
          ssd_dt_bias, ssd_a_log, ssd_d, ssd_norm, l1_w_in, l1_w_out, lru_conv_w, lru_conv_b, lru_w_a, lru_b_a,
          lru_w_x, lru_b_x, lru_lambda, gdn_conv_w, gdn_a_log, gdn_dt_bias, gdn_norm):
    w = {}
    w["ffn_in"] = [[ffn_w_in[l, i].astype(BF16) for i in range(2)] for l in range(2)]
    w["ffn_out"] = [[ffn_w_out[l, i].astype(BF16) for i in range(2)] for l in range(2)]
    c = l0_w_in
    small0 = jnp.concatenate([c[:, 8208:8240], c[:, 3072:3088], jnp.zeros((D_MODEL, LANES - 48), F32)], axis=1)
    w["l0_in"] = jnp.concatenate(
        [c[:, 1024:2048], c[:, 2048:3072], c[:, 0:512], c[:, 512:1024],
         c[:, 3088:5136], c[:, 5136:7184], c[:, 7184:7696], c[:, 7696:8208],
         small0], axis=1).astype(BF16)
    w["l0_out_a"] = l0_w_out[:1024].astype(BF16)
    w["l0_out_b"] = l0_w_out[1024:].astype(BF16)
    w["gla_wg2"] = jnp.zeros((LANES, 512), F32).at[32:48].set(gla_w_gate2).astype(BF16)
    w["gla_bg"] = gla_b_gate.reshape(1, 512)
    w["gla_norm"] = gla_norm.reshape(1, GLA_DV)
    w["ssd_cw"] = ssd_conv_w
    w["ssd_cb"] = ssd_conv_b.reshape(1, -1)
    w["ssd_dtb"] = _pad_lanes(ssd_dt_bias, LANES)
    w["ssd_alog"] = _pad_lanes(ssd_a_log, LANES)
    head_of_lane = jnp.arange(SSD_D_INNER) // SSD_HEAD_DIM
    emat = (jnp.arange(LANES)[:, None] == head_of_lane[None, :])
    w["ssd_e"] = emat.astype(BF16)
    w["ssd_et"] = emat.T.astype(BF16)
    w["ssd_dvec"] = jnp.repeat(ssd_d, SSD_HEAD_DIM).reshape(1, SSD_D_INNER)
    w["ssd_norm"] = ssd_norm.reshape(1, SSD_D_INNER)
    c = l1_w_in
    small1 = jnp.concatenate([c[:, 6656:6672], jnp.zeros((D_MODEL, LANES - 16), F32)], axis=1)
    w["l1_in"] = jnp.concatenate(
        [c[:, 2560:3584], c[:, 3584:4608], c[:, 4608:5632], c[:, 5632:6656],
         c[:, 0:1280], c[:, 1280:2560],
         small1], axis=1).astype(BF16)
    w["l1_out_a"] = l1_w_out[:LRU_WIDTH].astype(BF16)
    w["l1_out_b"] = l1_w_out[LRU_WIDTH:].astype(BF16)
    w["lru_cw"] = lru_conv_w
    w["lru_cb"] = lru_conv_b.reshape(1, -1)
    w["lru_wa"] = lru_w_a.astype(BF16)
    w["lru_ba"] = lru_b_a.reshape(1, -1)
    w["lru_wx"] = lru_w_x.astype(BF16)
    w["lru_bx"] = lru_b_x.reshape(1, -1)
    w["lru_lam"] = lru_lambda.reshape(1, -1)
    w["gdn_cw"] = gdn_conv_w
    w["gdn_alog"] = _pad_lanes(gdn_a_log, LANES)
    w["gdn_dtb"] = _pad_lanes(gdn_dt_bias, LANES)
    w["gdn_norm"] = gdn_norm.reshape(1, GDN_DV)
    return w


L0_WIDTHS = (GLA_W, SSD_W, LANES)
L1_WIDTHS = (GDN_W, LRU_W2, LANES)


def _trunk_prompt(x, norms, final_norm, w, *, tm, cs):
    nb, l = x.shape[0], x.shape[1]
    t = nb * l
    h = x.reshape(t, D_MODEL)
    h = _ffn(h, norms[0, 0], w["ffn_in"][0][0], w["ffn_out"][0][0], tm=tm)
    xa, xb, sm = _proj(h, norms[0, 1], w["l0_in"], L0_WIDTHS, (BF16, BF16, F32), tm=tm)
    sm = sm.reshape(nb, l, LANES)
    o_a, s_gla = _gla_prompt(xa.reshape(nb, l, GLA_W), sm, w["gla_wg2"], w["gla_bg"], w["gla_norm"], lg=cs)
    y, s_ssd, cx, cb, cc = _ssd_prompt(xb.reshape(nb, l, SSD_W), sm, w["ssd_cw"], w["ssd_cb"], w["ssd_dtb"],
                                       w["ssd_alog"], w["ssd_e"], w["ssd_et"], w["ssd_dvec"], w["ssd_norm"], cs=cs)
    h = _ffn(h, norms[0, 2], w["ffn_in"][0][1], w["ffn_out"][0][1],
             pre=(o_a.reshape(t, 1024), y.reshape(t, 2048), w["l0_out_a"], w["l0_out_b"]), tm=tm)
    h = _ffn(h, norms[1, 0], w["ffn_in"][1][0], w["ffn_out"][1][0], tm=tm)
    xg, xl, sm = _proj(h, norms[1, 1], w["l1_in"], L1_WIDTHS, (BF16, BF16, F32), tm=tm)
    sm = sm.reshape(nb, l, LANES)
    o_c, s_lru, s_lru_conv = _lru_prompt(xl.reshape(nb, l, LRU_W2), w["lru_cw"], w["lru_cb"], w["lru_wa"],
                                         w["lru_ba"], w["lru_wx"], w["lru_bx"], w["lru_lam"], cs=cs)
    o_d, s_gdn, cq, ck, cv = _gdn_prompt(xg.reshape(nb, l, GDN_W), sm, w["gdn_cw"], w["gdn_alog"], w["gdn_dtb"],
                                         w["gdn_norm"], cs=cs)
    h = _ffn(h, norms[1, 2], w["ffn_in"][1][1], w["ffn_out"][1][1], final_norm,
             pre=(o_c.reshape(t, LRU_WIDTH), o_d.reshape(t, 1024), w["l1_out_a"], w["l1_out_b"]), tm=tm)
    return (h.reshape(nb, l, D_MODEL), s_gla, s_ssd.reshape(nb, SSD_HEADS, SSD_HEAD_DIM, SSD_D_STATE),
            jnp.concatenate([cx, cb, cc], axis=-1), s_lru.reshape(nb, LRU_WIDTH), s_lru_conv, s_gdn,
            jnp.concatenate([cq, ck, cv], axis=-1))


def _conv_step(cst_ref, cst_out, cur, w_ref, bias_ref, lo, hi):
    acc = w_ref[CONV_W - 1:CONV_W, lo:hi] * cur
    for j in range(CONV_W - 1):
        acc = acc + w_ref[j:j + 1, lo:hi] * cst_ref[:, j, lo:hi]
    if bias_ref is not None:
        acc = acc + bias_ref[:, lo:hi]
    for j in range(CONV_W - 2):
        cst_out[:, j, lo:hi] = cst_ref[:, j + 1, lo:hi]
    cst_out[:, CONV_W - 2, lo:hi] = cur
    return acc


def _gla_dec_body(q_ref, k_ref, v_ref, gg_ref, sm_ref, wg2_ref, bg_ref, nw_ref, s_in, o_ref, s_out, o_scr, *, nb):
    pre = _dot(sm_ref[...].astype(BF16), wg2_ref[...]) + bg_ref[...]
    a = jnp.exp(-_softplus(-pre) * (1.0 / GLA_GATE_NORM))
    q = q_ref[...] * (GLA_DK ** -0.5)
    v = v_ref[...]
    a_t, q_t, k_t = a.T, q.T, k_ref[...].T
    for b in range(nb):
        for h in range(GLA_HEADS):
            dk = slice(h * GLA_DK, (h + 1) * GLA_DK)
            dv = slice(h * GLA_DV, (h + 1) * GLA_DV)
            sn = a_t[dk, b:b + 1] * s_in[b, h] + k_t[dk, b:b + 1] * v[b:b + 1, dv]
            s_out[b, h] = sn
            o_scr[b:b + 1, dv] = jnp.sum(q_t[dk, b:b + 1] * sn, axis=0, keepdims=True)
    o = o_scr[...]
    gg = gg_ref[...]
    for h in range(GLA_HEADS):
        dv = slice(h * GLA_DV, (h + 1) * GLA_DV)
        o_ref[:, dv] = _rms(o[:, dv], nw_ref[...]) * _silu(gg[:, dv])


def _gla_step(xa, sm, state, wg2, bgate, normw, *, nb):
    n = xa.shape[0]

    def seg(width, off):
        return pl.BlockSpec((nb, width), lambda i: (i, off // width))

    def full(shape):
        return pl.BlockSpec(shape, lambda i: (0,) * len(shape))

    st = pl.BlockSpec((nb, GLA_HEADS, GLA_DK, GLA_DV), lambda i: (i, 0, 0, 0))
    return pl.pallas_call(
        functools.partial(_gla_dec_body, nb=nb),
        grid=(n // nb,),
        in_specs=[seg(512, GLA_GQ), seg(512, GLA_GK), seg(1024, GLA_GV), seg(1024, GLA_GG), seg(LANES, 0),
                  full((LANES, 512)), full((1, 512)), full((1, GLA_DV)), st],
        out_specs=[pl.BlockSpec((nb, 1024), lambda i: (i, 0)), st],
        out_shape=[jax.ShapeDtypeStruct((n, 1024), F32), jax.ShapeDtypeStruct(state.shape, F32)],
        scratch_shapes=[pltpu.VMEM((nb, 1024), F32)],
        compiler_params=_params(("parallel",)),
        name="gla_step",
    )(xa, xa, xa, xa, sm, wg2, bgate, normw, state)


def _ssd_dec_body(z_ref, x_ref, b_ref, c_ref, sm_ref, cst_ref, cw, cbias, dtb_ref, alog_ref, e_ref, dvec_ref,
                  nw_ref, s_in, y_ref, s_out, cst_out, yt_scr, *, nb):
    xs = _silu(_conv_step(cst_ref, cst_out, x_ref[...], cw, cbias, 0, 2048))
    bm = _silu(_conv_step(cst_ref, cst_out, b_ref[...], cw, cbias, 2048, 2560))
    cm = _silu(_conv_step(cst_ref, cst_out, c_ref[...], cw, cbias, 2560, 3072)).astype(BF16)
    dt = _softplus(sm_ref[...] + dtb_ref[...])
    decay = jnp.exp(dt * (-jnp.exp(alog_ref[...])))
    c1_t = _dot_x01(decay, e_ref[...]).T
    c2_t = (_dot_x01(dt, e_ref[...]) * xs).T
    lane = lax.broadcasted_iota(jnp.int32, (SSD_GROUP_W, nb), 1)
    for g in range(SSD_GROUPS):
        gn = slice(g * SSD_D_STATE, (g + 1) * SSD_D_STATE)
        gw = slice(g * SSD_GROUP_W, (g + 1) * SSD_GROUP_W)
        rows = pl.ds(g * SSD_GROUP_W, SSD_GROUP_W)
        ycol = jnp.zeros((SSD_GROUP_W, nb), F32)
        for b in range(nb):
            sn = c1_t[gw, b:b + 1] * s_in[b, rows, :] + c2_t[gw, b:b + 1] * bm[b:b + 1, gn]
            s_out[b, rows, :] = sn
            ycol = jnp.where(lane == b, _dot_nt(sn.astype(BF16), cm[:, gn]), ycol)
        yt_scr[rows, :] = ycol
    y = yt_scr[...].T + dvec_ref[...] * xs
    y = y * _silu(z_ref[...])
    for g in range(SSD_GROUPS):
        gw = slice(g * SSD_GROUP_W, (g + 1) * SSD_GROUP_W)
        y_ref[:, gw] = _rms(y[:, gw], nw_ref[:, gw])


def _ssd_step(xb, sm, state, cstate, cw, cb, dtb, alog, emat, dvec, normw, *, nb):
    n = xb.shape[0]

    def seg(width, off):
        return pl.BlockSpec((nb, width), lambda i: (i, off // width))

    def full(shape):
        return pl.BlockSpec(shape, lambda i: (0,) * len(shape))

    st = pl.BlockSpec((nb, SSD_D_INNER, SSD_D_STATE), lambda i: (i, 0, 0))
    cst = pl.BlockSpec((nb, CONV_W - 1, 3072), lambda i: (i, 0, 0))
    return pl.pallas_call(
        functools.partial(_ssd_dec_body, nb=nb),
        grid=(n // nb,),
        in_specs=[seg(2048, SSD_Z), seg(2048, SSD_X), seg(512, SSD_B), seg(512, SSD_C), seg(LANES, 0), cst,
                  full((CONV_W, 3072)), full((1, 3072)), full((1, LANES)), full((1, LANES)), full((LANES, 2048)),
                  full((1, 2048)), full((1, 2048)), st],
        out_specs=[pl.BlockSpec((nb, 2048), lambda i: (i, 0)), st, cst],
        out_shape=[jax.ShapeDtypeStruct((n, 2048), F32), jax.ShapeDtypeStruct(state.shape, F32),
                   jax.ShapeDtypeStruct(cstate.shape, F32)],
        scratch_shapes=[pltpu.VMEM((SSD_D_INNER, nb), F32)],
        compiler_params=_params(("parallel",)),
        name="ssd_step",
    )(xb, xb, xb, xb, sm, cstate, cw, cb, dtb, alog, emat, dvec, normw, state)


def _lru_dec_body(rg_ref, rx_ref, cst_ref, cw, cbias, wa_ref, ba_ref, wx_ref, bx_ref, lam_ref, h_in,
                  o_ref, h_out, cst_out):
    xc = _conv_step(cst_ref, cst_out, rx_ref[...], cw, cbias, 0, LRU_WIDTH)
    a, b = _lru_gates(xc, wa_ref, ba_ref, wx_ref, bx_ref, lam_ref)
    h = a * h_in[...] + b
    h_out[...] = h
    o_ref[...] = _gelu_tanh(rg_ref[...]) * h


def _lru_step(xl, hstate, cstate, cw, cbias, wa, ba, wx, bx, lam):
    n = xl.shape[0]

    def full(shape):
        return pl.BlockSpec(shape, lambda i: (0,) * len(shape))

    return pl.pallas_call(
        _lru_dec_body,
        grid=(1,),
        in_specs=[pl.BlockSpec((n, LRU_WIDTH), lambda i: (0, LRU_RG // LRU_WIDTH)),
                  pl.BlockSpec((n, LRU_WIDTH), lambda i: (0, LRU_RX // LRU_WIDTH)),
                  full((n, CONV_W - 1, LRU_WIDTH)), full((CONV_W, LRU_WIDTH)), full((1, LRU_WIDTH)),
                  full((LRU_BLOCKS, LRU_BLOCK_W, LRU_BLOCK_W)), full((1, LRU_WIDTH)),
                  full((LRU_BLOCKS, LRU_BLOCK_W, LRU_BLOCK_W)), full((1, LRU_WIDTH)), full((1, LRU_WIDTH)),
                  full((n, LRU_WIDTH))],
        out_specs=[full((n, LRU_WIDTH)), full((n, LRU_WIDTH)), full((n, CONV_W - 1, LRU_WIDTH))],
        out_shape=[jax.ShapeDtypeStruct((n, LRU_WIDTH), F32), jax.ShapeDtypeStruct((n, LRU_WIDTH), F32),
                   jax.ShapeDtypeStruct(cstate.shape, F32)],
        compiler_params=_params(("arbitrary",)),
        name="lru_step",
    )(xl, xl, cstate, cw, cbias, wa, ba, wx, bx, lam, hstate)


def _gdn_dec_body(q_ref, k_ref, v_ref, z_ref, sm_ref, cst_ref, cw, alog_ref, dtb_ref, nw_ref, s_in,
                  o_ref, s_out, cst_out, o_scr, *, nb):
    q = _silu(_conv_step(cst_ref, cst_out, q_ref[...], cw, None, 0, 1024))
    k = _silu(_conv_step(cst_ref, cst_out, k_ref[...], cw, None, 1024, 2048))
    v = _silu(_conv_step(cst_ref, cst_out, v_ref[...], cw, None, 2048, 3072))
    sm = sm_ref[...]
    eg_all = jnp.exp(-jnp.exp(alog_ref[...]) * _softplus(sm + dtb_ref[...]))
    beta_all = _sigmoid(sm)
    qn = jnp.concatenate([_l2norm(q[:, h * GDN_DK:(h + 1) * GDN_DK]) for h in range(GDN_HEADS)], axis=-1)
    qn = qn * (GDN_DK ** -0.5)
    kn = jnp.concatenate([_l2norm(k[:, h * GDN_DK:(h + 1) * GDN_DK]) for h in range(GDN_HEADS)], axis=-1)
    q_t, k_t = qn.T, kn.T
    for b in range(nb):
        for h in range(GDN_HEADS):
            hd = slice(h * GDN_DK, (h + 1) * GDN_DK)
            s = s_in[b, h]
            kcol, qcol = k_t[hd, b:b + 1], q_t[hd, b:b + 1]
            eg = eg_all[b:b + 1, h:h + 1]
            beta = beta_all[b:b + 1, GDN_HEADS + h:GDN_HEADS + h + 1]
            ks = jnp.sum(kcol * s, axis=0, keepdims=True)
            qs = jnp.sum(qcol * s, axis=0, keepdims=True)
            delta = beta * (v[b:b + 1, hd] - eg * ks)
            qk = jnp.sum(qn[b:b + 1, hd] * kn[b:b + 1, hd], axis=-1, keepdims=True)
            o_scr[b:b + 1, hd] = eg * qs + qk * delta
            s_out[b, h] = eg * s + kcol * delta
    o = o_scr[...]
    z = z_ref[...]
    for h in range(GDN_HEADS):
        hd = slice(h * GDN_DK, (h + 1) * GDN_DK)
        o_ref[:, hd] = _rms(o[:, hd], nw_ref[...]) * _silu(z[:, hd])


def _gdn_step(xg, sm, state, cstate, cw, alog, dtb, normw, *, nb):
    n = xg.shape[0]

    def seg(width, off):
        return pl.BlockSpec((nb, width), lambda i: (i, off // width))

    def full(shape):
        return pl.BlockSpec(shape, lambda i: (0,) * len(shape))

    st = pl.BlockSpec((nb, GDN_HEADS, GDN_DK, GDN_DV), lambda i: (i, 0, 0, 0))
    cst = pl.BlockSpec((nb, CONV_W - 1, 3072), lambda i: (i, 0, 0))
    return pl.pallas_call(
        functools.partial(_gdn_dec_body, nb=nb),
        grid=(n // nb,),
        in_specs=[seg(1024, GDN_Q), seg(1024, GDN_K), seg(1024, GDN_V), seg(1024, GDN_Z), seg(LANES, 0), cst,
                  full((CONV_W, 3072)), full((1, LANES)), full((1, LANES)), full((1, GDN_DV)), st],
        out_specs=[pl.BlockSpec((nb, 1024), lambda i: (i, 0)), st, cst],
        out_shape=[jax.ShapeDtypeStruct((n, 1024), F32), jax.ShapeDtypeStruct(state.shape, F32),
                   jax.ShapeDtypeStruct(cstate.shape, F32)],
        scratch_shapes=[pltpu.VMEM((nb, 1024), F32)],
        compiler_params=_params(("parallel",)),
        name="gdn_step",
    )(xg, xg, xg, xg, sm, cstate, cw, alog, dtb, normw, state)


def _trunk_sample(x, s_gla, s_ssd, s_ssd_conv, s_lru, s_lru_conv, s_gdn, s_gdn_conv, norms, final_norm, w, *, nb=8):
    n = x.shape[0]
    f3 = (F32, F32, F32)
    h = x.reshape(n, D_MODEL)
    h = _ffn(h, norms[0, 0], w["ffn_in"][0][0], w["ffn_out"][0][0], tm=n)
    xa, xb, sm = _proj(h, norms[0, 1], w["l0_in"], L0_WIDTHS, f3, tm=n)
    o_a, n_gla = _gla_step(xa, sm, s_gla, w["gla_wg2"], w["gla_bg"], w["gla_norm"], nb=nb)
    y, n_ssd, n_ssd_conv = _ssd_step(xb, sm, s_ssd.reshape(n, SSD_D_INNER, SSD_D_STATE), s_ssd_conv, w["ssd_cw"],
                                     w["ssd_cb"], w["ssd_dtb"], w["ssd_alog"], w["ssd_e"], w["ssd_dvec"],
                                     w["ssd_norm"], nb=nb)
    h = _ffn(h, norms[0, 2], w["ffn_in"][0][1], w["ffn_out"][0][1], pre=(o_a, y, w["l0_out_a"], w["l0_out_b"]), tm=n)
    h = _ffn(h, norms[1, 0], w["ffn_in"][1][0], w["ffn_out"][1][0], tm=n)
    xg, xl, sm = _proj(h, norms[1, 1], w["l1_in"], L1_WIDTHS, f3, tm=n)
    o_c, n_lru, n_lru_conv = _lru_step(xl, s_lru, s_lru_conv, w["lru_cw"], w["lru_cb"], w["lru_wa"], w["lru_ba"],
                                       w["lru_wx"], w["lru_bx"], w["lru_lam"])
    o_d, n_gdn, n_gdn_conv = _gdn_step(xg, sm, s_gdn, s_gdn_conv, w["gdn_cw"], w["gdn_alog"], w["gdn_dtb"],
                                       w["gdn_norm"], nb=nb)
    h = _ffn(h, norms[1, 2], w["ffn_in"][1][1], w["ffn_out"][1][1], final_norm,
             pre=(o_c, o_d, w["l1_out_a"], w["l1_out_b"]), tm=n)
    return (h.reshape(n, 1, D_MODEL), n_gla, n_ssd.reshape(s_ssd.shape), n_ssd_conv, n_lru, n_lru_conv, n_gdn,
            n_gdn_conv)


def kernel(x_prompt, x_sample, state_gla, state_ssd, state_ssd_conv, state_lru, state_lru_conv, state_gdn,
           state_gdn_conv, norms, final_norm, ffn_w_in, ffn_w_out, l0_w_in, l0_w_out, gla_w_gate2, gla_b_gate,
           gla_norm, ssd_conv_w, ssd_conv_b, ssd_dt_bias, ssd_a_log, ssd_d, ssd_norm, l1_w_in, l1_w_out,
           lru_conv_w, lru_conv_b, lru_w_a, lru_b_a, lru_w_x, lru_b_x, lru_lambda, gdn_conv_w, gdn_a_log,
           gdn_dt_bias, gdn_norm):
    w = _prep(ffn_w_in, ffn_w_out, l0_w_in, l0_w_out, gla_w_gate2, gla_b_gate, gla_norm, ssd_conv_w, ssd_conv_b,
              ssd_dt_bias, ssd_a_log, ssd_d, ssd_norm, l1_w_in, l1_w_out, lru_conv_w, lru_conv_b, lru_w_a, lru_b_a,
              lru_w_x, lru_b_x, lru_lambda, gdn_conv_w, gdn_a_log, gdn_dt_bias, gdn_norm)
    prompt = _trunk_prompt(x_prompt, norms, final_norm, w, tm=512, cs=128)
    sample = _trunk_sample(x_sample, state_gla, state_ssd, state_ssd_conv, state_lru, state_lru_conv, state_gdn,
                           state_gdn_conv, norms, final_norm, w)
    return (prompt[0], sample[0]) + tuple(prompt[1:]) + tuple(sample[1:])
```

```python
import functools
import math

import jax
import jax.numpy as jnp
from jax import lax
from jax.experimental import pallas as pl
from jax.experimental.pallas import tpu as pltpu

F32 = jnp.float32
BF16 = jnp.bfloat16

D_MODEL = 1024
NORM_EPS = 1e-6
CONV_W = 4
D_FF = 2816
FFN_RES = 0.5

GLA_HEADS = 4
GLA_DK = 128
GLA_DV = 256
GLA_RANK = 16
GLA_GATE_NORM = 16.0
GLA_SUB = 16
GLA_FAST_MAX = 60.0

SSD_D_INNER = 2048
SSD_HEAD_DIM = 64
SSD_HEADS = 32
SSD_GROUPS = 4
SSD_HPG = 8
SSD_D_STATE = 128
SSD_GROUP_W = SSD_HPG * SSD_HEAD_DIM

LRU_WIDTH = 1280
LRU_BLOCKS = 10
LRU_BLOCK_W = 128
LRU_C = 8.0

GDN_HEADS = 8
GDN_DK = 128
GDN_DV = 128

LANES = 128
SUBLANES = 8
NEG_BIG = -1e30
VMEM_LIMIT = 56 * 1024 * 1024

GLA_W = 3072
GLA_GV, GLA_GG, GLA_GQ, GLA_GK = 0, 1024, 2048, 2560
SSD_W = 5120
SSD_Z, SSD_X, SSD_B, SSD_C = 0, 2048, 4096, 4608
GDN_W = 4096
GDN_Q, GDN_K, GDN_V, GDN_Z = 0, 1024, 2048, 3072
LRU_W2 = 2560
LRU_RG, LRU_RX = 0, 1280


def _sigmoid(x):
    return 0.5 * jnp.tanh(0.5 * x) + 0.5


def _silu(x):
    hx = 0.5 * x
    return hx * jnp.tanh(hx) + hx


def _softplus(x):
    return jnp.maximum(x, 0.0) + jnp.log(1.0 + jnp.exp(-jnp.abs(x)))


def _rms(x, g):
    return x * lax.rsqrt(jnp.mean(x * x, axis=-1, keepdims=True) + NORM_EPS) * g


def _dot(a, b):
    return jnp.dot(a, b, preferred_element_type=F32)


def _dot_nt(a, b):
    return lax.dot_general(a, b, (((1,), (1,)), ((), ())), preferred_element_type=F32)


def _dot_tn(a, b):
    return lax.dot_general(a, b, (((0,), (0,)), ((), ())), preferred_element_type=F32)


def _split2(x):
    hi = x.astype(BF16)
    lo = (x - hi.astype(F32)).astype(BF16)
    return hi, lo


def _split3(x):
    hi = x.astype(BF16)
    r = x - hi.astype(F32)
    mid = r.astype(BF16)
    lo = (r - mid.astype(F32)).astype(BF16)
    return hi, mid, lo


def _dot01_exact(m01, x):
    hi, mid, lo = _split3(x)
    return _dot(m01, hi) + _dot(m01, mid) + _dot(m01, lo)


def _dot_x01(x, m01):
    hi, lo = _split2(x)
    return _dot(hi, m01) + _dot(lo, m01)


def _lower_mask(n, strict=False):
    row = lax.broadcasted_iota(jnp.int32, (n, n), 0)
    col = lax.broadcasted_iota(jnp.int32, (n, n), 1)
    return row > col if strict else row >= col


def _params(sem):
    return pltpu.CompilerParams(dimension_semantics=sem, vmem_limit_bytes=VMEM_LIMIT)


FFN_CHUNK = 256


def _resident(shape):
    return pl.BlockSpec(shape, lambda i: (0,) * len(shape), pipeline_mode=pl.Buffered(1))


def _ffn_body(*refs, final, pre):
    refs = list(refs)
    x_ref = refs.pop(0)
    if pre:
        a_ref, b_ref, wa_ref, wb_ref = refs[:4]
        refs = refs[4:]
    g_ref, wi_ref, wo_ref = refs[:3]
    refs = refs[3:]
    fg_ref = refs.pop(0) if final else None
    o_ref, act_ref = refs
    x = x_ref[...]
    if pre:
        x = x + _dot(a_ref[...].astype(BF16), wa_ref[...]) + _dot(b_ref[...].astype(BF16), wb_ref[...])
    hn = _rms(x, g_ref[...]).astype(BF16)
    for c in range(D_FF // FFN_CHUNK):
        cols = slice(c * FFN_CHUNK, (c + 1) * FFN_CHUNK)
        gate = _dot(hn, wi_ref[:, cols])
        up = _dot(hn, wi_ref[:, D_FF + c * FFN_CHUNK:D_FF + (c + 1) * FFN_CHUNK])
        act_ref[:, cols] = (_silu(gate) * up).astype(BF16)
    y = x + FFN_RES * _dot(act_ref[...], wo_ref[...])
    if final:
        y = _rms(y, fg_ref[...])
    o_ref[...] = y


def _ffn(x, g, w_in, w_out, final_g=None, pre=None, *, tm):
    t = x.shape[0]
    final = final_g is not None
    in_specs = [pl.BlockSpec((tm, D_MODEL), lambda i: (i, 0))]
    args = [x]
    if pre is not None:
        a, b, wa, wb = pre
        in_specs += [pl.BlockSpec((tm, a.shape[1]), lambda i: (i, 0)), pl.BlockSpec((tm, b.shape[1]), lambda i: (i, 0)),
                     _resident(wa.shape), _resident(wb.shape)]
        args += [a, b, wa, wb]
    in_specs += [_resident((1, D_MODEL)), _resident(w_in.shape), _resident(w_out.shape)]
    args += [g.reshape(1, D_MODEL), w_in, w_out]
    if final:
        in_specs.append(_resident((1, D_MODEL)))
        args.append(final_g.reshape(1, D_MODEL))
    return pl.pallas_call(
        functools.partial(_ffn_body, final=final, pre=pre is not None),
        grid=(t // tm,),
        in_specs=in_specs,
        out_specs=pl.BlockSpec((tm, D_MODEL), lambda i: (i, 0)),
        out_shape=jax.ShapeDtypeStruct((t, D_MODEL), F32),
        scratch_shapes=[pltpu.VMEM((tm, D_FF), BF16)],
        compiler_params=_params(("parallel",)),
        name="ffn",
    )(*args)


PROJ_CHUNK = 1024


def _proj_body(x_ref, g_ref, w_ref, *o_refs, widths):
    hn = _rms(x_ref[...], g_ref[...]).astype(BF16)
    off = 0
    for o_ref, width in zip(o_refs, widths):
        for c in range(0, width, PROJ_CHUNK):
            n = min(PROJ_CHUNK, width - c)
            o_ref[:, c:c + n] = _dot(hn, w_ref[:, off + c:off + c + n]).astype(o_ref.dtype)
        off += width


def _proj(x, g, w, widths, dtypes, *, tm):
    t = x.shape[0]
    return pl.pallas_call(
        functools.partial(_proj_body, widths=tuple(widths)),
        grid=(t // tm,),
        in_specs=[pl.BlockSpec((tm, D_MODEL), lambda i: (i, 0)), _resident((1, D_MODEL)), _resident(w.shape)],
        out_specs=[pl.BlockSpec((tm, n), lambda i: (i, 0)) for n in widths],
        out_shape=[jax.ShapeDtypeStruct((t, n), dt) for n, dt in zip(widths, dtypes)],
        compiler_params=_params(("parallel",)),
        name="proj",
    )(x, g.reshape(1, D_MODEL), w)


def _shift_rows(u, carry):
    r = pltpu.roll(u, 1, 0)
    row = lax.broadcasted_iota(jnp.int32, (SUBLANES, u.shape[1]), 0)
    first = jnp.where(row == 0, carry, r[:SUBLANES])
    return jnp.concatenate([first, r[SUBLANES:]], axis=0)


def _conv_chunk(tail, src, w_ref, bias_ref, c):
    x = src.astype(F32)
    t3, t2, t1 = tail[5:6, :], tail[6:7, :], tail[7:8, :]
    w0, w1, w2, w3 = (w_ref[j:j + 1, :] for j in range(CONV_W))
    u = _shift_rows(w0 * x, w0 * t1)
    u = _shift_rows(w1 * x + u, w1 * t1 + w0 * t2)
    u = _shift_rows(w2 * x + u, w2 * t1 + w1 * t2 + w0 * t3)
    acc = w3 * x + u
    if bias_ref is not None:
        acc = acc + bias_ref[...]
    tail[pl.ds(5, 3), :] = x[c - 3:c, :]
    return acc


def _gla_body(q_in, k_in, v_in, gg_ref, sm_ref, wg2_ref, bg_ref, nw_ref, o_ref, s_out, st_ref, b_ref,
              q_ref, k_ref, v_ref, *, lg, nc):
    c = pl.program_id(1)

    @pl.when(c == 0)
    def _():
        st_ref[...] = jnp.zeros(st_ref.shape, F32)

    pre = _dot(sm_ref[0].astype(BF16), wg2_ref[...]) + bg_ref[...]
    log_a = -_softplus(-pre) * (1.0 / GLA_GATE_NORM)
    lower = _lower_mask(lg)
    b_all = _dot01_exact(jnp.where(lower, 1.0, 0.0).astype(BF16), log_a)
    b_ref[...] = b_all
    worst = jnp.max(-b_all[lg - 1:lg, :])

    @pl.when(worst <= GLA_FAST_MAX)
    def _():
        q = q_in[0].astype(F32) * (GLA_DK ** -0.5)
        k = k_in[0].astype(F32)
        vb = v_in[0]
        blast = b_all[lg - 1:lg, :]
        qe = (q * jnp.exp(b_all)).astype(BF16)
        ke = (k * jnp.exp(-b_all)).astype(BF16)
        kd = (k * jnp.exp(blast - b_all)).astype(BF16)
        elast = jnp.exp(blast)
        gg = gg_ref[0].astype(F32)
        for h in range(GLA_HEADS):
            dk = slice(h * GLA_DK, (h + 1) * GLA_DK)
            dv = slice(h * GLA_DV, (h + 1) * GLA_DV)
            att = jnp.where(lower, _dot_nt(qe[:, dk], ke[:, dk]), 0.0).astype(BF16)
            st = st_ref[h]
            oh = _dot(att, vb[:, dv]) + _dot_nt(qe[:, dk], st.astype(BF16))
            st_ref[h] = st * elast[:, dk] + _dot_tn(vb[:, dv], kd[:, dk])
            o_ref[0, :, dv] = (_rms(oh, nw_ref[...]) * _silu(gg[:, dv])).astype(BF16)

    @pl.when(worst > GLA_FAST_MAX)
    def _():
        q_ref[...] = q_in[0].astype(F32) * (GLA_DK ** -0.5)
        k_ref[...] = k_in[0].astype(F32)
        v_ref[...] = v_in[0].astype(F32)
        rowi = lax.broadcasted_iota(jnp.int32, (GLA_SUB, 1), 0)

        def block(i, carry):
            base = pl.multiple_of(i * GLA_SUB, GLA_SUB)
            q = q_ref[pl.ds(base, GLA_SUB), :]
            k = k_ref[pl.ds(base, GLA_SUB), :]
            v = v_ref[pl.ds(base, GLA_SUB), :]
            b = b_ref[pl.ds(base, GLA_SUB), :]
            o = [jnp.zeros((GLA_SUB, GLA_DV), F32) for _ in range(GLA_HEADS)]
            for s in range(GLA_SUB):
                ks = k_ref[pl.ds(base + s, 1), :]
                bs = b_ref[pl.ds(base + s, 1), :]
                vs = v_ref[pl.ds(base + s, 1), :]
                p = q * ks * jnp.exp(b - bs)
                for h in range(GLA_HEADS):
                    att = jnp.sum(p[:, h * GLA_DK:(h + 1) * GLA_DK], axis=-1, keepdims=True)
                    att = jnp.where(rowi >= s, att, 0.0)
                    o[h] = o[h] + att * vs[:, h * GLA_DV:(h + 1) * GLA_DV]
            bprev = jnp.where(i > 0, b_ref[pl.ds(jnp.maximum(base - 1, 0), 1), :], 0.0)
            blast = b_ref[pl.ds(base + GLA_SUB - 1, 1), :]
            qe = (q * jnp.exp(b - bprev)).astype(BF16)
            kd = (k * jnp.exp(blast - b)).astype(BF16)
            elast = jnp.exp(blast - bprev)
            vb = v.astype(BF16)
            gg = gg_ref[0, pl.ds(base, GLA_SUB), :].astype(F32)
            for h in range(GLA_HEADS):
                dk = slice(h * GLA_DK, (h + 1) * GLA_DK)
                dv = slice(h * GLA_DV, (h + 1) * GLA_DV)
                st = st_ref[h]
                oh = o[h] + _dot_nt(qe[:, dk], st.astype(BF16))
                st_ref[h] = st * elast[:, dk] + _dot_tn(vb[:, dv], kd[:, dk])
                o_ref[0, pl.ds(base, GLA_SUB), dv] = (_rms(oh, nw_ref[...]) * _silu(gg[:, dv])).astype(BF16)
            return carry

        lax.fori_loop(0, lg // GLA_SUB, block, 0)

    @pl.when(c == nc - 1)
    def _():
        for h in range(GLA_HEADS):
            s_out[0, h] = st_ref[h].T


def _gla_prompt(xa, sm, wg2, bgate, normw, *, lg):
    nb, l = xa.shape[0], xa.shape[1]
    nc = l // lg

    def seg(width, off):
        return pl.BlockSpec((1, lg, width), lambda b, c: (b, c, off // width))

    def full(shape):
        return pl.BlockSpec(shape, lambda b, c: (0,) * len(shape))

    return pl.pallas_call(
        functools.partial(_gla_body, lg=lg, nc=nc),
        grid=(nb, nc),
        in_specs=[seg(512, GLA_GQ), seg(512, GLA_GK), seg(1024, GLA_GV), seg(1024, GLA_GG),
                  pl.BlockSpec((1, lg, LANES), lambda b, c: (b, c, 0)),
                  full((LANES, 512)), full((1, 512)), full((1, GLA_DV))],
        out_specs=[pl.BlockSpec((1, lg, 1024), lambda b, c: (b, c, 0)),
                   pl.BlockSpec((1, GLA_HEADS, GLA_DK, GLA_DV), lambda b, c: (b, 0, 0, 0))],
        out_shape=[jax.ShapeDtypeStruct((nb, l, 1024), BF16),
                   jax.ShapeDtypeStruct((nb, GLA_HEADS, GLA_DK, GLA_DV), F32)],
        scratch_shapes=[pltpu.VMEM((GLA_HEADS, GLA_DV, GLA_DK), F32), pltpu.VMEM((lg, 512), F32),
                        pltpu.VMEM((lg, 512), F32), pltpu.VMEM((lg, 512), F32), pltpu.VMEM((lg, 1024), F32)],
        compiler_params=_params(("parallel", "arbitrary")),
        name="gla_prompt",
    )(xa, xa, xa, xa, sm, wg2, bgate, normw)


def _ssd_body(z_ref, x_ref, b_ref, c_ref, sm_ref, cwx, cbx, cwb, cbb, cwc, cbc, dtb_ref, alog_ref, e_ref, et_ref,
              dvec_ref, nw_ref, y_ref, s_out, cx_out, cb_out, cc_out, s_ref, bufx, bufb, bufc, y_scr, *, cs, nc):
    c = pl.program_id(1)

    @pl.when(c == 0)
    def _():
        s_ref[...] = jnp.zeros(s_ref.shape, F32)
        bufx[pl.ds(0, SUBLANES), :] = jnp.zeros((SUBLANES, bufx.shape[1]), F32)
        bufb[pl.ds(0, SUBLANES), :] = jnp.zeros((SUBLANES, bufb.shape[1]), F32)
        bufc[pl.ds(0, SUBLANES), :] = jnp.zeros((SUBLANES, bufc.shape[1]), F32)

    xs = _silu(_conv_chunk(bufx, x_ref[0], cwx, cbx, cs))
    bm = _silu(_conv_chunk(bufb, b_ref[0], cwb, cbb, cs)).astype(BF16)
    cm = _silu(_conv_chunk(bufc, c_ref[0], cwc, cbc, cs)).astype(BF16)

    dt = _softplus(sm_ref[0] + dtb_ref[...])
    da = dt * (-jnp.exp(alog_ref[...]))
    lower = _lower_mask(cs)
    tri = jnp.where(lower, 1.0, 0.0).astype(BF16)
    cum = _dot01_exact(tri, da)
    cum_t = cum.T
    dt_t = dt.T
    last = cum[cs - 1:cs, :]
    ecum_x = _dot_x01(jnp.exp(cum), e_ref[...])
    w_x = _dot_x01(jnp.exp(last - cum) * dt, e_ref[...])
    elast = jnp.broadcast_to(jnp.exp(last), (SUBLANES, LANES))
    eh, el = _split2(elast)
    ecol = _dot_nt(et_ref[...], eh) + _dot_nt(et_ref[...], el)

    xs_bf = xs.astype(BF16)
    xw = (xs * w_x).astype(BF16)
    z = z_ref[0].astype(F32)
    lane = lax.broadcasted_iota(jnp.int32, (cs, LANES), 1)
    for g in range(SSD_GROUPS):
        gn = slice(g * SSD_D_STATE, (g + 1) * SSD_D_STATE)
        gw = slice(g * SSD_GROUP_W, (g + 1) * SSD_GROUP_W)
        cg, bg = cm[:, gn], bm[:, gn]
        cb = _dot_nt(cg, bg)
        sg = s_ref[pl.ds(g * SSD_GROUP_W, SSD_GROUP_W), :]
        y_inter = _dot_nt(cg, sg.astype(BF16)) * ecum_x[:, gw]
        ss = jnp.zeros((cs, 1), F32)
        for p in range(SSD_HPG // 2):
            pair = slice(g * SSD_GROUP_W + p * LANES, g * SSD_GROUP_W + (p + 1) * LANES)
            xp = xs_bf[:, pair]
            ys = []
            for h in (g * SSD_HPG + 2 * p, g * SSD_HPG + 2 * p + 1):
                seg = jnp.exp(jnp.where(lower, cum[:, h:h + 1] - cum_t[h:h + 1, :], NEG_BIG))
                m = cb * seg * dt_t[h:h + 1, :]
                ys.append(_dot(m.astype(BF16), xp))
            yp = jnp.where(lane < SSD_HEAD_DIM, ys[0], ys[1])
            yp = yp + y_inter[:, p * LANES:(p + 1) * LANES] + dvec_ref[:, pair] * xs[:, pair]
            yp = yp * _silu(z[:, pair])
            ss = ss + jnp.sum(yp * yp, axis=-1, keepdims=True)
            y_scr[:, pair] = yp
        inv = lax.rsqrt(ss * (1.0 / SSD_GROUP_W) + NORM_EPS)
        y_ref[0, :, gw] = (y_scr[:, gw] * inv * nw_ref[:, gw]).astype(BF16)
        s_ref[pl.ds(g * SSD_GROUP_W, SSD_GROUP_W), :] = (
            ecol[g * SSD_GROUP_W:(g + 1) * SSD_GROUP_W, 0:1] * sg + _dot_tn(xw[:, gw], bg))

    @pl.when(c == nc - 1)
    def _():
        s_out[0] = s_ref[...]
        cx_out[0] = bufx[pl.ds(5, 3), :]
        cb_out[0] = bufb[pl.ds(5, 3), :]
        cc_out[0] = bufc[pl.ds(5, 3), :]


def _ssd_prompt(xb, sm, cw, cb, dtb, alog, emat, emat_t, dvec, normw, *, cs):
    nb, l = xb.shape[0], xb.shape[1]
    nc = l // cs

    def seg(width, off):
        return pl.BlockSpec((1, cs, width), lambda b, c: (b, c, off // width))

    def full(shape):
        return pl.BlockSpec(shape, lambda b, c: (0,) * len(shape))

    def tail(width):
        return pl.BlockSpec((1, CONV_W - 1, width), lambda b, c: (b, 0, 0))

    cwx, cwb, cwc = cw[:, :2048], cw[:, 2048:2560], cw[:, 2560:]
    cbx, cbb, cbc = cb[:, :2048], cb[:, 2048:2560], cb[:, 2560:]
    return pl.pallas_call(
        functools.partial(_ssd_body, cs=cs, nc=nc),
        grid=(nb, nc),
        in_specs=[seg(2048, SSD_Z), seg(2048, SSD_X), seg(512, SSD_B), seg(512, SSD_C),
                  pl.BlockSpec((1, cs, LANES), lambda b, c: (b, c, 0)),
                  full((CONV_W, 2048)), full((1, 2048)), full((CONV_W, 512)), full((1, 512)),
                  full((CONV_W, 512)), full((1, 512)), full((1, LANES)), full((1, LANES)),
                  full((LANES, 2048)), full((2048, LANES)), full((1, 2048)), full((1, 2048))],
        out_specs=[pl.BlockSpec((1, cs, 2048), lambda b, c: (b, c, 0)),
                   pl.BlockSpec((1, 2048, SSD_D_STATE), lambda b, c: (b, 0, 0)),
                   tail(2048), tail(512), tail(512)],
        out_shape=[jax.ShapeDtypeStruct((nb, l, 2048), BF16),
                   jax.ShapeDtypeStruct((nb, 2048, SSD_D_STATE), F32),
                   jax.ShapeDtypeStruct((nb, CONV_W - 1, 2048), F32),
                   jax.ShapeDtypeStruct((nb, CONV_W - 1, 512), F32),
                   jax.ShapeDtypeStruct((nb, CONV_W - 1, 512), F32)],
        scratch_shapes=[pltpu.VMEM((2048, SSD_D_STATE), F32),
                        pltpu.VMEM((SUBLANES, 2048), F32),
                        pltpu.VMEM((SUBLANES, 512), F32),
                        pltpu.VMEM((SUBLANES, 512), F32),
                        pltpu.VMEM((cs, 2048), F32)],
        compiler_params=_params(("parallel", "arbitrary")),
        name="ssd_prompt",
    )(xb, xb, xb, xb, sm, cwx, cbx, cwb, cbb, cwc, cbc, dtb, alog, emat, emat_t, dvec, normw)


def _lru_gates(xc, wa_ref, ba_ref, wx_ref, bx_ref, lam_ref):
    xb = xc.astype(BF16)
    rl, il = [], []
    for n in range(LRU_BLOCKS):
        blk = xb[:, n * LRU_BLOCK_W:(n + 1) * LRU_BLOCK_W]
        rl.append(_dot(blk, wa_ref[n]))
        il.append(_dot(blk, wx_ref[n]))
    r_logit = jnp.concatenate(rl, axis=-1) + ba_ref[...]
    i_logit = jnp.concatenate(il, axis=-1) + bx_ref[...]
    log_a = -LRU_C * _sigmoid(r_logit) * _softplus(-lam_ref[...])
    a = jnp.exp(log_a)
    b = jnp.sqrt(1.0 - jnp.exp(2.0 * log_a)) * (_sigmoid(i_logit) * xc)
    return a, b


def _gelu_tanh(x):
    return 0.5 * x * (1.0 + jnp.tanh(math.sqrt(2.0 / math.pi) * (x + 0.044715 * (x * x * x))))


def _lru_body(rg_ref, rx_ref, cw, cbias, wa_ref, ba_ref, wx_ref, bx_ref, lam_ref, o_ref, h_out, c_out,
              h_ref, buf, a_scr, b_scr, o_scr, *, cs, nc):
    c = pl.program_id(1)

    @pl.when(c == 0)
    def _():
        h_ref[...] = jnp.zeros(h_ref.shape, F32)
        buf[pl.ds(0, SUBLANES), :] = jnp.zeros((SUBLANES, LRU_WIDTH), F32)

    xc = _conv_chunk(buf, rx_ref[0], cw, cbias, cs)
    a, b = _lru_gates(xc, wa_ref, ba_ref, wx_ref, bx_ref, lam_ref)
    rowm = lax.broadcasted_iota(jnp.int32, (cs, LRU_WIDTH), 0) % SUBLANES
    for sh in (1, 2, 4):
        keep = rowm >= sh
        a_prev = jnp.where(keep, pltpu.roll(a, sh, 0), 1.0)
        b_prev = jnp.where(keep, pltpu.roll(b, sh, 0), 0.0)
        b = a * b_prev + b
        a = a * a_prev
    a_scr[...] = a
    b_scr[...] = b
    h = h_ref[...]
    for t in range(cs // SUBLANES):
        rows = pl.ds(t * SUBLANES, SUBLANES)
        ht = a_scr[rows, :] * h + b_scr[rows, :]
        o_scr[rows, :] = ht
        h = jnp.broadcast_to(ht[SUBLANES - 1:SUBLANES, :], (SUBLANES, LRU_WIDTH))
    h_ref[...] = h
    o_ref[0] = (_gelu_tanh(rg_ref[0].astype(F32)) * o_scr[...]).astype(BF16)

    @pl.when(c == nc - 1)
    def _():
        h_out[0] = h[0:1, :]
        c_out[0] = buf[pl.ds(5, 3), :]


def _lru_prompt(xl, cw, cbias, wa, ba, wx, bx, lam, *, cs):
    nb, l = xl.shape[0], xl.shape[1]
    nc = l // cs

    def full(shape):
        return pl.BlockSpec(shape, lambda b, c: (0,) * len(shape))

    return pl.pallas_call(
        functools.partial(_lru_body, cs=cs, nc=nc),
        grid=(nb, nc),
        in_specs=[pl.BlockSpec((1, cs, LRU_WIDTH), lambda b, c: (b, c, LRU_RG // LRU_WIDTH)),
                  pl.BlockSpec((1, cs, LRU_WIDTH), lambda b, c: (b, c, LRU_RX // LRU_WIDTH)),
                  full((CONV_W, LRU_WIDTH)), full((1, LRU_WIDTH)),
                  full((LRU_BLOCKS, LRU_BLOCK_W, LRU_BLOCK_W)), full((1, LRU_WIDTH)),
                  full((LRU_BLOCKS, LRU_BLOCK_W, LRU_BLOCK_W)), full((1, LRU_WIDTH)), full((1, LRU_WIDTH))],
        out_specs=[pl.BlockSpec((1, cs, LRU_WIDTH), lambda b, c: (b, c, 0)),
                   pl.BlockSpec((1, 1, LRU_WIDTH), lambda b, c: (b, 0, 0)),
                   pl.BlockSpec((1, CONV_W - 1, LRU_WIDTH), lambda b, c: (b, 0, 0))],
        out_shape=[jax.ShapeDtypeStruct((nb, l, LRU_WIDTH), BF16),
                   jax.ShapeDtypeStruct((nb, 1, LRU_WIDTH), F32),
                   jax.ShapeDtypeStruct((nb, CONV_W - 1, LRU_WIDTH), F32)],
        scratch_shapes=[pltpu.VMEM((SUBLANES, LRU_WIDTH), F32),
                        pltpu.VMEM((SUBLANES, LRU_WIDTH), F32),
                        pltpu.VMEM((cs, LRU_WIDTH), F32),
                        pltpu.VMEM((cs, LRU_WIDTH), F32),
                        pltpu.VMEM((cs, LRU_WIDTH), F32)],
        compiler_params=_params(("parallel", "arbitrary")),
        name="lru_prompt",
    )(xl, xl, cw, cbias, wa, ba, wx, bx, lam)


def _l2norm(x):
    return x * lax.rsqrt(jnp.sum(x * x, axis=-1, keepdims=True) + NORM_EPS)


def _cat3(a, b, c, axis):
    return jnp.concatenate([a, b, c], axis=axis)


def _gdn_body(q_ref, k_ref, v_ref, z_ref, sm_ref, cwq, cwk, cwv, alog_ref, dtb_ref, nw_ref,
              o_ref, s_out, cq_out, ck_out, cv_out, s_ref, bufq, bufk, bufv,
              ph_scr, pl_scr, t_scr, qb_scr, kd_scr, aqk_scr, rh_scr, rl_scr, *, cs, nc):
    c = pl.program_id(1)

    @pl.when(c == 0)
    def _():
        s_ref[...] = jnp.zeros(s_ref.shape, F32)
        for buf in (bufq, bufk, bufv):
            buf[pl.ds(0, SUBLANES), :] = jnp.zeros((SUBLANES, buf.shape[1]), F32)

    q = _silu(_conv_chunk(bufq, q_ref[0], cwq, None, cs))
    k = _silu(_conv_chunk(bufk, k_ref[0], cwk, None, cs))
    v = _silu(_conv_chunk(bufv, v_ref[0], cwv, None, cs))
    sm = sm_ref[0]
    g_all = -jnp.exp(alog_ref[...]) * _softplus(sm + dtb_ref[...])
    beta_all = _sigmoid(sm)
    lower = _lower_mask(cs)
    strict = _lower_mask(cs, strict=True)
    diag = lower & jnp.logical_not(strict)
    tri = jnp.where(lower, 1.0, 0.0).astype(BF16)
    gc = _dot01_exact(tri, g_all)
    gc_t = gc.T
    egc_all = jnp.exp(gc)

    for h in range(GDN_HEADS):
        hd = slice(h * GDN_DK, (h + 1) * GDN_DK)
        qh = _l2norm(q[:, hd]) * (GDN_DK ** -0.5)
        kh = _l2norm(k[:, hd])
        beta = beta_all[:, GDN_HEADS + h:GDN_HEADS + h + 1]
        gcol = gc[:, h:h + 1]
        dec = jnp.exp(jnp.where(lower, gcol - gc_t[h:h + 1, :], NEG_BIG))
        qb, kb = qh.astype(BF16), kh.astype(BF16)
        both = _dot_nt(jnp.concatenate([kb, qb], axis=0), kb)
        p = -(beta * jnp.where(strict, dec, 0.0) * both[:cs])
        hi, lo = _split2(p)
        ph_scr[h] = hi
        pl_scr[h] = lo
        t_scr[h] = jnp.where(diag, 1.0, p)
        qb_scr[h] = qb
        aqk_scr[h] = (dec * both[cs:]).astype(BF16)
        glast = gc[cs - 1:cs, h:h + 1]
        kd_scr[h] = (kh * jnp.exp(glast - gcol)).astype(BF16)
        rhs = jnp.concatenate([v[:, hd] * beta, kh * (beta * egc_all[:, h:h + 1])], axis=-1)
        hi, lo = _split2(rhs)
        rh_scr[h] = hi
        rl_scr[h] = lo

    levels = int(math.log2(cs))
    for lvl in range(levels):
        for h in range(GDN_HEADS):
            hi, lo = ph_scr[h], pl_scr[h]
            lhs = _cat3(hi, hi, lo, 1)
            if lvl == 0:
                p2 = _dot(lhs, _cat3(hi, lo, hi, 0))
                t_new = None
            else:
                t = t_scr[h]
                th, tl = _split2(t)
                if lvl < levels - 1:
                    rhs = _cat3(jnp.concatenate([hi, th], axis=1), jnp.concatenate([lo, tl], axis=1),
                                jnp.concatenate([hi, th], axis=1), 0)
                    both = _dot(lhs, rhs)
                    p2, t_new = both[:, :cs], t + both[:, cs:]
                else:
                    p2, t_new = None, t + _dot(lhs, _cat3(th, tl, th, 0))
            if p2 is not None:
                hi2, lo2 = _split2(p2)
                ph_scr[h] = hi2
                pl_scr[h] = lo2
            if t_new is not None:
                t_scr[h] = t_new

    z = z_ref[0].astype(F32)
    for h in range(GDN_HEADS):
        hd = slice(h * GDN_DK, (h + 1) * GDN_DK)
        th, tl = _split2(t_scr[h])
        rh, rl = rh_scr[h], rl_scr[h]
        sol = _dot(_cat3(th, th, tl, 1), _cat3(rh, rl, rh, 0))
        u, w = sol[:, :GDN_DV], sol[:, GDN_DV:]
        sh = s_ref[h]
        sb = sh.astype(BF16)
        qb = qb_scr[h]
        ws_qs = _dot(jnp.concatenate([w.astype(BF16), qb], axis=0), sb)
        db = (u - ws_qs[:cs]).astype(BF16)
        o = egc_all[:, h:h + 1] * ws_qs[cs:] + _dot(aqk_scr[h], db)
        s_ref[h] = jnp.exp(gc[cs - 1:cs, h:h + 1]) * sh + _dot_tn(kd_scr[h], db)
        o_ref[0, :, hd] = (_rms(o, nw_ref[...]) * _silu(z[:, hd])).astype(BF16)

    @pl.when(c == nc - 1)
    def _():
        s_out[0] = s_ref[...]
        cq_out[0] = bufq[pl.ds(5, 3), :]
        ck_out[0] = bufk[pl.ds(5, 3), :]
        cv_out[0] = bufv[pl.ds(5, 3), :]


def _gdn_prompt(xg, sm, cw, alog, dtb, normw, *, cs):
    nb, l = xg.shape[0], xg.shape[1]
    nc = l // cs

    def seg(width, off):
        return pl.BlockSpec((1, cs, width), lambda b, c: (b, c, off // width))

    def full(shape):
        return pl.BlockSpec(shape, lambda b, c: (0,) * len(shape))

    def tail(width):
        return pl.BlockSpec((1, CONV_W - 1, width), lambda b, c: (b, 0, 0))

    hm = (GDN_HEADS, cs, cs)
    return pl.pallas_call(
        functools.partial(_gdn_body, cs=cs, nc=nc),
        grid=(nb, nc),
        in_specs=[seg(1024, GDN_Q), seg(1024, GDN_K), seg(1024, GDN_V), seg(1024, GDN_Z),
                  pl.BlockSpec((1, cs, LANES), lambda b, c: (b, c, 0)),
                  full((CONV_W, 1024)), full((CONV_W, 1024)), full((CONV_W, 1024)),
                  full((1, LANES)), full((1, LANES)), full((1, GDN_DV))],
        out_specs=[pl.BlockSpec((1, cs, 1024), lambda b, c: (b, c, 0)),
                   pl.BlockSpec((1, GDN_HEADS, GDN_DK, GDN_DV), lambda b, c: (b, 0, 0, 0)),
                   tail(1024), tail(1024), tail(1024)],
        out_shape=[jax.ShapeDtypeStruct((nb, l, 1024), BF16),
                   jax.ShapeDtypeStruct((nb, GDN_HEADS, GDN_DK, GDN_DV), F32)]
        + [jax.ShapeDtypeStruct((nb, CONV_W - 1, 1024), F32)] * 3,
        scratch_shapes=[pltpu.VMEM((GDN_HEADS, GDN_DK, GDN_DV), F32)]
        + [pltpu.VMEM((SUBLANES, 1024), F32)] * 3
        + [pltpu.VMEM(hm, BF16), pltpu.VMEM(hm, BF16), pltpu.VMEM(hm, F32),
           pltpu.VMEM((GDN_HEADS, cs, GDN_DK), BF16), pltpu.VMEM((GDN_HEADS, cs, GDN_DK), BF16),
           pltpu.VMEM(hm, BF16), pltpu.VMEM((GDN_HEADS, cs, 2 * GDN_DV), BF16),
           pltpu.VMEM((GDN_HEADS, cs, 2 * GDN_DV), BF16)],
        compiler_params=_params(("parallel", "arbitrary")),
        name="gdn_prompt",
    )(xg, xg, xg, xg, sm, cw[:, :1024], cw[:, 1024:2048], cw[:, 2048:], alog, dtb, normw)


def _pad_lanes(v, width, offset=0):
    return jnp.zeros((1, width), F32).at[0, offset:offset + v.shape[0]].set(v.astype(F32))


def _prep(ffn_w_in, ffn_w_out, l0_w_in, l0_w_out, gla_w_gate2, gla_b_gate, gla_norm, ssd_conv_w, ssd_conv_b,
          ssd_dt_bias, ssd_a_log, ssd_d, ssd_norm, l1_w_in, l1_w_out, lru_conv_w, lru_conv_b, lru_w_a, lru_b_a,
          lru_w_x, lru_b_x, lru_lambda, gdn_conv_w, gdn_a_log, gdn_dt_bias, gdn_norm):
    w = {}
    w["ffn_in"] = [[ffn_w_in[l, i].astype(BF16) for i in range(2)] for l in range(2)]
    w["ffn_out"] = [[ffn_w_out[l, i].astype(BF16) for i in range(2)] for l in range(2)]
    c = l0_w_in
    small0 = jnp.concatenate([c[:, 8208:8240], c[:, 3072:3088], jnp.zeros((D_MODEL, LANES - 48), F32)], axis=1)
    w["l0_in"] = jnp.concatenate(
        [c[:, 1024:2048], c[:, 2048:3072], c[:, 0:512], c[:, 512:1024],
         c[:, 3088:5136], c[:, 5136:7184], c[:, 7184:7696], c[:, 7696:8208],
         small0], axis=1).astype(BF16)
    w["l0_out_a"] = l0_w_out[:1024].astype(BF16)
    w["l0_out_b"] = l0_w_out[1024:].astype(BF16)
    w["gla_wg2"] = jnp.zeros((LANES, 512), F32).at[32:48].set(gla_w_gate2).astype(BF16)
    w["gla_bg"] = gla_b_gate.reshape(1, 512)
    w["gla_norm"] = gla_norm.reshape(1, GLA_DV)
    w["ssd_cw"] = ssd_conv_w
    w["ssd_cb"] = ssd_conv_b.reshape(1, -1)
    w["ssd_dtb"] = _pad_lanes(ssd_dt_bias, LANES)
    w["ssd_alog"] = _pad_lanes(ssd_a_log, LANES)
    head_of_lane = jnp.arange(SSD_D_INNER) // SSD_HEAD_DIM
    emat = (jnp.arange(LANES)[:, None] == head_of_lane[None, :])
    w["ssd_e"] = emat.astype(BF16)
    w["ssd_et"] = emat.T.astype(BF16)
    w["ssd_dvec"] = jnp.repeat(ssd_d, SSD_HEAD_DIM).reshape(1, SSD_D_INNER)
    w["ssd_norm"] = ssd_norm.reshape(1, SSD_D_INNER)
    c = l1_w_in
    small1 = jnp.concatenate([c[:, 6656:6672], jnp.zeros((D_MODEL, LANES - 16), F32)], axis=1)
    w["l1_in"] = jnp.concatenate(
        [c[:, 2560:3584], c[:, 3584:4608], c[:, 4608:5632], c[:, 5632:6656],
         c[:, 0:1280], c[:, 1280:2560],
         small1], axis=1).astype(BF16)
    w["l1_out_a"] = l1_w_out[:LRU_WIDTH].astype(BF16)
    w["l1_out_b"] = l1_w_out[LRU_WIDTH:].astype(BF16)
    w["lru_cw"] = lru_conv_w
    w["lru_cb"] = lru_conv_b.reshape(1, -1)
    w["lru_wa"] = lru_w_a.astype(BF16)
    w["lru_ba"] = lru_b_a.reshape(1, -1)
    w["lru_wx"] = lru_w_x.astype(BF16)
    w["lru_bx"] = lru_b_x.reshape(1, -1)
    w["lru_lam"] = lru_lambda.reshape(1, -1)
    w["gdn_cw"] = gdn_conv_w
    w["gdn_alog"] = _pad_lanes(gdn_a_log, LANES)
    w["gdn_dtb"] = _pad_lanes(gdn_dt_bias, LANES)
    w["gdn_norm"] = gdn_norm.reshape(1, GDN_DV)
    return w


L0_WIDTHS = (GLA_W, SSD_W, LANES)
L1_WIDTHS = (GDN_W, LRU_W2, LANES)


def _trunk_prompt(x, norms, final_norm, w, *, tm, cs):
    nb, l = x.shape[0], x.shape[1]
    t = nb * l
    h = x.reshape(t, D_MODEL)
    h = _ffn(h, norms[0, 0], w["ffn_in"][0][0], w["ffn_out"][0][0], tm=tm)
    xa, xb, sm = _proj(h, norms[0, 1], w["l0_in"], L0_WIDTHS, (BF16, BF16, F32), tm=tm)
    sm = sm.reshape(nb, l, LANES)
    o_a, s_gla = _gla_prompt(xa.reshape(nb, l, GLA_W), sm, w["gla_wg2"], w["gla_bg"], w["gla_norm"], lg=cs)
    y, s_ssd, cx, cb, cc = _ssd_prompt(xb.reshape(nb, l, SSD_W), sm, w["ssd_cw"], w["ssd_cb"], w["ssd_dtb"],
                                       w["ssd_alog"], w["ssd_e"], w["ssd_et"], w["ssd_dvec"], w["ssd_norm"], cs=cs)
    h = _ffn(h, norms[0, 2], w["ffn_in"][0][1], w["ffn_out"][0][1],
             pre=(o_a.reshape(t, 1024), y.reshape(t, 2048), w["l0_out_a"], w["l0_out_b"]), tm=tm)
    h = _ffn(h, norms[1, 0], w["ffn_in"][1][0], w["ffn_out"][1][0], tm=tm)
    xg, xl, sm = _proj(h, norms[1, 1], w["l1_in"], L1_WIDTHS, (BF16, BF16, F32), tm=tm)
    sm = sm.reshape(nb, l, LANES)
    o_c, s_lru, s_lru_conv = _lru_prompt(xl.reshape(nb, l, LRU_W2), w["lru_cw"], w["lru_cb"], w["lru_wa"],
                                         w["lru_ba"], w["lru_wx"], w["lru_bx"], w["lru_lam"], cs=cs)
    o_d, s_gdn, cq, ck, cv = _gdn_prompt(xg.reshape(nb, l, GDN_W), sm, w["gdn_cw"], w["gdn_alog"], w["gdn_dtb"],
                                         w["gdn_norm"], cs=cs)
    h = _ffn(h, norms[1, 2], w["ffn_in"][1][1], w["ffn_out"][1][1], final_norm,
             pre=(o_c.reshape(t, LRU_WIDTH), o_d.reshape(t, 1024), w["l1_out_a"], w["l1_out_b"]), tm=tm)
    return (h.reshape(nb, l, D_MODEL), s_gla, s_ssd.reshape(nb, SSD_HEADS, SSD_HEAD_DIM, SSD_D_STATE),
            jnp.concatenate([cx, cb, cc], axis=-1), s_lru.reshape(nb, LRU_WIDTH), s_lru_conv, s_gdn,
            jnp.concatenate([cq, ck, cv], axis=-1))


def _conv_step(cst_ref, cst_out, cur, w_ref, bias_ref, lo, hi):
    acc = w_ref[CONV_W - 1:CONV_W, lo:hi] * cur
    for j in range(CONV_W - 1):
        acc = acc + w_ref[j:j + 1, lo:hi] * cst_ref[:, j, lo:hi]
    if bias_ref is not None:
        acc = acc + bias_ref[:, lo:hi]
    for j in range(CONV_W - 2):
        cst_out[:, j, lo:hi] = cst_ref[:, j + 1, lo:hi]
    cst_out[:, CONV_W - 2, lo:hi] = cur
    return acc


def _gla_dec_body(q_ref, k_ref, v_ref, gg_ref, sm_ref, wg2_ref, bg_ref, nw_ref, s_in, o_ref, s_out, o_scr, *, nb):
    pre = _dot(sm_ref[...].astype(BF16), wg2_ref[...]) + bg_ref[...]
    a = jnp.exp(-_softplus(-pre) * (1.0 / GLA_GATE_NORM))
    q = q_ref[...] * (GLA_DK ** -0.5)
    v = v_ref[...]
    a_t, q_t, k_t = a.T, q.T, k_ref[...].T
    for b in range(nb):
        for h in range(GLA_HEADS):
            dk = slice(h * GLA_DK, (h + 1) * GLA_DK)
            dv = slice(h * GLA_DV, (h + 1) * GLA_DV)
            sn = a_t[dk, b:b + 1] * s_in[b, h] + k_t[dk, b:b + 1] * v[b:b + 1, dv]
            s_out[b, h] = sn
            o_scr[b:b + 1, dv] = jnp.sum(q_t[dk, b:b + 1] * sn, axis=0, keepdims=True)
    o = o_scr[...]
    gg = gg_ref[...]
    for h in range(GLA_HEADS):
        dv = slice(h * GLA_DV, (h + 1) * GLA_DV)
        o_ref[:, dv] = _rms(o[:, dv], nw_ref[...]) * _silu(gg[:, dv])


def _gla_step(xa, sm, state, wg2, bgate, normw, *, nb):
    n = xa.shape[0]

    def seg(width, off):
        return pl.BlockSpec((nb, width), lambda i: (i, off // width))

    def full(shape):
        return pl.BlockSpec(shape, lambda i: (0,) * len(shape))

    st = pl.BlockSpec((nb, GLA_HEADS, GLA_DK, GLA_DV), lambda i: (i, 0, 0, 0))
    return pl.pallas_call(
        functools.partial(_gla_dec_body, nb=nb),
        grid=(n // nb,),
        in_specs=[seg(512, GLA_GQ), seg(512, GLA_GK), seg(1024, GLA_GV), seg(1024, GLA_GG), seg(LANES, 0),
                  full((LANES, 512)), full((1, 512)), full((1, GLA_DV)), st],
        out_specs=[pl.BlockSpec((nb, 1024), lambda i: (i, 0)), st],
        out_shape=[jax.ShapeDtypeStruct((n, 1024), F32), jax.ShapeDtypeStruct(state.shape, F32)],
        scratch_shapes=[pltpu.VMEM((nb, 1024), F32)],
        compiler_params=_params(("parallel",)),
        name="gla_step",
    )(xa, xa, xa, xa, sm, wg2, bgate, normw, state)


def _ssd_dec_body(z_ref, x_ref, b_ref, c_ref, sm_ref, cst_ref, cw, cbias, dtb_ref, alog_ref, e_ref, dvec_ref,
                  nw_ref, s_in, y_ref, s_out, cst_out, yt_scr, *, nb):
    xs = _silu(_conv_step(cst_ref, cst_out, x_ref[...], cw, cbias, 0, 2048))
    bm = _silu(_conv_step(cst_ref, cst_out, b_ref[...], cw, cbias, 2048, 2560))
    cm = _silu(_conv_step(cst_ref, cst_out, c_ref[...], cw, cbias, 2560, 3072)).astype(BF16)
    dt = _softplus(sm_ref[...] + dtb_ref[...])
    decay = jnp.exp(dt * (-jnp.exp(alog_ref[...])))
    c1_t = _dot_x01(decay, e_ref[...]).T
    c2_t = (_dot_x01(dt, e_ref[...]) * xs).T
    lane = lax.broadcasted_iota(jnp.int32, (SSD_GROUP_W, nb), 1)
    for g in range(SSD_GROUPS):
        gn = slice(g * SSD_D_STATE, (g + 1) * SSD_D_STATE)
        gw = slice(g * SSD_GROUP_W, (g + 1) * SSD_GROUP_W)
        rows = pl.ds(g * SSD_GROUP_W, SSD_GROUP_W)
        ycol = jnp.zeros((SSD_GROUP_W, nb), F32)
        for b in range(nb):
            sn = c1_t[gw, b:b + 1] * s_in[b, rows, :] + c2_t[gw, b:b + 1] * bm[b:b + 1, gn]
            s_out[b, rows, :] = sn
            ycol = jnp.where(lane == b, _dot_nt(sn.astype(BF16), cm[:, gn]), ycol)
        yt_scr[rows, :] = ycol
    y = yt_scr[...].T + dvec_ref[...] * xs
    y = y * _silu(z_ref[...])
    for g in range(SSD_GROUPS):
        gw = slice(g * SSD_GROUP_W, (g + 1) * SSD_GROUP_W)
        y_ref[:, gw] = _rms(y[:, gw], nw_ref[:, gw])


def _ssd_step(xb, sm, state, cstate, cw, cb, dtb, alog, emat, dvec, normw, *, nb):
    n = xb.shape[0]

    def seg(width, off):
        return pl.BlockSpec((nb, width), lambda i: (i, off // width))

    def full(shape):
        return pl.BlockSpec(shape, lambda i: (0,) * len(shape))

    st = pl.BlockSpec((nb, SSD_D_INNER, SSD_D_STATE), lambda i: (i, 0, 0))
    cst = pl.BlockSpec((nb, CONV_W - 1, 3072), lambda i: (i, 0, 0))
    return pl.pallas_call(
        functools.partial(_ssd_dec_body, nb=nb),
        grid=(n // nb,),
        in_specs=[seg(2048, SSD_Z), seg(2048, SSD_X), seg(512, SSD_B), seg(512, SSD_C), seg(LANES, 0), cst,
                  full((CONV_W, 3072)), full((1, 3072)), full((1, LANES)), full((1, LANES)), full((LANES, 2048)),
                  full((1, 2048)), full((1, 2048)), st],
        out_specs=[pl.BlockSpec((nb, 2048), lambda i: (i, 0)), st, cst],
        out_shape=[jax.ShapeDtypeStruct((n, 2048), F32), jax.ShapeDtypeStruct(state.shape, F32),
                   jax.ShapeDtypeStruct(cstate.shape, F32)],
        scratch_shapes=[pltpu.VMEM((SSD_D_INNER, nb), F32)],
        compiler_params=_params(("parallel",)),
        name="ssd_step",
    )(xb, xb, xb, xb, sm, cstate, cw, cb, dtb, alog, emat, dvec, normw, state)


def _lru_dec_body(rg_ref, rx_ref, cst_ref, cw, cbias, wa_ref, ba_ref, wx_ref, bx_ref, lam_ref, h_in,
                  o_ref, h_out, cst_out):
    xc = _conv_step(cst_ref, cst_out, rx_ref[...], cw, cbias, 0, LRU_WIDTH)
    a, b = _lru_gates(xc, wa_ref, ba_ref, wx_ref, bx_ref, lam_ref)
    h = a * h_in[...] + b
    h_out[...] = h
    o_ref[...] = _gelu_tanh(rg_ref[...]) * h


def _lru_step(xl, hstate, cstate, cw, cbias, wa, ba, wx, bx, lam):
    n = xl.shape[0]

    def full(shape):
        return pl.BlockSpec(shape, lambda i: (0,) * len(shape))

    return pl.pallas_call(
        _lru_dec_body,
        grid=(1,),
        in_specs=[pl.BlockSpec((n, LRU_WIDTH), lambda i: (0, LRU_RG // LRU_WIDTH)),
                  pl.BlockSpec((n, LRU_WIDTH), lambda i: (0, LRU_RX // LRU_WIDTH)),
                  full((n, CONV_W - 1, LRU_WIDTH)), full((CONV_W, LRU_WIDTH)), full((1, LRU_WIDTH)),
                  full((LRU_BLOCKS, LRU_BLOCK_W, LRU_BLOCK_W)), full((1, LRU_WIDTH)),
                  full((LRU_BLOCKS, LRU_BLOCK_W, LRU_BLOCK_W)), full((1, LRU_WIDTH)), full((1, LRU_WIDTH)),
                  full((n, LRU_WIDTH))],
        out_specs=[full((n, LRU_WIDTH)), full((n, LRU_WIDTH)), full((n, CONV_W - 1, LRU_WIDTH))],
        out_shape=[jax.ShapeDtypeStruct((n, LRU_WIDTH), F32), jax.ShapeDtypeStruct((n, LRU_WIDTH), F32),
                   jax.ShapeDtypeStruct(cstate.shape, F32)],
        compiler_params=_params(("arbitrary",)),
        name="lru_step",
    )(xl, xl, cstate, cw, cbias, wa, ba, wx, bx, lam, hstate)


def _gdn_dec_body(q_ref, k_ref, v_ref, z_ref, sm_ref, cst_ref, cw, alog_ref, dtb_ref, nw_ref, s_in,
                  o_ref, s_out, cst_out, o_scr, *, nb):
    q = _silu(_conv_step(cst_ref, cst_out, q_ref[...], cw, None, 0, 1024))
    k = _silu(_conv_step(cst_ref, cst_out, k_ref[...], cw, None, 1024, 2048))
    v = _silu(_conv_step(cst_ref, cst_out, v_ref[...], cw, None, 2048, 3072))
    sm = sm_ref[...]
    eg_all = jnp.exp(-jnp.exp(alog_ref[...]) * _softplus(sm + dtb_ref[...]))
    beta_all = _sigmoid(sm)
    qn = jnp.concatenate([_l2norm(q[:, h * GDN_DK:(h + 1) * GDN_DK]) for h in range(GDN_HEADS)], axis=-1)
    qn = qn * (GDN_DK ** -0.5)
    kn = jnp.concatenate([_l2norm(k[:, h * GDN_DK:(h + 1) * GDN_DK]) for h in range(GDN_HEADS)], axis=-1)
    q_t, k_t = qn.T, kn.T
    for b in range(nb):
        for h in range(GDN_HEADS):
            hd = slice(h * GDN_DK, (h + 1) * GDN_DK)
            s = s_in[b, h]
            kcol, qcol = k_t[hd, b:b + 1], q_t[hd, b:b + 1]
            eg = eg_all[b:b + 1, h:h + 1]
            beta = beta_all[b:b + 1, GDN_HEADS + h:GDN_HEADS + h + 1]
            ks = jnp.sum(kcol * s, axis=0, keepdims=True)
            qs = jnp.sum(qcol * s, axis=0, keepdims=True)
            delta = beta * (v[b:b + 1, hd] - eg * ks)
            qk = jnp.sum(qn[b:b + 1, hd] * kn[b:b + 1, hd], axis=-1, keepdims=True)
            o_scr[b:b + 1, hd] = eg * qs + qk * delta
            s_out[b, h] = eg * s + kcol * delta
    o = o_scr[...]
    z = z_ref[...]
    for h in range(GDN_HEADS):
        hd = slice(h * GDN_DK, (h + 1) * GDN_DK)
        o_ref[:, hd] = _rms(o[:, hd], nw_ref[...]) * _silu(z[:, hd])


def _gdn_step(xg, sm, state, cstate, cw, alog, dtb, normw, *, nb):
    n = xg.shape[0]

    def seg(width, off):
        return pl.BlockSpec((nb, width), lambda i: (i, off // width))

    def full(shape):
        return pl.BlockSpec(shape, lambda i: (0,) * len(shape))

    st = pl.BlockSpec((nb, GDN_HEADS, GDN_DK, GDN_DV), lambda i: (i, 0, 0, 0))
    cst = pl.BlockSpec((nb, CONV_W - 1, 3072), lambda i: (i, 0, 0))
    return pl.pallas_call(
        functools.partial(_gdn_dec_body, nb=nb),
        grid=(n // nb,),
        in_specs=[seg(1024, GDN_Q), seg(1024, GDN_K), seg(1024, GDN_V), seg(1024, GDN_Z), seg(LANES, 0), cst,
                  full((CONV_W, 3072)), full((1, LANES)), full((1, LANES)), full((1, GDN_DV)), st],
        out_specs=[pl.BlockSpec((nb, 1024), lambda i: (i, 0)), st, cst],
        out_shape=[jax.ShapeDtypeStruct((n, 1024), F32), jax.ShapeDtypeStruct(state.shape, F32),
                   jax.ShapeDtypeStruct(cstate.shape, F32)],
        scratch_shapes=[pltpu.VMEM((nb, 1024), F32)],
        compiler_params=_params(("parallel",)),
        name="gdn_step",
    )(xg, xg, xg, xg, sm, cstate, cw, alog, dtb, normw, state)


def _trunk_sample(x, s_gla, s_ssd, s_ssd_conv, s_lru, s_lru_conv, s_gdn, s_gdn_conv, norms, final_norm, w, *, nb=8):
    n = x.shape[0]
    f3 = (F32, F32, F32)
    h = x.reshape(n, D_MODEL)
    h = _ffn(h, norms[0, 0], w["ffn_in"][0][0], w["ffn_out"][0][0], tm=n)
    xa, xb, sm = _proj(h, norms[0, 1], w["l0_in"], L0_WIDTHS, f3, tm=n)
    o_a, n_gla = _gla_step(xa, sm, s_gla, w["gla_wg2"], w["gla_bg"], w["gla_norm"], nb=nb)
    y, n_ssd, n_ssd_conv = _ssd_step(xb, sm, s_ssd.reshape(n, SSD_D_INNER, SSD_D_STATE), s_ssd_conv, w["ssd_cw"],
                                     w["ssd_cb"], w["ssd_dtb"], w["ssd_alog"], w["ssd_e"], w["ssd_dvec"],
                                     w["ssd_norm"], nb=nb)
    h = _ffn(h, norms[0, 2], w["ffn_in"][0][1], w["ffn_out"][0][1], pre=(o_a, y, w["l0_out_a"], w["l0_out_b"]), tm=n)
    h = _ffn(h, norms[1, 0], w["ffn_in"][1][0], w["ffn_out"][1][0], tm=n)
    xg, xl, sm = _proj(h, norms[1, 1], w["l1_in"], L1_WIDTHS, f3, tm=n)
    o_c, n_lru, n_lru_conv = _lru_step(xl, s_lru, s_lru_conv, w["lru_cw"], w["lru_cb"], w["lru_wa"], w["lru_ba"],
                                       w["lru_wx"], w["lru_bx"], w["lru_lam"])
    o_d, n_gdn, n_gdn_conv = _gdn_step(xg, sm, s_gdn, s_gdn_conv, w["gdn_cw"], w["gdn_alog"], w["gdn_dtb"],
                                       w["gdn_norm"], nb=nb)
    h = _ffn(h, norms[1, 2], w["ffn_in"][1][1], w["ffn_out"][1][1], final_norm,
             pre=(o_c, o_d, w["l1_out_a"], w["l1_out_b"]), tm=n)
    return (h.reshape(n, 1, D_MODEL), n_gla, n_ssd.reshape(s_ssd.shape), n_ssd_conv, n_lru, n_lru_conv, n_gdn,
            n_gdn_conv)


def kernel(x_prompt, x_sample, state_gla, state_ssd, state_ssd_conv, state_lru, state_lru_conv, state_gdn,
           state_gdn_conv, norms, final_norm, ffn_w_in, ffn_w_out, l0_w_in, l0_w_out, gla_w_gate2, gla_b_gate,
           gla_norm, ssd_conv_w, ssd_conv_b, ssd_dt_bias, ssd_a_log, ssd_d, ssd_norm, l1_w_in, l1_w_out,
           lru_conv_w, lru_conv_b, lru_w_a, lru_b_a, lru_w_x, lru_b_x, lru_lambda, gdn_conv_w, gdn_a_log,
           gdn_dt_bias, gdn_norm):
    w = _prep(ffn_w_in, ffn_w_out, l0_w_in, l0_w_out, gla_w_gate2, gla_b_gate, gla_norm, ssd_conv_w, ssd_conv_b,
              ssd_dt_bias, ssd_a_log, ssd_d, ssd_norm, l1_w_in, l1_w_out, lru_conv_w, lru_conv_b, lru_w_a, lru_b_a,
              lru_w_x, lru_b_x, lru_lambda, gdn_conv_w, gdn_a_log, gdn_dt_bias, gdn_norm)
    prompt = _trunk_prompt(x_prompt, norms, final_norm, w, tm=512, cs=128)
    sample = _trunk_sample(x_sample, state_gla, state_ssd, state_ssd_conv, state_lru, state_lru_conv, state_gdn,
                           state_gdn_conv, norms, final_norm, w)
    return (prompt[0], sample[0]) + tuple(prompt[1:]) + tuple(sample[1:])
```

```python
import functools
import math

import jax
import jax.numpy as jnp
from jax import lax
from jax.experimental import pallas as pl
from jax.experimental.pallas import tpu as pltpu

F32 = jnp.float32
BF16 = jnp.bfloat16

D_MODEL = 1024
NORM_EPS = 1e-6
CONV_W = 4
D_FF = 2816
FFN_RES = 0.5

GLA_HEADS = 4
GLA_DK = 128
GLA_DV = 256
GLA_RANK = 16
GLA_GATE_NORM = 16.0
GLA_SUB = 16
GLA_FAST_MAX = 60.0

SSD_D_INNER = 2048
SSD_HEAD_DIM = 64
SSD_HEADS = 32
SSD_GROUPS = 4
SSD_HPG = 8
SSD_D_STATE = 128
SSD_GROUP_W = SSD_HPG * SSD_HEAD_DIM

LRU_WIDTH = 1280
LRU_BLOCKS = 10
LRU_BLOCK_W = 128
LRU_C = 8.0

GDN_HEADS = 8
GDN_DK = 128
GDN_DV = 128

LANES = 128
SUBLANES = 8
NEG_BIG = -1e30
VMEM_LIMIT = 56 * 1024 * 1024

GLA_W = 3072
GLA_GV, GLA_GG, GLA_GQ, GLA_GK = 0, 1024, 2048, 2560
SSD_W = 5120
SSD_Z, SSD_X, SSD_B, SSD_C = 0, 2048, 4096, 4608
GDN_W = 4096
GDN_Q, GDN_K, GDN_V, GDN_Z = 0, 1024, 2048, 3072
LRU_W2 = 2560
LRU_RG, LRU_RX = 0, 1280


def _sigmoid(x):
    return 0.5 * jnp.tanh(0.5 * x) + 0.5


def _silu(x):
    hx = 0.5 * x
    return hx * jnp.tanh(hx) + hx


def _softplus(x):
    return jnp.maximum(x, 0.0) + jnp.log(1.0 + jnp.exp(-jnp.abs(x)))


def _rms(x, g):
    return x * lax.rsqrt(jnp.mean(x * x, axis=-1, keepdims=True) + NORM_EPS) * g


def _dot(a, b):
    return jnp.dot(a, b, preferred_element_type=F32)


def _dot_nt(a, b):
    return lax.dot_general(a, b, (((1,), (1,)), ((), ())), preferred_element_type=F32)


def _dot_tn(a, b):
    return lax.dot_general(a, b, (((0,), (0,)), ((), ())), preferred_element_type=F32)


def _split2(x):
    hi = x.astype(BF16)
    lo = (x - hi.astype(F32)).astype(BF16)
    return hi, lo


def _split3(x):
    hi = x.astype(BF16)
    r = x - hi.astype(F32)
    mid = r.astype(BF16)
    lo = (r - mid.astype(F32)).astype(BF16)
    return hi, mid, lo


def _dot01_exact(m01, x):
    hi, mid, lo = _split3(x)
    return _dot(m01, hi) + _dot(m01, mid) + _dot(m01, lo)


def _dot_x01(x, m01):
    hi, lo = _split2(x)
    return _dot(hi, m01) + _dot(lo, m01)


def _lower_mask(n, strict=False):
    row = lax.broadcasted_iota(jnp.int32, (n, n), 0)
    col = lax.broadcasted_iota(jnp.int32, (n, n), 1)
    return row > col if strict else row >= col


def _params(sem):
    return pltpu.CompilerParams(dimension_semantics=sem, vmem_limit_bytes=VMEM_LIMIT)


FFN_CHUNK = 256


def _resident(shape):
    return pl.BlockSpec(shape, lambda i: (0,) * len(shape), pipeline_mode=pl.Buffered(1))


def _ffn_body(*refs, final, pre):
    refs = list(refs)
    x_ref = refs.pop(0)
    if pre:
        a_ref, b_ref, wab_ref = refs[:3]
        refs = refs[3:]
    g_ref, wi_ref, wo_ref = refs[:3]
    refs = refs[3:]
    fg_ref = refs.pop(0) if final else None
    o_ref, act_ref = refs
    x = x_ref[...]
    if pre:
        ka = a_ref.shape[1]
        x = x + _dot(a_ref[...].astype(BF16), wab_ref[:ka, :]) + _dot(b_ref[...].astype(BF16), wab_ref[ka:, :])
    hn = _rms(x, g_ref[...]).astype(BF16)
    for c in range(D_FF // FFN_CHUNK):
        cols = slice(c * FFN_CHUNK, (c + 1) * FFN_CHUNK)
        gate = _dot(hn, wi_ref[:, cols])
        up = _dot(hn, wi_ref[:, D_FF + c * FFN_CHUNK:D_FF + (c + 1) * FFN_CHUNK])
        act_ref[:, cols] = (_silu(gate) * up).astype(BF16)
    y = x + FFN_RES * _dot(act_ref[...], wo_ref[...])
    if final:
        y = _rms(y, fg_ref[...])
    o_ref[...] = y


def _ffn(x, g, w_in, w_out, sel, final_g=None, pre=None, *, tm):
    t = x.shape[0]
    final = final_g is not None
    in_specs = [pl.BlockSpec((tm, D_MODEL), lambda i: (i, 0))]
    args = [x]
    if pre is not None:
        a, b, wab = pre
        in_specs += [pl.BlockSpec((tm, a.shape[1]), lambda i: (i, 0)), pl.BlockSpec((tm, b.shape[1]), lambda i: (i, 0)),
                     _resident(wab.shape)]
        args += [a, b, wab]
    in_specs += [_resident((1, D_MODEL)),
                 pl.BlockSpec((None, None) + w_in.shape[2:], lambda i: sel + (0, 0), pipeline_mode=pl.Buffered(1)),
                 pl.BlockSpec((None, None) + w_out.shape[2:], lambda i: sel + (0, 0), pipeline_mode=pl.Buffered(1))]
    args += [g.reshape(1, D_MODEL), w_in, w_out]
    if final:
        in_specs.append(_resident((1, D_MODEL)))
        args.append(final_g.reshape(1, D_MODEL))
    return pl.pallas_call(
        functools.partial(_ffn_body, final=final, pre=pre is not None),
        grid=(t // tm,),
        in_specs=in_specs,
        out_specs=pl.BlockSpec((tm, D_MODEL), lambda i: (i, 0)),
        out_shape=jax.ShapeDtypeStruct((t, D_MODEL), F32),
        scratch_shapes=[pltpu.VMEM((tm, D_FF), BF16)],
        compiler_params=_params(("parallel",)),
        name="ffn",
    )(*args)


PROJ_CHUNK = 1024


def _proj_body(x_ref, g_ref, w_ref, *o_refs, widths):
    hn = _rms(x_ref[...], g_ref[...]).astype(BF16)
    off = 0
    for o_ref, width in zip(o_refs, widths):
        for c in range(0, width, PROJ_CHUNK):
            n = min(PROJ_CHUNK, width - c)
            o_ref[:, c:c + n] = _dot(hn, w_ref[:, off + c:off + c + n]).astype(o_ref.dtype)
        off += width


def _proj(x, g, w, widths, dtypes, *, tm):
    t = x.shape[0]
    return pl.pallas_call(
        functools.partial(_proj_body, widths=tuple(widths)),
        grid=(t // tm,),
        in_specs=[pl.BlockSpec((tm, D_MODEL), lambda i: (i, 0)), _resident((1, D_MODEL)), _resident(w.shape)],
        out_specs=[pl.BlockSpec((tm, n), lambda i: (i, 0)) for n in widths],
        out_shape=[jax.ShapeDtypeStruct((t, n), dt) for n, dt in zip(widths, dtypes)],
        compiler_params=_params(("parallel",)),
        name="proj",
    )(x, g.reshape(1, D_MODEL), w)


def _shift_matrix(c):
    out_row = jnp.arange(3 * c)
    k = out_row // c + 1
    src = out_row % c - k
    return (src[:, None] == jnp.arange(c)[None, :]).astype(BF16)


def _conv_chunk(tail, src, shift_ref, w_ref, bias_ref, c):
    x = src.astype(F32)
    sh = _dot(shift_ref[...], src)
    w0, w1, w2, w3 = (w_ref[j:j + 1, :] for j in range(CONV_W))
    acc = w3 * x + w2 * sh[:c] + w1 * sh[c:2 * c] + w0 * sh[2 * c:]
    if bias_ref is not None:
        acc = acc + bias_ref[...]
    t3, t2, t1 = tail[5:6, :], tail[6:7, :], tail[7:8, :]
    row = lax.broadcasted_iota(jnp.int32, (SUBLANES, x.shape[1]), 0)
    head = jnp.where(row == 0, w0 * t3 + w1 * t2 + w2 * t1,
                     jnp.where(row == 1, w0 * t2 + w1 * t1, jnp.where(row == 2, w0 * t1, 0.0)))
    tail[pl.ds(5, 3), :] = x[c - 3:c, :]
    return jnp.concatenate([acc[:SUBLANES] + head, acc[SUBLANES:]], axis=0)


def _conv_state(tail):
    return tail[pl.ds(5, 3), :]


def _gla_body(q_in, k_in, v_in, gg_ref, sm_ref, wg2_ref, bg_ref, nw_ref, o_ref, s_out, st_ref, b_ref,
              q_ref, k_ref, v_ref, *, lg, nc):
    c = pl.program_id(1)

    @pl.when(c == 0)
    def _():
        st_ref[...] = jnp.zeros(st_ref.shape, F32)

    pre = _dot(sm_ref[0].astype(BF16), wg2_ref[...]) + bg_ref[...]
    log_a = -_softplus(-pre) * (1.0 / GLA_GATE_NORM)
    lower = _lower_mask(lg)
    b_all = _dot01_exact(jnp.where(lower, 1.0, 0.0).astype(BF16), log_a)
    b_ref[...] = b_all
    worst = jnp.max(-b_all[lg - 1:lg, :])

    @pl.when(worst <= GLA_FAST_MAX)
    def _():
        q = q_in[0].astype(F32) * (GLA_DK ** -0.5)
        k = k_in[0].astype(F32)
        vb = v_in[0]
        blast = b_all[lg - 1:lg, :]
        qe = (q * jnp.exp(b_all)).astype(BF16)
        ke = (k * jnp.exp(-b_all)).astype(BF16)
        kd = (k * jnp.exp(blast - b_all)).astype(BF16)
        elast = jnp.exp(blast)
        gg = gg_ref[0].astype(F32)
        for h in range(GLA_HEADS):
            dk = slice(h * GLA_DK, (h + 1) * GLA_DK)
            dv = slice(h * GLA_DV, (h + 1) * GLA_DV)
            att = jnp.where(lower, _dot_nt(qe[:, dk], ke[:, dk]), 0.0).astype(BF16)
            st = st_ref[h]
            oh = _dot(att, vb[:, dv]) + _dot_nt(qe[:, dk], st.astype(BF16))
            st_ref[h] = st * elast[:, dk] + _dot_tn(vb[:, dv], kd[:, dk])
            o_ref[0, :, dv] = (_rms(oh, nw_ref[...]) * _silu(gg[:, dv])).astype(BF16)

    @pl.when(worst > GLA_FAST_MAX)
    def _():
        q_ref[...] = q_in[0].astype(F32) * (GLA_DK ** -0.5)
        k_ref[...] = k_in[0].astype(F32)
        v_ref[...] = v_in[0].astype(F32)
        rowi = lax.broadcasted_iota(jnp.int32, (GLA_SUB, 1), 0)

        def block(i, carry):
            base = pl.multiple_of(i * GLA_SUB, GLA_SUB)
            q = q_ref[pl.ds(base, GLA_SUB), :]
            k = k_ref[pl.ds(base, GLA_SUB), :]
            v = v_ref[pl.ds(base, GLA_SUB), :]
            b = b_ref[pl.ds(base, GLA_SUB), :]
            o = [jnp.zeros((GLA_SUB, GLA_DV), F32) for _ in range(GLA_HEADS)]
            for s in range(GLA_SUB):
                ks = k_ref[pl.ds(base + s, 1), :]
                bs = b_ref[pl.ds(base + s, 1), :]
                vs = v_ref[pl.ds(base + s, 1), :]
                p = q * ks * jnp.exp(b - bs)
                for h in range(GLA_HEADS):
                    att = jnp.sum(p[:, h * GLA_DK:(h + 1) * GLA_DK], axis=-1, keepdims=True)
                    att = jnp.where(rowi >= s, att, 0.0)
                    o[h] = o[h] + att * vs[:, h * GLA_DV:(h + 1) * GLA_DV]
            bprev = jnp.where(i > 0, b_ref[pl.ds(jnp.maximum(base - 1, 0), 1), :], 0.0)
            blast = b_ref[pl.ds(base + GLA_SUB - 1, 1), :]
            qe = (q * jnp.exp(b - bprev)).astype(BF16)
            kd = (k * jnp.exp(blast - b)).astype(BF16)
            elast = jnp.exp(blast - bprev)
            vb = v.astype(BF16)
            gg = gg_ref[0, pl.ds(base, GLA_SUB), :].astype(F32)
            for h in range(GLA_HEADS):
                dk = slice(h * GLA_DK, (h + 1) * GLA_DK)
                dv = slice(h * GLA_DV, (h + 1) * GLA_DV)
                st = st_ref[h]
                oh = o[h] + _dot_nt(qe[:, dk], st.astype(BF16))
                st_ref[h] = st * elast[:, dk] + _dot_tn(vb[:, dv], kd[:, dk])
                o_ref[0, pl.ds(base, GLA_SUB), dv] = (_rms(oh, nw_ref[...]) * _silu(gg[:, dv])).astype(BF16)
            return carry

        lax.fori_loop(0, lg // GLA_SUB, block, 0)

    @pl.when(c == nc - 1)
    def _():
        for h in range(GLA_HEADS):
            s_out[0, h] = st_ref[h].T


def _gla_prompt(xa, sm, wg2, bgate, normw, *, lg):
    nb, l = xa.shape[0], xa.shape[1]
    nc = l // lg

    def seg(width, off):
        return pl.BlockSpec((1, lg, width), lambda b, c: (b, c, off // width))

    def full(shape):
        return pl.BlockSpec(shape, lambda b, c: (0,) * len(shape))

    return pl.pallas_call(
        functools.partial(_gla_body, lg=lg, nc=nc),
        grid=(nb, nc),
        in_specs=[seg(512, GLA_GQ), seg(512, GLA_GK), seg(1024, GLA_GV), seg(1024, GLA_GG),
                  pl.BlockSpec((1, lg, LANES), lambda b, c: (b, c, 0)),
                  full((LANES, 512)), full((1, 512)), full((1, GLA_DV))],
        out_specs=[pl.BlockSpec((1, lg, 1024), lambda b, c: (b, c, 0)),
                   pl.BlockSpec((1, GLA_HEADS, GLA_DK, GLA_DV), lambda b, c: (b, 0, 0, 0))],
        out_shape=[jax.ShapeDtypeStruct((nb, l, 1024), BF16),
                   jax.ShapeDtypeStruct((nb, GLA_HEADS, GLA_DK, GLA_DV), F32)],
        scratch_shapes=[pltpu.VMEM((GLA_HEADS, GLA_DV, GLA_DK), F32), pltpu.VMEM((lg, 512), F32),
                        pltpu.VMEM((lg, 512), F32), pltpu.VMEM((lg, 512), F32), pltpu.VMEM((lg, 1024), F32)],
        compiler_params=_params(("parallel", "arbitrary")),
        name="gla_prompt",
    )(xa, xa, xa, xa, sm, wg2, bgate, normw)


def _ssd_body(z_ref, x_ref, b_ref, c_ref, sm_ref, shift_ref, cwx, cbx, cwb, cbb, cwc, cbc, dtb_ref, alog_ref, e_ref, et_ref,
              dvec_ref, nw_ref, y_ref, s_out, cx_out, cb_out, cc_out, s_ref, bufx, bufb, bufc, y_scr, *, cs, nc):
    c = pl.program_id(1)

    @pl.when(c == 0)
    def _():
        s_ref[...] = jnp.zeros(s_ref.shape, F32)
        for buf in (bufx, bufb, bufc):
            buf[...] = jnp.zeros(buf.shape, F32)

    xs = _silu(_conv_chunk(bufx, x_ref[0], shift_ref, cwx, cbx, cs))
    bm = _silu(_conv_chunk(bufb, b_ref[0], shift_ref, cwb, cbb, cs)).astype(BF16)
    cm = _silu(_conv_chunk(bufc, c_ref[0], shift_ref, cwc, cbc, cs)).astype(BF16)

    dt = _softplus(sm_ref[0] + dtb_ref[...])
    da = dt * (-jnp.exp(alog_ref[...]))
    lower = _lower_mask(cs)
    tri = jnp.where(lower, 1.0, 0.0).astype(BF16)
    cum = _dot01_exact(tri, da)
    cum_t = cum.T
    dt_t = dt.T
    last = cum[cs - 1:cs, :]
    ecum_x = _dot_x01(jnp.exp(cum), e_ref[...])
    w_x = _dot_x01(jnp.exp(last - cum) * dt, e_ref[...])
    elast = jnp.broadcast_to(jnp.exp(last), (SUBLANES, LANES))
    eh, el = _split2(elast)
    ecol = _dot_nt(et_ref[...], eh) + _dot_nt(et_ref[...], el)

    xs_bf = xs.astype(BF16)
    xw = (xs * w_x).astype(BF16)
    z = z_ref[0].astype(F32)
    lane = lax.broadcasted_iota(jnp.int32, (cs, LANES), 1)
    for g in range(SSD_GROUPS):
        gn = slice(g * SSD_D_STATE, (g + 1) * SSD_D_STATE)
        gw = slice(g * SSD_GROUP_W, (g + 1) * SSD_GROUP_W)
        cg, bg = cm[:, gn], bm[:, gn]
        cb = _dot_nt(cg, bg)
        sg = s_ref[pl.ds(g * SSD_GROUP_W, SSD_GROUP_W), :]
        y_inter = _dot_nt(cg, sg.astype(BF16)) * ecum_x[:, gw]
        ss = jnp.zeros((cs, 1), F32)
        for p in range(SSD_HPG // 2):
            pair = slice(g * SSD_GROUP_W + p * LANES, g * SSD_GROUP_W + (p + 1) * LANES)
            xp = xs_bf[:, pair]
            ys = []
            for h in (g * SSD_HPG + 2 * p, g * SSD_HPG + 2 * p + 1):
                seg = jnp.exp(jnp.where(lower, cum[:, h:h + 1] - cum_t[h:h + 1, :], NEG_BIG))
                m = cb * seg * dt_t[h:h + 1, :]
                ys.append(_dot(m.astype(BF16), xp))
            yp = jnp.where(lane < SSD_HEAD_DIM, ys[0], ys[1])
            yp = yp + y_inter[:, p * LANES:(p + 1) * LANES] + dvec_ref[:, pair] * xs[:, pair]
            yp = yp * _silu(z[:, pair])
            ss = ss + jnp.sum(yp * yp, axis=-1, keepdims=True)
            y_scr[:, pair] = yp
        inv = lax.rsqrt(ss * (1.0 / SSD_GROUP_W) + NORM_EPS)
        y_ref[0, :, gw] = (y_scr[:, gw] * inv * nw_ref[:, gw]).astype(BF16)
        s_ref[pl.ds(g * SSD_GROUP_W, SSD_GROUP_W), :] = (
            ecol[g * SSD_GROUP_W:(g + 1) * SSD_GROUP_W, 0:1] * sg + _dot_tn(xw[:, gw], bg))

    @pl.when(c == nc - 1)
    def _():
        s_out[0] = s_ref[...]
        cx_out[0] = _conv_state(bufx)
        cb_out[0] = _conv_state(bufb)
        cc_out[0] = _conv_state(bufc)


def _ssd_prompt(xb, sm, cw, cb, dtb, alog, emat, emat_t, dvec, normw, *, cs):
    nb, l = xb.shape[0], xb.shape[1]
    nc = l // cs

    def seg(width, off):
        return pl.BlockSpec((1, cs, width), lambda b, c: (b, c, off // width))

    def full(shape):
        return pl.BlockSpec(shape, lambda b, c: (0,) * len(shape))

    def tail(width):
        return pl.BlockSpec((1, CONV_W - 1, width), lambda b, c: (b, 0, 0))

    cwx, cwb, cwc = cw[:, :2048], cw[:, 2048:2560], cw[:, 2560:]
    cbx, cbb, cbc = cb[:, :2048], cb[:, 2048:2560], cb[:, 2560:]
    return pl.pallas_call(
        functools.partial(_ssd_body, cs=cs, nc=nc),
        grid=(nb, nc),
        in_specs=[seg(2048, SSD_Z), seg(2048, SSD_X), seg(512, SSD_B), seg(512, SSD_C),
                  pl.BlockSpec((1, cs, LANES), lambda b, c: (b, c, 0)), full((3 * cs, cs)),
                  full((CONV_W, 2048)), full((1, 2048)), full((CONV_W, 512)), full((1, 512)),
                  full((CONV_W, 512)), full((1, 512)), full((1, LANES)), full((1, LANES)),
                  full((LANES, 2048)), full((2048, LANES)), full((1, 2048)), full((1, 2048))],
        out_specs=[pl.BlockSpec((1, cs, 2048), lambda b, c: (b, c, 0)),
                   pl.BlockSpec((1, 2048, SSD_D_STATE), lambda b, c: (b, 0, 0)),
                   tail(2048), tail(512), tail(512)],
        out_shape=[jax.ShapeDtypeStruct((nb, l, 2048), BF16),
                   jax.ShapeDtypeStruct((nb, 2048, SSD_D_STATE), F32),
                   jax.ShapeDtypeStruct((nb, CONV_W - 1, 2048), F32),
                   jax.ShapeDtypeStruct((nb, CONV_W - 1, 512), F32),
                   jax.ShapeDtypeStruct((nb, CONV_W - 1, 512), F32)],
        scratch_shapes=[pltpu.VMEM((2048, SSD_D_STATE), F32),
                        pltpu.VMEM((SUBLANES, 2048), F32),
                        pltpu.VMEM((SUBLANES, 512), F32),
                        pltpu.VMEM((SUBLANES, 512), F32),
                        pltpu.VMEM((cs, 2048), F32)],
        compiler_params=_params(("parallel", "arbitrary")),
        name="ssd_prompt",
    )(xb, xb, xb, xb, sm, _shift_matrix(cs), cwx, cbx, cwb, cbb, cwc, cbc, dtb, alog, emat, emat_t, dvec, normw)


def _lru_gates(xc, wa_ref, ba_ref, wx_ref, bx_ref, lam_ref):
    xb = xc.astype(BF16)
    rl, il = [], []
    for n in range(LRU_BLOCKS):
        blk = xb[:, n * LRU_BLOCK_W:(n + 1) * LRU_BLOCK_W]
        rl.append(_dot(blk, wa_ref[n]))
        il.append(_dot(blk, wx_ref[n]))
    r_logit = jnp.concatenate(rl, axis=-1) + ba_ref[...]
    i_logit = jnp.concatenate(il, axis=-1) + bx_ref[...]
    log_a = -LRU_C * _sigmoid(r_logit) * _softplus(-lam_ref[...])
    a = jnp.exp(log_a)
    b = jnp.sqrt(1.0 - jnp.exp(2.0 * log_a)) * (_sigmoid(i_logit) * xc)
    return a, b


def _gelu_tanh(x):
    return 0.5 * x * (1.0 + jnp.tanh(math.sqrt(2.0 / math.pi) * (x + 0.044715 * (x * x * x))))


def _lru_body(rg_ref, rx_ref, shift_ref, cw, cbias, wa_ref, ba_ref, wx_ref, bx_ref, lam_ref, o_ref, h_out, c_out,
              h_ref, buf, a_scr, b_scr, o_scr, *, cs, nc):
    c = pl.program_id(1)

    @pl.when(c == 0)
    def _():
        h_ref[...] = jnp.zeros(h_ref.shape, F32)
        buf[...] = jnp.zeros(buf.shape, F32)

    xc = _conv_chunk(buf, rx_ref[0], shift_ref, cw, cbias, cs)
    a, b = _lru_gates(xc, wa_ref, ba_ref, wx_ref, bx_ref, lam_ref)
    tiles = (cs // SUBLANES, SUBLANES, LRU_WIDTH)
    a, b = a.reshape(tiles), b.reshape(tiles)
    rowm = lax.broadcasted_iota(jnp.int32, tiles, 1)
    for sh in (1, 2, 4):
        keep = rowm >= sh
        a_prev = jnp.where(keep, pltpu.roll(a, sh, 1), 1.0)
        b_prev = jnp.where(keep, pltpu.roll(b, sh, 1), 0.0)
        b = a * b_prev + b
        a = a * a_prev
    a_scr[...] = a.reshape(cs, LRU_WIDTH)
    b_scr[...] = b.reshape(cs, LRU_WIDTH)
    h = h_ref[...]
    for t in range(cs // SUBLANES):
        rows = pl.ds(t * SUBLANES, SUBLANES)
        ht = a_scr[rows, :] * h + b_scr[rows, :]
        o_scr[rows, :] = ht
        h = jnp.broadcast_to(ht[SUBLANES - 1:SUBLANES, :], (SUBLANES, LRU_WIDTH))
    h_ref[...] = h
    o_ref[0] = (_gelu_tanh(rg_ref[0].astype(F32)) * o_scr[...]).astype(BF16)

    @pl.when(c == nc - 1)
    def _():
        h_out[0] = h[0:1, :]
        c_out[0] = _conv_state(buf)


def _lru_prompt(xl, cw, cbias, wa, ba, wx, bx, lam, *, cs):
    nb, l = xl.shape[0], xl.shape[1]
    nc = l // cs

    def full(shape):
        return pl.BlockSpec(shape, lambda b, c: (0,) * len(shape))

    return pl.pallas_call(
        functools.partial(_lru_body, cs=cs, nc=nc),
        grid=(nb, nc),
        in_specs=[pl.BlockSpec((1, cs, LRU_WIDTH), lambda b, c: (b, c, LRU_RG // LRU_WIDTH)),
                  pl.BlockSpec((1, cs, LRU_WIDTH), lambda b, c: (b, c, LRU_RX // LRU_WIDTH)),
                  full((3 * cs, cs)), full((CONV_W, LRU_WIDTH)), full((1, LRU_WIDTH)),
                  full((LRU_BLOCKS, LRU_BLOCK_W, LRU_BLOCK_W)), full((1, LRU_WIDTH)),
                  full((LRU_BLOCKS, LRU_BLOCK_W, LRU_BLOCK_W)), full((1, LRU_WIDTH)), full((1, LRU_WIDTH))],
        out_specs=[pl.BlockSpec((1, cs, LRU_WIDTH), lambda b, c: (b, c, 0)),
                   pl.BlockSpec((1, 1, LRU_WIDTH), lambda b, c: (b, 0, 0)),
                   pl.BlockSpec((1, CONV_W - 1, LRU_WIDTH), lambda b, c: (b, 0, 0))],
        out_shape=[jax.ShapeDtypeStruct((nb, l, LRU_WIDTH), BF16),
                   jax.ShapeDtypeStruct((nb, 1, LRU_WIDTH), F32),
                   jax.ShapeDtypeStruct((nb, CONV_W - 1, LRU_WIDTH), F32)],
        scratch_shapes=[pltpu.VMEM((SUBLANES, LRU_WIDTH), F32),
                        pltpu.VMEM((SUBLANES, LRU_WIDTH), F32),
                        pltpu.VMEM((cs, LRU_WIDTH), F32),
                        pltpu.VMEM((cs, LRU_WIDTH), F32),
                        pltpu.VMEM((cs, LRU_WIDTH), F32)],
        compiler_params=_params(("parallel", "arbitrary")),
        name="lru_prompt",
    )(xl, xl, _shift_matrix(cs), cw, cbias, wa, ba, wx, bx, lam)


def _l2norm(x):
    return x * lax.rsqrt(jnp.sum(x * x, axis=-1, keepdims=True) + NORM_EPS)


def _cat3(a, b, c, axis):
    return jnp.concatenate([a, b, c], axis=axis)


def _gdn_body(q_ref, k_ref, v_ref, z_ref, sm_ref, shift_ref, cwq, cwk, cwv, alog_ref, dtb_ref, nw_ref,
              o_ref, s_out, cq_out, ck_out, cv_out, s_ref, bufq, bufk, bufv,
              ph_scr, pl_scr, t_scr, qb_scr, kd_scr, aqk_scr, rh_scr, rl_scr, *, cs, nc):
    c = pl.program_id(1)

    @pl.when(c == 0)
    def _():
        s_ref[...] = jnp.zeros(s_ref.shape, F32)
        for buf in (bufq, bufk, bufv):
            buf[...] = jnp.zeros(buf.shape, F32)

    q = _silu(_conv_chunk(bufq, q_ref[0], shift_ref, cwq, None, cs))
    k = _silu(_conv_chunk(bufk, k_ref[0], shift_ref, cwk, None, cs))
    v = _silu(_conv_chunk(bufv, v_ref[0], shift_ref, cwv, None, cs))
    sm = sm_ref[0]
    g_all = -jnp.exp(alog_ref[...]) * _softplus(sm + dtb_ref[...])
    beta_all = _sigmoid(sm)
    lower = _lower_mask(cs)
    strict = _lower_mask(cs, strict=True)
    diag = lower & jnp.logical_not(strict)
    tri = jnp.where(lower, 1.0, 0.0).astype(BF16)
    gc = _dot01_exact(tri, g_all)
    gc_t = gc.T
    egc_all = jnp.exp(gc)

    for h in range(GDN_HEADS):
        hd = slice(h * GDN_DK, (h + 1) * GDN_DK)
        qh = _l2norm(q[:, hd]) * (GDN_DK ** -0.5)
        kh = _l2norm(k[:, hd])
        beta = beta_all[:, GDN_HEADS + h:GDN_HEADS + h + 1]
        gcol = gc[:, h:h + 1]
        dec = jnp.exp(jnp.where(lower, gcol - gc_t[h:h + 1, :], NEG_BIG))
        qb, kb = qh.astype(BF16), kh.astype(BF16)
        both = _dot_nt(jnp.concatenate([kb, qb], axis=0), kb)
        p = -(beta * jnp.where(strict, dec, 0.0) * both[:cs])
        hi, lo = _split2(p)
        ph_scr[h] = hi
        pl_scr[h] = lo
        t_scr[h] = jnp.where(diag, 1.0, p)
        qb_scr[h] = qb
        aqk_scr[h] = (dec * both[cs:]).astype(BF16)
        glast = gc[cs - 1:cs, h:h + 1]
        kd_scr[h] = (kh * jnp.exp(glast - gcol)).astype(BF16)
        rhs = jnp.concatenate([v[:, hd] * beta, kh * (beta * egc_all[:, h:h + 1])], axis=-1)
        hi, lo = _split2(rhs)
        rh_scr[h] = hi
        rl_scr[h] = lo

    levels = int(math.log2(cs))
    for lvl in range(levels):
        for h in range(GDN_HEADS):
            hi, lo = ph_scr[h], pl_scr[h]
            lhs = _cat3(hi, hi, lo, 1)
            if lvl == 0:
                p2 = _dot(lhs, _cat3(hi, lo, hi, 0))
                t_new = None
            else:
                t = t_scr[h]
                th, tl = _split2(t)
                if lvl < levels - 1:
                    rhs = _cat3(jnp.concatenate([hi, th], axis=1), jnp.concatenate([lo, tl], axis=1),
                                jnp.concatenate([hi, th], axis=1), 0)
                    both = _dot(lhs, rhs)
                    p2, t_new = both[:, :cs], t + both[:, cs:]
                else:
                    p2, t_new = None, t + _dot(lhs, _cat3(th, tl, th, 0))
            if p2 is not None:
                hi2, lo2 = _split2(p2)
                ph_scr[h] = hi2
                pl_scr[h] = lo2
            if t_new is not None:
                t_scr[h] = t_new

    z = z_ref[0].astype(F32)
    us, wqs = [], []
    for h in range(GDN_HEADS):
        th, tl = _split2(t_scr[h])
        rh, rl = rh_scr[h], rl_scr[h]
        sol = _dot(_cat3(th, th, tl, 1), _cat3(rh, rl, rh, 0))
        us.append(sol[:, :GDN_DV])
        wqs.append(jnp.concatenate([sol[:, GDN_DV:].astype(BF16), qb_scr[h]], axis=0))
    dbs, qss = [], []
    for h in range(GDN_HEADS):
        ws_qs = _dot(wqs[h], s_ref[h].astype(BF16))
        dbs.append((us[h] - ws_qs[:cs]).astype(BF16))
        qss.append(ws_qs[cs:])
    for h in range(GDN_HEADS):
        hd = slice(h * GDN_DK, (h + 1) * GDN_DK)
        o = egc_all[:, h:h + 1] * qss[h] + _dot(aqk_scr[h], dbs[h])
        s_ref[h] = jnp.exp(gc[cs - 1:cs, h:h + 1]) * s_ref[h] + _dot_tn(kd_scr[h], dbs[h])
        o_ref[0, :, hd] = (_rms(o, nw_ref[...]) * _silu(z[:, hd])).astype(BF16)

    @pl.when(c == nc - 1)
    def _():
        s_out[0] = s_ref[...]
        cq_out[0] = _conv_state(bufq)
        ck_out[0] = _conv_state(bufk)
        cv_out[0] = _conv_state(bufv)


def _gdn_prompt(xg, sm, cw, alog, dtb, normw, *, cs):
    nb, l = xg.shape[0], xg.shape[1]
    nc = l // cs

    def seg(width, off):
        return pl.BlockSpec((1, cs, width), lambda b, c: (b, c, off // width))

    def full(shape):
        return pl.BlockSpec(shape, lambda b, c: (0,) * len(shape))

    def tail(width):
        return pl.BlockSpec((1, CONV_W - 1, width), lambda b, c: (b, 0, 0))

    hm = (GDN_HEADS, cs, cs)
    return pl.pallas_call(
        functools.partial(_gdn_body, cs=cs, nc=nc),
        grid=(nb, nc),
        in_specs=[seg(1024, GDN_Q), seg(1024, GDN_K), seg(1024, GDN_V), seg(1024, GDN_Z),
                  pl.BlockSpec((1, cs, LANES), lambda b, c: (b, c, 0)), full((3 * cs, cs)),
                  full((CONV_W, 1024)), full((CONV_W, 1024)), full((CONV_W, 1024)),
                  full((1, LANES)), full((1, LANES)), full((1, GDN_DV))],
        out_specs=[pl.BlockSpec((1, cs, 1024), lambda b, c: (b, c, 0)),
                   pl.BlockSpec((1, GDN_HEADS, GDN_DK, GDN_DV), lambda b, c: (b, 0, 0, 0)),
                   tail(1024), tail(1024), tail(1024)],
        out_shape=[jax.ShapeDtypeStruct((nb, l, 1024), BF16),
                   jax.ShapeDtypeStruct((nb, GDN_HEADS, GDN_DK, GDN_DV), F32)]
        + [jax.ShapeDtypeStruct((nb, CONV_W - 1, 1024), F32)] * 3,
        scratch_shapes=[pltpu.VMEM((GDN_HEADS, GDN_DK, GDN_DV), F32)]
        + [pltpu.VMEM((SUBLANES, 1024), F32)] * 3
        + [pltpu.VMEM(hm, BF16), pltpu.VMEM(hm, BF16), pltpu.VMEM(hm, F32),
           pltpu.VMEM((GDN_HEADS, cs, GDN_DK), BF16), pltpu.VMEM((GDN_HEADS, cs, GDN_DK), BF16),
           pltpu.VMEM(hm, BF16), pltpu.VMEM((GDN_HEADS, cs, 2 * GDN_DV), BF16),
           pltpu.VMEM((GDN_HEADS, cs, 2 * GDN_DV), BF16)],
        compiler_params=_params(("parallel", "arbitrary")),
        name="gdn_prompt",
    )(xg, xg, xg, xg, sm, _shift_matrix(cs), cw[:, :1024], cw[:, 1024:2048], cw[:, 2048:], alog, dtb, normw)


def _pad_lanes(v, width, offset=0):
    return jnp.zeros((1, width), F32).at[0, offset:offset + v.shape[0]].set(v.astype(F32))


def _prep(ffn_w_in, ffn_w_out, l0_w_in, l0_w_out, gla_w_gate2, gla_b_gate, gla_norm, ssd_conv_w, ssd_conv_b,
          ssd_dt_bias, ssd_a_log, ssd_d, ssd_norm, l1_w_in, l1_w_out, lru_conv_w, lru_conv_b, lru_w_a, lru_b_a,
          lru_w_x, lru_b_x, lru_lambda, gdn_conv_w, gdn_a_log, gdn_dt_bias, gdn_norm):
    w = {}
    w["ffn_in"] = ffn_w_in.astype(BF16)
    w["ffn_out"] = ffn_w_out.astype(BF16)
    c = l0_w_in
    small0 = jnp.concatenate([c[:, 8208:8240], c[:, 3072:3088], jnp.zeros((D_MODEL, LANES - 48), F32)], axis=1)
    w["l0_in"] = jnp.concatenate(
        [c[:, 1024:2048], c[:, 2048:3072], c[:, 0:512], c[:, 512:1024],
         c[:, 3088:5136], c[:, 5136:7184], c[:, 7184:7696], c[:, 7696:8208],
         small0], axis=1).astype(BF16)
    w["l0_out"] = l0_w_out.astype(BF16)
    w["gla_wg2"] = jnp.zeros((LANES, 512), F32).at[32:48].set(gla_w_gate2).astype(BF16)
    w["gla_bg"] = gla_b_gate.reshape(1, 512)
    w["gla_norm"] = gla_norm.reshape(1, GLA_DV)
    w["ssd_cw"] = ssd_conv_w
    w["ssd_cb"] = ssd_conv_b.reshape(1, -1)
    w["ssd_dtb"] = _pad_lanes(ssd_dt_bias, LANES)
    w["ssd_alog"] = _pad_lanes(ssd_a_log, LANES)
    head_of_lane = jnp.arange(SSD_D_INNER) // SSD_HEAD_DIM
    emat = (jnp.arange(LANES)[:, None] == head_of_lane[None, :])
    w["ssd_e"] = emat.astype(BF16)
    w["ssd_et"] = emat.T.astype(BF16)
    w["ssd_dvec"] = jnp.repeat(ssd_d, SSD_HEAD_DIM).reshape(1, SSD_D_INNER)
    w["ssd_norm"] = ssd_norm.reshape(1, SSD_D_INNER)
    c = l1_w_in
    small1 = jnp.concatenate([c[:, 6656:6672], jnp.zeros((D_MODEL, LANES - 16), F32)], axis=1)
    w["l1_in"] = jnp.concatenate(
        [c[:, 2560:3584], c[:, 3584:4608], c[:, 4608:5632], c[:, 5632:6656],
         c[:, 0:1280], c[:, 1280:2560],
         small1], axis=1).astype(BF16)
    w["l1_out"] = l1_w_out.astype(BF16)
    w["lru_cw"] = lru_conv_w
    w["lru_cb"] = lru_conv_b.reshape(1, -1)
    w["lru_wa"] = lru_w_a.astype(BF16)
    w["lru_ba"] = lru_b_a.reshape(1, -1)
    w["lru_wx"] = lru_w_x.astype(BF16)
    w["lru_bx"] = lru_b_x.reshape(1, -1)
    w["lru_lam"] = lru_lambda.reshape(1, -1)
    w["gdn_cw"] = gdn_conv_w
    w["gdn_alog"] = _pad_lanes(gdn_a_log, LANES)
    w["gdn_dtb"] = _pad_lanes(gdn_dt_bias, LANES)
    w["gdn_norm"] = gdn_norm.reshape(1, GDN_DV)
    return w


L0_WIDTHS = (GLA_W, SSD_W, LANES)
L1_WIDTHS = (GDN_W, LRU_W2, LANES)


def _trunk_prompt(x, norms, final_norm, w, *, tm, cs):
    nb, l = x.shape[0], x.shape[1]
    t = nb * l
    h = x.reshape(t, D_MODEL)
    h = _ffn(h, norms[0, 0], w["ffn_in"], w["ffn_out"], (0, 0), tm=tm)
    xa, xb, sm = _proj(h, norms[0, 1], w["l0_in"], L0_WIDTHS, (BF16, BF16, F32), tm=tm)
    sm = sm.reshape(nb, l, LANES)
    o_a, s_gla = _gla_prompt(xa.reshape(nb, l, GLA_W), sm, w["gla_wg2"], w["gla_bg"], w["gla_norm"], lg=cs)
    y, s_ssd, cx, cb, cc = _ssd_prompt(xb.reshape(nb, l, SSD_W), sm, w["ssd_cw"], w["ssd_cb"], w["ssd_dtb"],
                                       w["ssd_alog"], w["ssd_e"], w["ssd_et"], w["ssd_dvec"], w["ssd_norm"], cs=cs)
    h = _ffn(h, norms[0, 2], w["ffn_in"], w["ffn_out"], (0, 1),
             pre=(o_a.reshape(t, 1024), y.reshape(t, 2048), w["l0_out"]), tm=tm)
    h = _ffn(h, norms[1, 0], w["ffn_in"], w["ffn_out"], (1, 0), tm=tm)
    xg, xl, sm = _proj(h, norms[1, 1], w["l1_in"], L1_WIDTHS, (BF16, BF16, F32), tm=tm)
    sm = sm.reshape(nb, l, LANES)
    o_c, s_lru, s_lru_conv = _lru_prompt(xl.reshape(nb, l, LRU_W2), w["lru_cw"], w["lru_cb"], w["lru_wa"],
                                         w["lru_ba"], w["lru_wx"], w["lru_bx"], w["lru_lam"], cs=cs)
    o_d, s_gdn, cq, ck, cv = _gdn_prompt(xg.reshape(nb, l, GDN_W), sm, w["gdn_cw"], w["gdn_alog"], w["gdn_dtb"],
                                         w["gdn_norm"], cs=cs)
    h = _ffn(h, norms[1, 2], w["ffn_in"], w["ffn_out"], (1, 1), final_norm,
             pre=(o_c.reshape(t, LRU_WIDTH), o_d.reshape(t, 1024), w["l1_out"]), tm=tm)
    return (h.reshape(nb, l, D_MODEL), s_gla, s_ssd.reshape(nb, SSD_HEADS, SSD_HEAD_DIM, SSD_D_STATE),
            jnp.concatenate([cx, cb, cc], axis=-1), s_lru.reshape(nb, LRU_WIDTH), s_lru_conv, s_gdn,
            jnp.concatenate([cq, ck, cv], axis=-1))


def _conv_step(cst_ref, cst_out, cur, w_ref, bias_ref, lo, hi):
    acc = w_ref[CONV_W - 1:CONV_W, lo:hi] * cur
    for j in range(CONV_W - 1):
        acc = acc + w_ref[j:j + 1, lo:hi] * cst_ref[:, j, lo:hi]
    if bias_ref is not None:
        acc = acc + bias_ref[:, lo:hi]
    for j in range(CONV_W - 2):
        cst_out[:, j, lo:hi] = cst_ref[:, j + 1, lo:hi]
    cst_out[:, CONV_W - 2, lo:hi] = cur
    return acc


def _gla_dec_body(q_ref, k_ref, v_ref, gg_ref, sm_ref, wg2_ref, bg_ref, nw_ref, s_in, o_ref, s_out, o_scr, *, nb):
    pre = _dot(sm_ref[...].astype(BF16), wg2_ref[...]) + bg_ref[...]
    a = jnp.exp(-_softplus(-pre) * (1.0 / GLA_GATE_NORM))
    q = q_ref[...] * (GLA_DK ** -0.5)
    v = v_ref[...]
    a_t, q_t, k_t = a.T, q.T, k_ref[...].T
    for b in range(nb):
        for h in range(GLA_HEADS):
            dk = slice(h * GLA_DK, (h + 1) * GLA_DK)
            dv = slice(h * GLA_DV, (h + 1) * GLA_DV)
            sn = a_t[dk, b:b + 1] * s_in[b, h] + k_t[dk, b:b + 1] * v[b:b + 1, dv]
            s_out[b, h] = sn
            o_scr[b:b + 1, dv] = jnp.sum(q_t[dk, b:b + 1] * sn, axis=0, keepdims=True)
    o = o_scr[...]
    gg = gg_ref[...]
    for h in range(GLA_HEADS):
        dv = slice(h * GLA_DV, (h + 1) * GLA_DV)
        o_ref[:, dv] = _rms(o[:, dv], nw_ref[...]) * _silu(gg[:, dv])


def _gla_step(xa, sm, state, wg2, bgate, normw, *, nb):
    n = xa.shape[0]

    def seg(width, off):
        return pl.BlockSpec((nb, width), lambda i: (i, off // width))

    def full(shape):
        return pl.BlockSpec(shape, lambda i: (0,) * len(shape))

    st = pl.BlockSpec((nb, GLA_HEADS, GLA_DK, GLA_DV), lambda i: (i, 0, 0, 0))
    return pl.pallas_call(
        functools.partial(_gla_dec_body, nb=nb),
        grid=(n // nb,),
        in_specs=[seg(512, GLA_GQ), seg(512, GLA_GK), seg(1024, GLA_GV), seg(1024, GLA_GG), seg(LANES, 0),
                  full((LANES, 512)), full((1, 512)), full((1, GLA_DV)), st],
        out_specs=[pl.BlockSpec((nb, 1024), lambda i: (i, 0)), st],
        out_shape=[jax.ShapeDtypeStruct((n, 1024), F32), jax.ShapeDtypeStruct(state.shape, F32)],
        scratch_shapes=[pltpu.VMEM((nb, 1024), F32)],
        compiler_params=_params(("parallel",)),
        name="gla_step",
    )(xa, xa, xa, xa, sm, wg2, bgate, normw, state)


def _ssd_dec_body(z_ref, x_ref, b_ref, c_ref, sm_ref, cst_ref, cw, cbias, dtb_ref, alog_ref, e_ref, dvec_ref,
                  nw_ref, s_in, y_ref, s_out, cst_out, yt_scr, *, nb):
    xs = _silu(_conv_step(cst_ref, cst_out, x_ref[...], cw, cbias, 0, 2048))
    bm = _silu(_conv_step(cst_ref, cst_out, b_ref[...], cw, cbias, 2048, 2560))
    cm = _silu(_conv_step(cst_ref, cst_out, c_ref[...], cw, cbias, 2560, 3072)).astype(BF16)
    dt = _softplus(sm_ref[...] + dtb_ref[...])
    decay = jnp.exp(dt * (-jnp.exp(alog_ref[...])))
    c1_t = _dot_x01(decay, e_ref[...]).T
    c2_t = (_dot_x01(dt, e_ref[...]) * xs).T
    lane = lax.broadcasted_iota(jnp.int32, (SSD_GROUP_W, nb), 1)
    for g in range(SSD_GROUPS):
        gn = slice(g * SSD_D_STATE, (g + 1) * SSD_D_STATE)
        gw = slice(g * SSD_GROUP_W, (g + 1) * SSD_GROUP_W)
        rows = pl.ds(g * SSD_GROUP_W, SSD_GROUP_W)
        ycol = jnp.zeros((SSD_GROUP_W, nb), F32)
        for b in range(nb):
            sn = c1_t[gw, b:b + 1] * s_in[b, rows, :] + c2_t[gw, b:b + 1] * bm[b:b + 1, gn]
            s_out[b, rows, :] = sn
            ycol = jnp.where(lane == b, _dot_nt(sn.astype(BF16), cm[:, gn]), ycol)
        yt_scr[rows, :] = ycol
    y = yt_scr[...].T + dvec_ref[...] * xs
    y = y * _silu(z_ref[...])
    for g in range(SSD_GROUPS):
        gw = slice(g * SSD_GROUP_W, (g + 1) * SSD_GROUP_W)
        y_ref[:, gw] = _rms(y[:, gw], nw_ref[:, gw])


def _ssd_step(xb, sm, state, cstate, cw, cb, dtb, alog, emat, dvec, normw, *, nb):
    n = xb.shape[0]

    def seg(width, off):
        return pl.BlockSpec((nb, width), lambda i: (i, off // width))

    def full(shape):
        return pl.BlockSpec(shape, lambda i: (0,) * len(shape))

    st = pl.BlockSpec((nb, SSD_D_INNER, SSD_D_STATE), lambda i: (i, 0, 0))
    cst = pl.BlockSpec((nb, CONV_W - 1, 3072), lambda i: (i, 0, 0))
    return pl.pallas_call(
        functools.partial(_ssd_dec_body, nb=nb),
        grid=(n // nb,),
        in_specs=[seg(2048, SSD_Z), seg(2048, SSD_X), seg(512, SSD_B), seg(512, SSD_C), seg(LANES, 0), cst,
                  full((CONV_W, 3072)), full((1, 3072)), full((1, LANES)), full((1, LANES)), full((LANES, 2048)),
                  full((1, 2048)), full((1, 2048)), st],
        out_specs=[pl.BlockSpec((nb, 2048), lambda i: (i, 0)), st, cst],
        out_shape=[jax.ShapeDtypeStruct((n, 2048), F32), jax.ShapeDtypeStruct(state.shape, F32),
                   jax.ShapeDtypeStruct(cstate.shape, F32)],
        scratch_shapes=[pltpu.VMEM((SSD_D_INNER, nb), F32)],
        compiler_params=_params(("parallel",)),
        name="ssd_step",
    )(xb, xb, xb, xb, sm, cstate, cw, cb, dtb, alog, emat, dvec, normw, state)


def _lru_dec_body(rg_ref, rx_ref, cst_ref, cw, cbias, wa_ref, ba_ref, wx_ref, bx_ref, lam_ref, h_in,
                  o_ref, h_out, cst_out):
    xc = _conv_step(cst_ref, cst_out, rx_ref[...], cw, cbias, 0, LRU_WIDTH)
    a, b = _lru_gates(xc, wa_ref, ba_ref, wx_ref, bx_ref, lam_ref)
    h = a * h_in[...] + b
    h_out[...] = h
    o_ref[...] = _gelu_tanh(rg_ref[...]) * h


def _lru_step(xl, hstate, cstate, cw, cbias, wa, ba, wx, bx, lam):
    n = xl.shape[0]

    def full(shape):
        return pl.BlockSpec(shape, lambda i: (0,) * len(shape))

    return pl.pallas_call(
        _lru_dec_body,
        grid=(1,),
        in_specs=[pl.BlockSpec((n, LRU_WIDTH), lambda i: (0, LRU_RG // LRU_WIDTH)),
                  pl.BlockSpec((n, LRU_WIDTH), lambda i: (0, LRU_RX // LRU_WIDTH)),
                  full((n, CONV_W - 1, LRU_WIDTH)), full((CONV_W, LRU_WIDTH)), full((1, LRU_WIDTH)),
                  full((LRU_BLOCKS, LRU_BLOCK_W, LRU_BLOCK_W)), full((1, LRU_WIDTH)),
                  full((LRU_BLOCKS, LRU_BLOCK_W, LRU_BLOCK_W)), full((1, LRU_WIDTH)), full((1, LRU_WIDTH)),
                  full((n, LRU_WIDTH))],
        out_specs=[full((n, LRU_WIDTH)), full((n, LRU_WIDTH)), full((n, CONV_W - 1, LRU_WIDTH))],
        out_shape=[jax.ShapeDtypeStruct((n, LRU_WIDTH), F32), jax.ShapeDtypeStruct((n, LRU_WIDTH), F32),
                   jax.ShapeDtypeStruct(cstate.shape, F32)],
        compiler_params=_params(("arbitrary",)),
        name="lru_step",
    )(xl, xl, cstate, cw, cbias, wa, ba, wx, bx, lam, hstate)


def _gdn_dec_body(q_ref, k_ref, v_ref, z_ref, sm_ref, cst_ref, cw, alog_ref, dtb_ref, nw_ref, s_in,
                  o_ref, s_out, cst_out, o_scr, *, nb):
    q = _silu(_conv_step(cst_ref, cst_out, q_ref[...], cw, None, 0, 1024))
    k = _silu(_conv_step(cst_ref, cst_out, k_ref[...], cw, None, 1024, 2048))
    v = _silu(_conv_step(cst_ref, cst_out, v_ref[...], cw, None, 2048, 3072))
    sm = sm_ref[...]
    eg_all = jnp.exp(-jnp.exp(alog_ref[...]) * _softplus(sm + dtb_ref[...]))
    beta_all = _sigmoid(sm)
    qn = jnp.concatenate([_l2norm(q[:, h * GDN_DK:(h + 1) * GDN_DK]) for h in range(GDN_HEADS)], axis=-1)
    qn = qn * (GDN_DK ** -0.5)
    kn = jnp.concatenate([_l2norm(k[:, h * GDN_DK:(h + 1) * GDN_DK]) for h in range(GDN_HEADS)], axis=-1)
    q_t, k_t = qn.T, kn.T
    for b in range(nb):
        for h in range(GDN_HEADS):
            hd = slice(h * GDN_DK, (h + 1) * GDN_DK)
            s = s_in[b, h]
            kcol, qcol = k_t[hd, b:b + 1], q_t[hd, b:b + 1]
            eg = eg_all[b:b + 1, h:h + 1]
            beta = beta_all[b:b + 1, GDN_HEADS + h:GDN_HEADS + h + 1]
            ks = jnp.sum(kcol * s, axis=0, keepdims=True)
            qs = jnp.sum(qcol * s, axis=0, keepdims=True)
            delta = beta * (v[b:b + 1, hd] - eg * ks)
            qk = jnp.sum(qn[b:b + 1, hd] * kn[b:b + 1, hd], axis=-1, keepdims=True)
            o_scr[b:b + 1, hd] = eg * qs + qk * delta
            s_out[b, h] = eg * s + kcol * delta
    o = o_scr[...]
    z = z_ref[...]
    for h in range(GDN_HEADS):
        hd = slice(h * GDN_DK, (h + 1) * GDN_DK)
        o_ref[:, hd] = _rms(o[:, hd], nw_ref[...]) * _silu(z[:, hd])


def _gdn_step(xg, sm, state, cstate, cw, alog, dtb, normw, *, nb):
    n = xg.shape[0]

    def seg(width, off):
        return pl.BlockSpec((nb, width), lambda i: (i, off // width))

    def full(shape):
        return pl.BlockSpec(shape, lambda i: (0,) * len(shape))

    st = pl.BlockSpec((nb, GDN_HEADS, GDN_DK, GDN_DV), lambda i: (i, 0, 0, 0))
    cst = pl.BlockSpec((nb, CONV_W - 1, 3072), lambda i: (i, 0, 0))
    return pl.pallas_call(
        functools.partial(_gdn_dec_body, nb=nb),
        grid=(n // nb,),
        in_specs=[seg(1024, GDN_Q), seg(1024, GDN_K), seg(1024, GDN_V), seg(1024, GDN_Z), seg(LANES, 0), cst,
                  full((CONV_W, 3072)), full((1, LANES)), full((1, LANES)), full((1, GDN_DV)), st],
        out_specs=[pl.BlockSpec((nb, 1024), lambda i: (i, 0)), st, cst],
        out_shape=[jax.ShapeDtypeStruct((n, 1024), F32), jax.ShapeDtypeStruct(state.shape, F32),
                   jax.ShapeDtypeStruct(cstate.shape, F32)],
        scratch_shapes=[pltpu.VMEM((nb, 1024), F32)],
        compiler_params=_params(("parallel",)),
        name="gdn_step",
    )(xg, xg, xg, xg, sm, cstate, cw, alog, dtb, normw, state)


def _trunk_sample(x, s_gla, s_ssd, s_ssd_conv, s_lru, s_lru_conv, s_gdn, s_gdn_conv, norms, final_norm, w, *, nb=8):
    n = x.shape[0]
    f3 = (F32, F32, F32)
    h = x.reshape(n, D_MODEL)
    h = _ffn(h, norms[0, 0], w["ffn_in"], w["ffn_out"], (0, 0), tm=n)
    xa, xb, sm = _proj(h, norms[0, 1], w["l0_in"], L0_WIDTHS, f3, tm=n)
    o_a, n_gla = _gla_step(xa, sm, s_gla, w["gla_wg2"], w["gla_bg"], w["gla_norm"], nb=nb)
    y, n_ssd, n_ssd_conv = _ssd_step(xb, sm, s_ssd.reshape(n, SSD_D_INNER, SSD_D_STATE), s_ssd_conv, w["ssd_cw"],
                                     w["ssd_cb"], w["ssd_dtb"], w["ssd_alog"], w["ssd_e"], w["ssd_dvec"],
                                     w["ssd_norm"], nb=nb)
    h = _ffn(h, norms[0, 2], w["ffn_in"], w["ffn_out"], (0, 1), pre=(o_a, y, w["l0_out"]), tm=n)
    h = _ffn(h, norms[1, 0], w["ffn_in"], w["ffn_out"], (1, 0), tm=n)
    xg, xl, sm = _proj(h, norms[1, 1], w["l1_in"], L1_WIDTHS, f3, tm=n)
    o_c, n_lru, n_lru_conv = _lru_step(xl, s_lru, s_lru_conv, w["lru_cw"], w["lru_cb"], w["lru_wa"], w["lru_ba"],
                                       w["lru_wx"], w["lru_bx"], w["lru_lam"])
    o_d, n_gdn, n_gdn_conv = _gdn_step(xg, sm, s_gdn, s_gdn_conv, w["gdn_cw"], w["gdn_alog"], w["gdn_dtb"],
                                       w["gdn_norm"], nb=nb)
    h = _ffn(h, norms[1, 2], w["ffn_in"], w["ffn_out"], (1, 1), final_norm,
             pre=(o_c, o_d, w["l1_out"]), tm=n)
    return (h.reshape(n, 1, D_MODEL), n_gla, n_ssd.reshape(s_ssd.shape), n_ssd_conv, n_lru, n_lru_conv, n_gdn,
            n_gdn_conv)


def kernel(x_prompt, x_sample, state_gla, state_ssd, state_ssd_conv, state_lru, state_lru_conv, state_gdn,
           state_gdn_conv, norms, final_norm, ffn_w_in, ffn_w_out, l0_w_in, l0_w_out, gla_w_gate2, gla_b_gate,
           gla_norm, ssd_conv_w, ssd_conv_b, ssd_dt_bias, ssd_a_log, ssd_d, ssd_norm, l1_w_in, l1_w_out,
           lru_conv_w, lru_conv_b, lru_w_a, lru_b_a, lru_w_x, lru_b_x, lru_lambda, gdn_conv_w, gdn_a_log,
           gdn_dt_bias, gdn_norm):
    w = _prep(ffn_w_in, ffn_w_out, l0_w_in, l0_w_out, gla_w_gate2, gla_b_gate, gla_norm, ssd_conv_w, ssd_conv_b,
              ssd_dt_bias, ssd_a_log, ssd_d, ssd_norm, l1_w_in, l1_w_out, lru_conv_w, lru_conv_b, lru_w_a, lru_b_a,
              lru_w_x, lru_b_x, lru_lambda, gdn_conv_w, gdn_a_log, gdn_dt_bias, gdn_norm)
    prompt = _trunk_prompt(x_prompt, norms, final_norm, w, tm=512, cs=128)
    sample = _trunk_sample(x_sample, state_gla, state_ssd, state_ssd_conv, state_lru, state_lru_conv, state_gdn,
                           state_gdn_conv, norms, final_norm, w)
    return (prompt[0], sample[0]) + tuple(prompt[1:]) + tuple(sample[1:])
```

```python
import functools
import math

import jax
import jax.numpy as jnp
from jax import lax
from jax.experimental import pallas as pl
from jax.experimental.pallas import tpu as pltpu

F32 = jnp.float32
BF16 = jnp.bfloat16

D_MODEL = 1024
NORM_EPS = 1e-6
CONV_W = 4
D_FF = 2816
FFN_RES = 0.5

GLA_HEADS = 4
GLA_DK = 128
GLA_DV = 256
GLA_RANK = 16
GLA_GATE_NORM = 16.0
GLA_SUB = 16
GLA_FAST_MAX = 60.0

SSD_D_INNER = 2048
SSD_HEAD_DIM = 64
SSD_HEADS = 32
SSD_GROUPS = 4
SSD_HPG = 8
SSD_D_STATE = 128
SSD_GROUP_W = SSD_HPG * SSD_HEAD_DIM

LRU_WIDTH = 1280
LRU_BLOCKS = 10
LRU_BLOCK_W = 128
LRU_C = 8.0

GDN_HEADS = 8
GDN_DK = 128
GDN_DV = 128

LANES = 128
SUBLANES = 8
NEG_BIG = -1e30
VMEM_LIMIT = 56 * 1024 * 1024

GLA_W = 3072
GLA_GV, GLA_GG, GLA_GQ, GLA_GK = 0, 1024, 2048, 2560
SSD_W = 5120
SSD_Z, SSD_X, SSD_B, SSD_C = 0, 2048, 4096, 4608
GDN_W = 4096
GDN_Q, GDN_K, GDN_V, GDN_Z = 0, 1024, 2048, 3072
LRU_W2 = 2560
LRU_RG, LRU_RX = 0, 1280


def _sigmoid(x):
    return 0.5 * jnp.tanh(0.5 * x) + 0.5


def _silu(x):
    hx = 0.5 * x
    return hx * jnp.tanh(hx) + hx


def _softplus(x):
    return jnp.maximum(x, 0.0) + jnp.log(1.0 + jnp.exp(-jnp.abs(x)))


def _rms(x, g):
    return x * lax.rsqrt(jnp.mean(x * x, axis=-1, keepdims=True) + NORM_EPS) * g


def _dot(a, b):
    return jnp.dot(a, b, preferred_element_type=F32)


def _dot_nt(a, b):
    return lax.dot_general(a, b, (((1,), (1,)), ((), ())), preferred_element_type=F32)


def _dot_tn(a, b):
    return lax.dot_general(a, b, (((0,), (0,)), ((), ())), preferred_element_type=F32)


def _split2(x):
    hi = x.astype(BF16)
    lo = (x - hi.astype(F32)).astype(BF16)
    return hi, lo


def _split3(x):
    hi = x.astype(BF16)
    r = x - hi.astype(F32)
    mid = r.astype(BF16)
    lo = (r - mid.astype(F32)).astype(BF16)
    return hi, mid, lo


def _dot01_exact(m01, x):
    hi, mid, lo = _split3(x)
    return _dot(m01, hi) + _dot(m01, mid) + _dot(m01, lo)


def _dot_x01(x, m01):
    hi, lo = _split2(x)
    return _dot(hi, m01) + _dot(lo, m01)


def _lower_mask(n, strict=False):
    row = lax.broadcasted_iota(jnp.int32, (n, n), 0)
    col = lax.broadcasted_iota(jnp.int32, (n, n), 1)
    return row > col if strict else row >= col


def _params(sem):
    return pltpu.CompilerParams(dimension_semantics=sem, vmem_limit_bytes=VMEM_LIMIT)


FFN_CHUNK = 256


def _resident(shape):
    return pl.BlockSpec(shape, lambda i: (0,) * len(shape), pipeline_mode=pl.Buffered(1))


def _ffn_body(*refs, final, pre):
    refs = list(refs)
    x_ref = refs.pop(0)
    if pre:
        a_ref, b_ref, wab_ref = refs[:3]
        refs = refs[3:]
    g_ref, wi_ref, wo_ref = refs[:3]
    refs = refs[3:]
    fg_ref = refs.pop(0) if final else None
    o_ref, act_ref = refs
    x = x_ref[...]
    if pre:
        ka = a_ref.shape[1]
        x = x + _dot(a_ref[...].astype(BF16), wab_ref[:ka, :]) + _dot(b_ref[...].astype(BF16), wab_ref[ka:, :])
    hn = _rms(x, g_ref[...]).astype(BF16)
    for c in range(D_FF // FFN_CHUNK):
        cols = slice(c * FFN_CHUNK, (c + 1) * FFN_CHUNK)
        gate = _dot(hn, wi_ref[:, cols])
        up = _dot(hn, wi_ref[:, D_FF + c * FFN_CHUNK:D_FF + (c + 1) * FFN_CHUNK])
        act_ref[:, cols] = (_silu(gate) * up).astype(BF16)
    y = x + FFN_RES * _dot(act_ref[...], wo_ref[...])
    if final:
        y = _rms(y, fg_ref[...])
    o_ref[...] = y


def _ffn(x, g, w_in, w_out, sel, final_g=None, pre=None, *, tm):
    t = x.shape[0]
    final = final_g is not None
    in_specs = [pl.BlockSpec((tm, D_MODEL), lambda i: (i, 0))]
    args = [x]
    if pre is not None:
        a, b, wab = pre
        in_specs += [pl.BlockSpec((tm, a.shape[1]), lambda i: (i, 0)), pl.BlockSpec((tm, b.shape[1]), lambda i: (i, 0)),
                     _resident(wab.shape)]
        args += [a, b, wab]
    in_specs += [_resident((1, D_MODEL)),
                 pl.BlockSpec((None, None) + w_in.shape[2:], lambda i: sel + (0, 0), pipeline_mode=pl.Buffered(1)),
                 pl.BlockSpec((None, None) + w_out.shape[2:], lambda i: sel + (0, 0), pipeline_mode=pl.Buffered(1))]
    args += [g.reshape(1, D_MODEL), w_in, w_out]
    if final:
        in_specs.append(_resident((1, D_MODEL)))
        args.append(final_g.reshape(1, D_MODEL))
    return pl.pallas_call(
        functools.partial(_ffn_body, final=final, pre=pre is not None),
        grid=(t // tm,),
        in_specs=in_specs,
        out_specs=pl.BlockSpec((tm, D_MODEL), lambda i: (i, 0)),
        out_shape=jax.ShapeDtypeStruct((t, D_MODEL), F32),
        scratch_shapes=[pltpu.VMEM((tm, D_FF), BF16)],
        compiler_params=_params(("parallel",)),
        name="ffn",
    )(*args)


PROJ_CHUNK = 1024


def _proj_body(x_ref, g_ref, w_ref, *o_refs, widths):
    hn = _rms(x_ref[...], g_ref[...]).astype(BF16)
    off = 0
    for o_ref, width in zip(o_refs, widths):
        for c in range(0, width, PROJ_CHUNK):
            n = min(PROJ_CHUNK, width - c)
            o_ref[:, c:c + n] = _dot(hn, w_ref[:, off + c:off + c + n]).astype(o_ref.dtype)
        off += width


def _proj(x, g, w, widths, dtypes, *, tm):
    t = x.shape[0]
    return pl.pallas_call(
        functools.partial(_proj_body, widths=tuple(widths)),
        grid=(t // tm,),
        in_specs=[pl.BlockSpec((tm, D_MODEL), lambda i: (i, 0)), _resident((1, D_MODEL)), _resident(w.shape)],
        out_specs=[pl.BlockSpec((tm, n), lambda i: (i, 0)) for n in widths],
        out_shape=[jax.ShapeDtypeStruct((t, n), dt) for n, dt in zip(widths, dtypes)],
        compiler_params=_params(("parallel",)),
        name="proj",
    )(x, g.reshape(1, D_MODEL), w)


def _shift_matrix(c):
    out_row = jnp.arange(3 * c)
    k = out_row // c + 1
    src = out_row % c - k
    return (src[:, None] == jnp.arange(c)[None, :]).astype(BF16)


def _conv_chunk(tail, src, shift_ref, w_ref, bias_ref, c):
    x = src.astype(F32)
    sh = _dot(shift_ref[...], src)
    w0, w1, w2, w3 = (w_ref[j:j + 1, :] for j in range(CONV_W))
    acc = w3 * x + w2 * sh[:c] + w1 * sh[c:2 * c] + w0 * sh[2 * c:]
    if bias_ref is not None:
        acc = acc + bias_ref[...]
    t3, t2, t1 = tail[5:6, :], tail[6:7, :], tail[7:8, :]
    row = lax.broadcasted_iota(jnp.int32, (SUBLANES, x.shape[1]), 0)
    head = jnp.where(row == 0, w0 * t3 + w1 * t2 + w2 * t1,
                     jnp.where(row == 1, w0 * t2 + w1 * t1, jnp.where(row == 2, w0 * t1, 0.0)))
    tail[pl.ds(5, 3), :] = x[c - 3:c, :]
    return jnp.concatenate([acc[:SUBLANES] + head, acc[SUBLANES:]], axis=0)


def _conv_state(tail):
    return tail[pl.ds(5, 3), :]


def _gla_body(q_in, k_in, v_in, gg_ref, sm_ref, wg2_ref, bg_ref, nw_ref, o_ref, s_out, st_ref, b_ref,
              q_ref, k_ref, v_ref, *, lg, nc):
    c = pl.program_id(1)

    @pl.when(c == 0)
    def _():
        st_ref[...] = jnp.zeros(st_ref.shape, F32)

    pre = _dot(sm_ref[0].astype(BF16), wg2_ref[...]) + bg_ref[...]
    log_a = -_softplus(-pre) * (1.0 / GLA_GATE_NORM)
    lower = _lower_mask(lg)
    b_all = _dot01_exact(jnp.where(lower, 1.0, 0.0).astype(BF16), log_a)
    b_ref[...] = b_all
    fast = jnp.max(-b_all[lg - 1:lg, :]) <= GLA_FAST_MAX

    q = q_in[0].astype(F32) * (GLA_DK ** -0.5)
    k = k_in[0].astype(F32)
    vb = v_in[0]
    blast = b_all[lg - 1:lg, :]
    qe = (q * jnp.exp(b_all)).astype(BF16)
    ke = (k * jnp.exp(-b_all)).astype(BF16)
    kd = (k * jnp.exp(blast - b_all)).astype(BF16)
    elast = jnp.exp(blast)
    gg = gg_ref[0].astype(F32)
    for h in range(GLA_HEADS):
        dk = slice(h * GLA_DK, (h + 1) * GLA_DK)
        dv = slice(h * GLA_DV, (h + 1) * GLA_DV)
        att = jnp.where(lower, _dot_nt(qe[:, dk], ke[:, dk]), 0.0).astype(BF16)
        st = st_ref[h]
        oh = _dot(att, vb[:, dv]) + _dot_nt(qe[:, dk], st.astype(BF16))
        st_ref[h] = jnp.where(fast, st * elast[:, dk] + _dot_tn(vb[:, dv], kd[:, dk]), st)
        o_ref[0, :, dv] = (_rms(oh, nw_ref[...]) * _silu(gg[:, dv])).astype(BF16)

    @pl.when(jnp.logical_not(fast))
    def _():
        q_ref[...] = q_in[0].astype(F32) * (GLA_DK ** -0.5)
        k_ref[...] = k_in[0].astype(F32)
        v_ref[...] = v_in[0].astype(F32)
        rowi = lax.broadcasted_iota(jnp.int32, (GLA_SUB, 1), 0)

        def block(i, carry):
            base = pl.multiple_of(i * GLA_SUB, GLA_SUB)
            q = q_ref[pl.ds(base, GLA_SUB), :]
            k = k_ref[pl.ds(base, GLA_SUB), :]
            v = v_ref[pl.ds(base, GLA_SUB), :]
            b = b_ref[pl.ds(base, GLA_SUB), :]
            o = [jnp.zeros((GLA_SUB, GLA_DV), F32) for _ in range(GLA_HEADS)]
            for s in range(GLA_SUB):
                ks = k_ref[pl.ds(base + s, 1), :]
                bs = b_ref[pl.ds(base + s, 1), :]
                vs = v_ref[pl.ds(base + s, 1), :]
                p = q * ks * jnp.exp(b - bs)
                for h in range(GLA_HEADS):
                    att = jnp.sum(p[:, h * GLA_DK:(h + 1) * GLA_DK], axis=-1, keepdims=True)
                    att = jnp.where(rowi >= s, att, 0.0)
                    o[h] = o[h] + att * vs[:, h * GLA_DV:(h + 1) * GLA_DV]
            bprev = jnp.where(i > 0, b_ref[pl.ds(jnp.maximum(base - 1, 0), 1), :], 0.0)
            blast = b_ref[pl.ds(base + GLA_SUB - 1, 1), :]
            qe = (q * jnp.exp(b - bprev)).astype(BF16)
            kd = (k * jnp.exp(blast - b)).astype(BF16)
            elast = jnp.exp(blast - bprev)
            vb = v.astype(BF16)
            gg = gg_ref[0, pl.ds(base, GLA_SUB), :].astype(F32)
            for h in range(GLA_HEADS):
                dk = slice(h * GLA_DK, (h + 1) * GLA_DK)
                dv = slice(h * GLA_DV, (h + 1) * GLA_DV)
                st = st_ref[h]
                oh = o[h] + _dot_nt(qe[:, dk], st.astype(BF16))
                st_ref[h] = st * elast[:, dk] + _dot_tn(vb[:, dv], kd[:, dk])
                o_ref[0, pl.ds(base, GLA_SUB), dv] = (_rms(oh, nw_ref[...]) * _silu(gg[:, dv])).astype(BF16)
            return carry

        lax.fori_loop(0, lg // GLA_SUB, block, 0)

    @pl.when(c == nc - 1)
    def _():
        for h in range(GLA_HEADS):
            s_out[0, h] = st_ref[h].T


def _gla_prompt(xa, sm, wg2, bgate, normw, *, lg):
    nb, l = xa.shape[0], xa.shape[1]
    nc = l // lg

    def seg(width, off):
        return pl.BlockSpec((1, lg, width), lambda b, c: (b, c, off // width))

    def full(shape):
        return pl.BlockSpec(shape, lambda b, c: (0,) * len(shape))

    return pl.pallas_call(
        functools.partial(_gla_body, lg=lg, nc=nc),
        grid=(nb, nc),
        in_specs=[seg(512, GLA_GQ), seg(512, GLA_GK), seg(1024, GLA_GV), seg(1024, GLA_GG),
                  pl.BlockSpec((1, lg, LANES), lambda b, c: (b, c, 0)),
                  full((LANES, 512)), full((1, 512)), full((1, GLA_DV))],
        out_specs=[pl.BlockSpec((1, lg, 1024), lambda b, c: (b, c, 0)),
                   pl.BlockSpec((1, GLA_HEADS, GLA_DK, GLA_DV), lambda b, c: (b, 0, 0, 0))],
        out_shape=[jax.ShapeDtypeStruct((nb, l, 1024), BF16),
                   jax.ShapeDtypeStruct((nb, GLA_HEADS, GLA_DK, GLA_DV), F32)],
        scratch_shapes=[pltpu.VMEM((GLA_HEADS, GLA_DV, GLA_DK), F32), pltpu.VMEM((lg, 512), F32),
                        pltpu.VMEM((lg, 512), F32), pltpu.VMEM((lg, 512), F32), pltpu.VMEM((lg, 1024), F32)],
        compiler_params=_params(("parallel", "arbitrary")),
        name="gla_prompt",
    )(xa, xa, xa, xa, sm, wg2, bgate, normw)


def _ssd_body(z_ref, x_ref, b_ref, c_ref, sm_ref, shift_ref, cwx, cbx, cwb, cbb, cwc, cbc, dtb_ref, alog_ref, e_ref, et_ref,
              dvec_ref, nw_ref, y_ref, s_out, cx_out, cb_out, cc_out, s_ref, bufx, bufb, bufc, y_scr, *, cs, nc):
    c = pl.program_id(1)

    @pl.when(c == 0)
    def _():
        s_ref[...] = jnp.zeros(s_ref.shape, F32)
        for buf in (bufx, bufb, bufc):
            buf[...] = jnp.zeros(buf.shape, F32)

    xs = _silu(_conv_chunk(bufx, x_ref[0], shift_ref, cwx, cbx, cs))
    bm = _silu(_conv_chunk(bufb, b_ref[0], shift_ref, cwb, cbb, cs)).astype(BF16)
    cm = _silu(_conv_chunk(bufc, c_ref[0], shift_ref, cwc, cbc, cs)).astype(BF16)

    dt = _softplus(sm_ref[0] + dtb_ref[...])
    da = dt * (-jnp.exp(alog_ref[...]))
    lower = _lower_mask(cs)
    tri = jnp.where(lower, 1.0, 0.0).astype(BF16)
    cum = _dot01_exact(tri, da)
    cum_t = cum.T
    dt_t = dt.T
    last = cum[cs - 1:cs, :]
    ecum_x = _dot_x01(jnp.exp(cum), e_ref[...])
    w_x = _dot_x01(jnp.exp(last - cum) * dt, e_ref[...])
    elast = jnp.broadcast_to(jnp.exp(last), (SUBLANES, LANES))
    eh, el = _split2(elast)
    ecol = _dot_nt(et_ref[...], eh) + _dot_nt(et_ref[...], el)

    xs_bf = xs.astype(BF16)
    xw = (xs * w_x).astype(BF16)
    z = z_ref[0].astype(F32)
    lane = lax.broadcasted_iota(jnp.int32, (cs, LANES), 1)
    for g in range(SSD_GROUPS):
        gn = slice(g * SSD_D_STATE, (g + 1) * SSD_D_STATE)
        gw = slice(g * SSD_GROUP_W, (g + 1) * SSD_GROUP_W)
        cg, bg = cm[:, gn], bm[:, gn]
        cb = _dot_nt(cg, bg)
        sg = s_ref[pl.ds(g * SSD_GROUP_W, SSD_GROUP_W), :]
        y_inter = _dot_nt(cg, sg.astype(BF16)) * ecum_x[:, gw]
        ss = jnp.zeros((cs, 1), F32)
        for p in range(SSD_HPG // 2):
            pair = slice(g * SSD_GROUP_W + p * LANES, g * SSD_GROUP_W + (p + 1) * LANES)
            xp = xs_bf[:, pair]
            ys = []
            for h in (g * SSD_HPG + 2 * p, g * SSD_HPG + 2 * p + 1):
                seg = jnp.exp(jnp.where(lower, cum[:, h:h + 1] - cum_t[h:h + 1, :], NEG_BIG))
                m = cb * seg * dt_t[h:h + 1, :]
                ys.append(_dot(m.astype(BF16), xp))
            yp = jnp.where(lane < SSD_HEAD_DIM, ys[0], ys[1])
            yp = yp + y_inter[:, p * LANES:(p + 1) * LANES] + dvec_ref[:, pair] * xs[:, pair]
            yp = yp * _silu(z[:, pair])
            ss = ss + jnp.sum(yp * yp, axis=-1, keepdims=True)
            y_scr[:, pair] = yp
        inv = lax.rsqrt(ss * (1.0 / SSD_GROUP_W) + NORM_EPS)
        y_ref[0, :, gw] = (y_scr[:, gw] * inv * nw_ref[:, gw]).astype(BF16)
        s_ref[pl.ds(g * SSD_GROUP_W, SSD_GROUP_W), :] = (
            ecol[g * SSD_GROUP_W:(g + 1) * SSD_GROUP_W, 0:1] * sg + _dot_tn(xw[:, gw], bg))

    @pl.when(c == nc - 1)
    def _():
        s_out[0] = s_ref[...]
        cx_out[0] = _conv_state(bufx)
        cb_out[0] = _conv_state(bufb)
        cc_out[0] = _conv_state(bufc)


def _ssd_prompt(xb, sm, cw, cb, dtb, alog, emat, emat_t, dvec, normw, *, cs):
    nb, l = xb.shape[0], xb.shape[1]
    nc = l // cs

    def seg(width, off):
        return pl.BlockSpec((1, cs, width), lambda b, c: (b, c, off // width))

    def full(shape):
        return pl.BlockSpec(shape, lambda b, c: (0,) * len(shape))

    def tail(width):
        return pl.BlockSpec((1, CONV_W - 1, width), lambda b, c: (b, 0, 0))

    cwx, cwb, cwc = cw[:, :2048], cw[:, 2048:2560], cw[:, 2560:]
    cbx, cbb, cbc = cb[:, :2048], cb[:, 2048:2560], cb[:, 2560:]
    return pl.pallas_call(
        functools.partial(_ssd_body, cs=cs, nc=nc),
        grid=(nb, nc),
        in_specs=[seg(2048, SSD_Z), seg(2048, SSD_X), seg(512, SSD_B), seg(512, SSD_C),
                  pl.BlockSpec((1, cs, LANES), lambda b, c: (b, c, 0)), full((3 * cs, cs)),
                  full((CONV_W, 2048)), full((1, 2048)), full((CONV_W, 512)), full((1, 512)),
                  full((CONV_W, 512)), full((1, 512)), full((1, LANES)), full((1, LANES)),
                  full((LANES, 2048)), full((2048, LANES)), full((1, 2048)), full((1, 2048))],
        out_specs=[pl.BlockSpec((1, cs, 2048), lambda b, c: (b, c, 0)),
                   pl.BlockSpec((1, 2048, SSD_D_STATE), lambda b, c: (b, 0, 0)),
                   tail(2048), tail(512), tail(512)],
        out_shape=[jax.ShapeDtypeStruct((nb, l, 2048), BF16),
                   jax.ShapeDtypeStruct((nb, 2048, SSD_D_STATE), F32),
                   jax.ShapeDtypeStruct((nb, CONV_W - 1, 2048), F32),
                   jax.ShapeDtypeStruct((nb, CONV_W - 1, 512), F32),
                   jax.ShapeDtypeStruct((nb, CONV_W - 1, 512), F32)],
        scratch_shapes=[pltpu.VMEM((2048, SSD_D_STATE), F32),
                        pltpu.VMEM((SUBLANES, 2048), F32),
                        pltpu.VMEM((SUBLANES, 512), F32),
                        pltpu.VMEM((SUBLANES, 512), F32),
                        pltpu.VMEM((cs, 2048), F32)],
        compiler_params=_params(("parallel", "arbitrary")),
        name="ssd_prompt",
    )(xb, xb, xb, xb, sm, _shift_matrix(cs), cwx, cbx, cwb, cbb, cwc, cbc, dtb, alog, emat, emat_t, dvec, normw)


def _lru_gates(xc, wa_ref, ba_ref, wx_ref, bx_ref, lam_ref):
    xb = xc.astype(BF16)
    rl, il = [], []
    for n in range(LRU_BLOCKS):
        blk = xb[:, n * LRU_BLOCK_W:(n + 1) * LRU_BLOCK_W]
        rl.append(_dot(blk, wa_ref[n]))
        il.append(_dot(blk, wx_ref[n]))
    r_logit = jnp.concatenate(rl, axis=-1) + ba_ref[...]
    i_logit = jnp.concatenate(il, axis=-1) + bx_ref[...]
    log_a = -LRU_C * _sigmoid(r_logit) * _softplus(-lam_ref[...])
    a = jnp.exp(log_a)
    b = jnp.sqrt(1.0 - jnp.exp(2.0 * log_a)) * (_sigmoid(i_logit) * xc)
    return a, b


def _gelu_tanh(x):
    return 0.5 * x * (1.0 + jnp.tanh(math.sqrt(2.0 / math.pi) * (x + 0.044715 * (x * x * x))))


def _lru_body(rg_ref, rx_ref, shift_ref, cw, cbias, wa_ref, ba_ref, wx_ref, bx_ref, lam_ref, o_ref, h_out, c_out,
              h_ref, buf, a_scr, b_scr, o_scr, *, cs, nc):
    c = pl.program_id(1)

    @pl.when(c == 0)
    def _():
        h_ref[...] = jnp.zeros(h_ref.shape, F32)
        buf[...] = jnp.zeros(buf.shape, F32)

    xc = _conv_chunk(buf, rx_ref[0], shift_ref, cw, cbias, cs)
    a, b = _lru_gates(xc, wa_ref, ba_ref, wx_ref, bx_ref, lam_ref)
    tiles = (cs // SUBLANES, SUBLANES, LRU_WIDTH)
    a, b = a.reshape(tiles), b.reshape(tiles)
    rowm = lax.broadcasted_iota(jnp.int32, tiles, 1)
    for sh in (1, 2, 4):
        keep = rowm >= sh
        a_prev = jnp.where(keep, pltpu.roll(a, sh, 1), 1.0)
        b_prev = jnp.where(keep, pltpu.roll(b, sh, 1), 0.0)
        b = a * b_prev + b
        a = a * a_prev
    a_scr[...] = a.reshape(cs, LRU_WIDTH)
    b_scr[...] = b.reshape(cs, LRU_WIDTH)
    h = h_ref[...]
    for t in range(cs // SUBLANES):
        rows = pl.ds(t * SUBLANES, SUBLANES)
        ht = a_scr[rows, :] * h + b_scr[rows, :]
        o_scr[rows, :] = ht
        h = jnp.broadcast_to(ht[SUBLANES - 1:SUBLANES, :], (SUBLANES, LRU_WIDTH))
    h_ref[...] = h
    o_ref[0] = (_gelu_tanh(rg_ref[0].astype(F32)) * o_scr[...]).astype(BF16)

    @pl.when(c == nc - 1)
    def _():
        h_out[0] = h[0:1, :]
        c_out[0] = _conv_state(buf)


def _lru_prompt(xl, cw, cbias, wa, ba, wx, bx, lam, *, cs):
    nb, l = xl.shape[0], xl.shape[1]
    nc = l // cs

    def full(shape):
        return pl.BlockSpec(shape, lambda b, c: (0,) * len(shape))

    return pl.pallas_call(
        functools.partial(_lru_body, cs=cs, nc=nc),
        grid=(nb, nc),
        in_specs=[pl.BlockSpec((1, cs, LRU_WIDTH), lambda b, c: (b, c, LRU_RG // LRU_WIDTH)),
                  pl.BlockSpec((1, cs, LRU_WIDTH), lambda b, c: (b, c, LRU_RX // LRU_WIDTH)),
                  full((3 * cs, cs)), full((CONV_W, LRU_WIDTH)), full((1, LRU_WIDTH)),
                  full((LRU_BLOCKS, LRU_BLOCK_W, LRU_BLOCK_W)), full((1, LRU_WIDTH)),
                  full((LRU_BLOCKS, LRU_BLOCK_W, LRU_BLOCK_W)), full((1, LRU_WIDTH)), full((1, LRU_WIDTH))],
        out_specs=[pl.BlockSpec((1, cs, LRU_WIDTH), lambda b, c: (b, c, 0)),
                   pl.BlockSpec((1, 1, LRU_WIDTH), lambda b, c: (b, 0, 0)),
                   pl.BlockSpec((1, CONV_W - 1, LRU_WIDTH), lambda b, c: (b, 0, 0))],
        out_shape=[jax.ShapeDtypeStruct((nb, l, LRU_WIDTH), BF16),
                   jax.ShapeDtypeStruct((nb, 1, LRU_WIDTH), F32),
                   jax.ShapeDtypeStruct((nb, CONV_W - 1, LRU_WIDTH), F32)],
        scratch_shapes=[pltpu.VMEM((SUBLANES, LRU_WIDTH), F32),
                        pltpu.VMEM((SUBLANES, LRU_WIDTH), F32),
                        pltpu.VMEM((cs, LRU_WIDTH), F32),
                        pltpu.VMEM((cs, LRU_WIDTH), F32),
                        pltpu.VMEM((cs, LRU_WIDTH), F32)],
        compiler_params=_params(("parallel", "arbitrary")),
        name="lru_prompt",
    )(xl, xl, _shift_matrix(cs), cw, cbias, wa, ba, wx, bx, lam)


def _l2norm(x):
    return x * lax.rsqrt(jnp.sum(x * x, axis=-1, keepdims=True) + NORM_EPS)


def _cat3(a, b, c, axis):
    return jnp.concatenate([a, b, c], axis=axis)


def _gdn_body(q_ref, k_ref, v_ref, z_ref, sm_ref, shift_ref, cwq, cwk, cwv, alog_ref, dtb_ref, nw_ref,
              o_ref, s_out, cq_out, ck_out, cv_out, s_ref, bufq, bufk, bufv,
              ph_scr, pl_scr, t_scr, qb_scr, kd_scr, aqk_scr, rh_scr, rl_scr, *, cs, nc):
    c = pl.program_id(1)

    @pl.when(c == 0)
    def _():
        s_ref[...] = jnp.zeros(s_ref.shape, F32)
        for buf in (bufq, bufk, bufv):
            buf[...] = jnp.zeros(buf.shape, F32)

    q = _silu(_conv_chunk(bufq, q_ref[0], shift_ref, cwq, None, cs))
    k = _silu(_conv_chunk(bufk, k_ref[0], shift_ref, cwk, None, cs))
    v = _silu(_conv_chunk(bufv, v_ref[0], shift_ref, cwv, None, cs))
    sm = sm_ref[0]
    g_all = -jnp.exp(alog_ref[...]) * _softplus(sm + dtb_ref[...])
    beta_all = _sigmoid(sm)
    lower = _lower_mask(cs)
    strict = _lower_mask(cs, strict=True)
    diag = lower & jnp.logical_not(strict)
    tri = jnp.where(lower, 1.0, 0.0).astype(BF16)
    gc = _dot01_exact(tri, g_all)
    gc_t = gc.T
    egc_all = jnp.exp(gc)

    for h in range(GDN_HEADS):
        hd = slice(h * GDN_DK, (h + 1) * GDN_DK)
        qh = _l2norm(q[:, hd]) * (GDN_DK ** -0.5)
        kh = _l2norm(k[:, hd])
        beta = beta_all[:, GDN_HEADS + h:GDN_HEADS + h + 1]
        gcol = gc[:, h:h + 1]
        dec = jnp.exp(jnp.where(lower, gcol - gc_t[h:h + 1, :], NEG_BIG))
        qb, kb = qh.astype(BF16), kh.astype(BF16)
        both = _dot_nt(jnp.concatenate([kb, qb], axis=0), kb)
        p = -(beta * jnp.where(strict, dec, 0.0) * both[:cs])
        hi, lo = _split2(p)
        ph_scr[h] = hi
        pl_scr[h] = lo
        t_scr[h] = jnp.where(diag, 1.0, p)
        qb_scr[h] = qb
        aqk_scr[h] = (dec * both[cs:]).astype(BF16)
        glast = gc[cs - 1:cs, h:h + 1]
        kd_scr[h] = (kh * jnp.exp(glast - gcol)).astype(BF16)
        rhs = jnp.concatenate([v[:, hd] * beta, kh * (beta * egc_all[:, h:h + 1])], axis=-1)
        hi, lo = _split2(rhs)
        rh_scr[h] = hi
        rl_scr[h] = lo

    levels = int(math.log2(cs))
    for lvl in range(levels):
        for h in range(GDN_HEADS):
            hi, lo = ph_scr[h], pl_scr[h]
            lhs = _cat3(hi, hi, lo, 1)
            if lvl == 0:
                p2 = _dot(lhs, _cat3(hi, lo, hi, 0))
                t_new = None
            else:
                t = t_scr[h]
                th, tl = _split2(t)
                if lvl < levels - 1:
                    rhs = _cat3(jnp.concatenate([hi, th], axis=1), jnp.concatenate([lo, tl], axis=1),
                                jnp.concatenate([hi, th], axis=1), 0)
                    both = _dot(lhs, rhs)
                    p2, t_new = both[:, :cs], t + both[:, cs:]
                else:
                    p2, t_new = None, t + _dot(lhs, _cat3(th, tl, th, 0))
            if p2 is not None:
                hi2, lo2 = _split2(p2)
                ph_scr[h] = hi2
                pl_scr[h] = lo2
            if t_new is not None:
                t_scr[h] = t_new

    z = z_ref[0].astype(F32)
    us, wqs = [], []
    for h in range(GDN_HEADS):
        th, tl = _split2(t_scr[h])
        rh, rl = rh_scr[h], rl_scr[h]
        sol = _dot(_cat3(th, th, tl, 1), _cat3(rh, rl, rh, 0))
        us.append(sol[:, :GDN_DV])
        wqs.append(jnp.concatenate([sol[:, GDN_DV:].astype(BF16), qb_scr[h]], axis=0))
    dbs, qss = [], []
    for h in range(GDN_HEADS):
        ws_qs = _dot(wqs[h], s_ref[h].astype(BF16))
        dbs.append((us[h] - ws_qs[:cs]).astype(BF16))
        qss.append(ws_qs[cs:])
    for h in range(GDN_HEADS):
        hd = slice(h * GDN_DK, (h + 1) * GDN_DK)
        o = egc_all[:, h:h + 1] * qss[h] + _dot(aqk_scr[h], dbs[h])
        s_ref[h] = jnp.exp(gc[cs - 1:cs, h:h + 1]) * s_ref[h] + _dot_tn(kd_scr[h], dbs[h])
        o_ref[0, :, hd] = (_rms(o, nw_ref[...]) * _silu(z[:, hd])).astype(BF16)

    @pl.when(c == nc - 1)
    def _():
        s_out[0] = s_ref[...]
        cq_out[0] = _conv_state(bufq)
        ck_out[0] = _conv_state(bufk)
        cv_out[0] = _conv_state(bufv)


def _gdn_prompt(xg, sm, cw, alog, dtb, normw, *, cs):
    nb, l = xg.shape[0], xg.shape[1]
    nc = l // cs

    def seg(width, off):
        return pl.BlockSpec((1, cs, width), lambda b, c: (b, c, off // width))

    def full(shape):
        return pl.BlockSpec(shape, lambda b, c: (0,) * len(shape))

    def tail(width):
        return pl.BlockSpec((1, CONV_W - 1, width), lambda b, c: (b, 0, 0))

    hm = (GDN_HEADS, cs, cs)
    return pl.pallas_call(
        functools.partial(_gdn_body, cs=cs, nc=nc),
        grid=(nb, nc),
        in_specs=[seg(1024, GDN_Q), seg(1024, GDN_K), seg(1024, GDN_V), seg(1024, GDN_Z),
                  pl.BlockSpec((1, cs, LANES), lambda b, c: (b, c, 0)), full((3 * cs, cs)),
                  full((CONV_W, 1024)), full((CONV_W, 1024)), full((CONV_W, 1024)),
                  full((1, LANES)), full((1, LANES)), full((1, GDN_DV))],
        out_specs=[pl.BlockSpec((1, cs, 1024), lambda b, c: (b, c, 0)),
                   pl.BlockSpec((1, GDN_HEADS, GDN_DK, GDN_DV), lambda b, c: (b, 0, 0, 0)),
                   tail(1024), tail(1024), tail(1024)],
        out_shape=[jax.ShapeDtypeStruct((nb, l, 1024), BF16),
                   jax.ShapeDtypeStruct((nb, GDN_HEADS, GDN_DK, GDN_DV), F32)]
        + [jax.ShapeDtypeStruct((nb, CONV_W - 1, 1024), F32)] * 3,
        scratch_shapes=[pltpu.VMEM((GDN_HEADS, GDN_DK, GDN_DV), F32)]
        + [pltpu.VMEM((SUBLANES, 1024), F32)] * 3
        + [pltpu.VMEM(hm, BF16), pltpu.VMEM(hm, BF16), pltpu.VMEM(hm, F32),
           pltpu.VMEM((GDN_HEADS, cs, GDN_DK), BF16), pltpu.VMEM((GDN_HEADS, cs, GDN_DK), BF16),
           pltpu.VMEM(hm, BF16), pltpu.VMEM((GDN_HEADS, cs, 2 * GDN_DV), BF16),
           pltpu.VMEM((GDN_HEADS, cs, 2 * GDN_DV), BF16)],
        compiler_params=_params(("parallel", "arbitrary")),
        name="gdn_prompt",
    )(xg, xg, xg, xg, sm, _shift_matrix(cs), cw[:, :1024], cw[:, 1024:2048], cw[:, 2048:], alog, dtb, normw)


def _pad_lanes(v, width, offset=0):
    return jnp.zeros((1, width), F32).at[0, offset:offset + v.shape[0]].set(v.astype(F32))


def _prep(ffn_w_in, ffn_w_out, l0_w_in, l0_w_out, gla_w_gate2, gla_b_gate, gla_norm, ssd_conv_w, ssd_conv_b,
          ssd_dt_bias, ssd_a_log, ssd_d, ssd_norm, l1_w_in, l1_w_out, lru_conv_w, lru_conv_b, lru_w_a, lru_b_a,
          lru_w_x, lru_b_x, lru_lambda, gdn_conv_w, gdn_a_log, gdn_dt_bias, gdn_norm):
    w = {}
    w["ffn_in"] = ffn_w_in.astype(BF16)
    w["ffn_out"] = ffn_w_out.astype(BF16)
    c = l0_w_in
    small0 = jnp.concatenate([c[:, 8208:8240], c[:, 3072:3088], jnp.zeros((D_MODEL, LANES - 48), F32)], axis=1)
    w["l0_in"] = jnp.concatenate(
        [c[:, 1024:2048], c[:, 2048:3072], c[:, 0:512], c[:, 512:1024],
         c[:, 3088:5136], c[:, 5136:7184], c[:, 7184:7696], c[:, 7696:8208],
         small0], axis=1).astype(BF16)
    w["l0_out"] = l0_w_out.astype(BF16)
    w["gla_wg2"] = jnp.zeros((LANES, 512), F32).at[32:48].set(gla_w_gate2).astype(BF16)
    w["gla_bg"] = gla_b_gate.reshape(1, 512)
    w["gla_norm"] = gla_norm.reshape(1, GLA_DV)
    w["ssd_cw"] = ssd_conv_w
    w["ssd_cb"] = ssd_conv_b.reshape(1, -1)
    w["ssd_dtb"] = _pad_lanes(ssd_dt_bias, LANES)
    w["ssd_alog"] = _pad_lanes(ssd_a_log, LANES)
    head_of_lane = jnp.arange(SSD_D_INNER) // SSD_HEAD_DIM
    emat = (jnp.arange(LANES)[:, None] == head_of_lane[None, :])
    w["ssd_e"] = emat.astype(BF16)
    w["ssd_et"] = emat.T.astype(BF16)
    w["ssd_dvec"] = jnp.repeat(ssd_d, SSD_HEAD_DIM).reshape(1, SSD_D_INNER)
    w["ssd_norm"] = ssd_norm.reshape(1, SSD_D_INNER)
    c = l1_w_in
    small1 = jnp.concatenate([c[:, 6656:6672], jnp.zeros((D_MODEL, LANES - 16), F32)], axis=1)
    w["l1_in"] = jnp.concatenate(
        [c[:, 2560:3584], c[:, 3584:4608], c[:, 4608:5632], c[:, 5632:6656],
         c[:, 0:1280], c[:, 1280:2560],
         small1], axis=1).astype(BF16)
    w["l1_out"] = l1_w_out.astype(BF16)
    w["lru_cw"] = lru_conv_w
    w["lru_cb"] = lru_conv_b.reshape(1, -1)
    w["lru_wa"] = lru_w_a.astype(BF16)
    w["lru_ba"] = lru_b_a.reshape(1, -1)
    w["lru_wx"] = lru_w_x.astype(BF16)
    w["lru_bx"] = lru_b_x.reshape(1, -1)
    w["lru_lam"] = lru_lambda.reshape(1, -1)
    w["gdn_cw"] = gdn_conv_w
    w["gdn_alog"] = _pad_lanes(gdn_a_log, LANES)
    w["gdn_dtb"] = _pad_lanes(gdn_dt_bias, LANES)
    w["gdn_norm"] = gdn_norm.reshape(1, GDN_DV)
    return w


L0_WIDTHS = (GLA_W, SSD_W, LANES)
L1_WIDTHS = (GDN_W, LRU_W2, LANES)


def _trunk_prompt(x, norms, final_norm, w, *, tm, cs):
    nb, l = x.shape[0], x.shape[1]
    t = nb * l
    h = x.reshape(t, D_MODEL)
    h = _ffn(h, norms[0, 0], w["ffn_in"], w["ffn_out"], (0, 0), tm=tm)
    xa, xb, sm = _proj(h, norms[0, 1], w["l0_in"], L0_WIDTHS, (BF16, BF16, F32), tm=tm)
    sm = sm.reshape(nb, l, LANES)
    o_a, s_gla = _gla_prompt(xa.reshape(nb, l, GLA_W), sm, w["gla_wg2"], w["gla_bg"], w["gla_norm"], lg=cs)
    y, s_ssd, cx, cb, cc = _ssd_prompt(xb.reshape(nb, l, SSD_W), sm, w["ssd_cw"], w["ssd_cb"], w["ssd_dtb"],
                                       w["ssd_alog"], w["ssd_e"], w["ssd_et"], w["ssd_dvec"], w["ssd_norm"], cs=cs)
    h = _ffn(h, norms[0, 2], w["ffn_in"], w["ffn_out"], (0, 1),
             pre=(o_a.reshape(t, 1024), y.reshape(t, 2048), w["l0_out"]), tm=tm)
    h = _ffn(h, norms[1, 0], w["ffn_in"], w["ffn_out"], (1, 0), tm=tm)
    xg, xl, sm = _proj(h, norms[1, 1], w["l1_in"], L1_WIDTHS, (BF16, BF16, F32), tm=tm)
    sm = sm.reshape(nb, l, LANES)
    o_c, s_lru, s_lru_conv = _lru_prompt(xl.reshape(nb, l, LRU_W2), w["lru_cw"], w["lru_cb"], w["lru_wa"],
                                         w["lru_ba"], w["lru_wx"], w["lru_bx"], w["lru_lam"], cs=cs)
    o_d, s_gdn, cq, ck, cv = _gdn_prompt(xg.reshape(nb, l, GDN_W), sm, w["gdn_cw"], w["gdn_alog"], w["gdn_dtb"],
                                         w["gdn_norm"], cs=cs)
    h = _ffn(h, norms[1, 2], w["ffn_in"], w["ffn_out"], (1, 1), final_norm,
             pre=(o_c.reshape(t, LRU_WIDTH), o_d.reshape(t, 1024), w["l1_out"]), tm=tm)
    return (h.reshape(nb, l, D_MODEL), s_gla, s_ssd.reshape(nb, SSD_HEADS, SSD_HEAD_DIM, SSD_D_STATE),
            jnp.concatenate([cx, cb, cc], axis=-1), s_lru.reshape(nb, LRU_WIDTH), s_lru_conv, s_gdn,
            jnp.concatenate([cq, ck, cv], axis=-1))


def _conv_step(cst_ref, cst_out, cur, w_ref, bias_ref, lo, hi):
    acc = w_ref[CONV_W - 1:CONV_W, lo:hi] * cur
    for j in range(CONV_W - 1):
        acc = acc + w_ref[j:j + 1, lo:hi] * cst_ref[:, j, lo:hi]
    if bias_ref is not None:
        acc = acc + bias_ref[:, lo:hi]
    for j in range(CONV_W - 2):
        cst_out[:, j, lo:hi] = cst_ref[:, j + 1, lo:hi]
    cst_out[:, CONV_W - 2, lo:hi] = cur
    return acc


def _gla_dec_body(q_ref, k_ref, v_ref, gg_ref, sm_ref, wg2_ref, bg_ref, nw_ref, s_in, o_ref, s_out, o_scr, *, nb):
    pre = _dot(sm_ref[...].astype(BF16), wg2_ref[...]) + bg_ref[...]
    a = jnp.exp(-_softplus(-pre) * (1.0 / GLA_GATE_NORM))
    q = q_ref[...] * (GLA_DK ** -0.5)
    v = v_ref[...]
    a_t, q_t, k_t = a.T, q.T, k_ref[...].T
    for b in range(nb):
        for h in range(GLA_HEADS):
            dk = slice(h * GLA_DK, (h + 1) * GLA_DK)
            dv = slice(h * GLA_DV, (h + 1) * GLA_DV)
            sn = a_t[dk, b:b + 1] * s_in[b, h] + k_t[dk, b:b + 1] * v[b:b + 1, dv]
            s_out[b, h] = sn
            o_scr[b:b + 1, dv] = jnp.sum(q_t[dk, b:b + 1] * sn, axis=0, keepdims=True)
    o = o_scr[...]
    gg = gg_ref[...]
    for h in range(GLA_HEADS):
        dv = slice(h * GLA_DV, (h + 1) * GLA_DV)
        o_ref[:, dv] = _rms(o[:, dv], nw_ref[...]) * _silu(gg[:, dv])


def _gla_step(xa, sm, state, wg2, bgate, normw, *, nb):
    n = xa.shape[0]

    def seg(width, off):
        return pl.BlockSpec((nb, width), lambda i: (i, off // width))

    def full(shape):
        return pl.BlockSpec(shape, lambda i: (0,) * len(shape))

    st = pl.BlockSpec((nb, GLA_HEADS, GLA_DK, GLA_DV), lambda i: (i, 0, 0, 0))
    return pl.pallas_call(
        functools.partial(_gla_dec_body, nb=nb),
        grid=(n // nb,),
        in_specs=[seg(512, GLA_GQ), seg(512, GLA_GK), seg(1024, GLA_GV), seg(1024, GLA_GG), seg(LANES, 0),
                  full((LANES, 512)), full((1, 512)), full((1, GLA_DV)), st],
        out_specs=[pl.BlockSpec((nb, 1024), lambda i: (i, 0)), st],
        out_shape=[jax.ShapeDtypeStruct((n, 1024), F32), jax.ShapeDtypeStruct(state.shape, F32)],
        scratch_shapes=[pltpu.VMEM((nb, 1024), F32)],
        compiler_params=_params(("parallel",)),
        name="gla_step",
    )(xa, xa, xa, xa, sm, wg2, bgate, normw, state)


def _ssd_dec_body(z_ref, x_ref, b_ref, c_ref, sm_ref, cst_ref, cw, cbias, dtb_ref, alog_ref, e_ref, dvec_ref,
                  nw_ref, s_in, y_ref, s_out, cst_out, yt_scr, *, nb):
    xs = _silu(_conv_step(cst_ref, cst_out, x_ref[...], cw, cbias, 0, 2048))
    bm = _silu(_conv_step(cst_ref, cst_out, b_ref[...], cw, cbias, 2048, 2560))
    cm = _silu(_conv_step(cst_ref, cst_out, c_ref[...], cw, cbias, 2560, 3072)).astype(BF16)
    dt = _softplus(sm_ref[...] + dtb_ref[...])
    decay = jnp.exp(dt * (-jnp.exp(alog_ref[...])))
    dtx = (_dot_x01(dt, e_ref[...]) * xs).astype(BF16)
    lane = lax.broadcasted_iota(jnp.int32, (SSD_GROUP_W, nb), 1)
    seq = lax.broadcasted_iota(jnp.int32, (nb, 1), 0)
    tile = (SSD_HEAD_DIM, SSD_D_STATE)
    for g in range(SSD_GROUPS):
        gn = slice(g * SSD_D_STATE, (g + 1) * SSD_D_STATE)
        gw = slice(g * SSD_GROUP_W, (g + 1) * SSD_GROUP_W)
        rows = pl.ds(g * SSD_GROUP_W, SSD_GROUP_W)
        ycol = jnp.zeros((SSD_GROUP_W, nb), F32)
        for b in range(nb):
            b_only = jnp.where(seq == b, bm[:, gn], 0.0).astype(BF16)
            dec = jnp.concatenate([jnp.broadcast_to(decay[b:b + 1, h:h + 1], tile)
                                   for h in range(g * SSD_HPG, (g + 1) * SSD_HPG)], axis=0)
            sn = dec * s_in[b, rows, :] + _dot_tn(dtx[:, gw], b_only)
            s_out[b, rows, :] = sn
            ycol = jnp.where(lane == b, _dot_nt(sn.astype(BF16), cm[:, gn]), ycol)
        yt_scr[rows, :] = ycol
    y = yt_scr[...].T + dvec_ref[...] * xs
    y = y * _silu(z_ref[...])
    for g in range(SSD_GROUPS):
        gw = slice(g * SSD_GROUP_W, (g + 1) * SSD_GROUP_W)
        y_ref[:, gw] = _rms(y[:, gw], nw_ref[:, gw])


def _ssd_step(xb, sm, state, cstate, cw, cb, dtb, alog, emat, dvec, normw, *, nb):
    n = xb.shape[0]

    def seg(width, off):
        return pl.BlockSpec((nb, width), lambda i: (i, off // width))

    def full(shape):
        return pl.BlockSpec(shape, lambda i: (0,) * len(shape))

    st = pl.BlockSpec((nb, SSD_D_INNER, SSD_D_STATE), lambda i: (i, 0, 0))
    cst = pl.BlockSpec((nb, CONV_W - 1, 3072), lambda i: (i, 0, 0))
    return pl.pallas_call(
        functools.partial(_ssd_dec_body, nb=nb),
        grid=(n // nb,),
        in_specs=[seg(2048, SSD_Z), seg(2048, SSD_X), seg(512, SSD_B), seg(512, SSD_C), seg(LANES, 0), cst,
                  full((CONV_W, 3072)), full((1, 3072)), full((1, LANES)), full((1, LANES)), full((LANES, 2048)),
                  full((1, 2048)), full((1, 2048)), st],
        out_specs=[pl.BlockSpec((nb, 2048), lambda i: (i, 0)), st, cst],
        out_shape=[jax.ShapeDtypeStruct((n, 2048), F32), jax.ShapeDtypeStruct(state.shape, F32),
                   jax.ShapeDtypeStruct(cstate.shape, F32)],
        scratch_shapes=[pltpu.VMEM((SSD_D_INNER, nb), F32)],
        compiler_params=_params(("parallel",)),
        name="ssd_step",
    )(xb, xb, xb, xb, sm, cstate, cw, cb, dtb, alog, emat, dvec, normw, state)


def _lru_dec_body(rg_ref, rx_ref, cst_ref, cw, cbias, wa_ref, ba_ref, wx_ref, bx_ref, lam_ref, h_in,
                  o_ref, h_out, cst_out):
    xc = _conv_step(cst_ref, cst_out, rx_ref[...], cw, cbias, 0, LRU_WIDTH)
    a, b = _lru_gates(xc, wa_ref, ba_ref, wx_ref, bx_ref, lam_ref)
    h = a * h_in[...] + b
    h_out[...] = h
    o_ref[...] = _gelu_tanh(rg_ref[...]) * h


def _lru_step(xl, hstate, cstate, cw, cbias, wa, ba, wx, bx, lam):
    n = xl.shape[0]

    def full(shape):
        return pl.BlockSpec(shape, lambda i: (0,) * len(shape))

    return pl.pallas_call(
        _lru_dec_body,
        grid=(1,),
        in_specs=[pl.BlockSpec((n, LRU_WIDTH), lambda i: (0, LRU_RG // LRU_WIDTH)),
                  pl.BlockSpec((n, LRU_WIDTH), lambda i: (0, LRU_RX // LRU_WIDTH)),
                  full((n, CONV_W - 1, LRU_WIDTH)), full((CONV_W, LRU_WIDTH)), full((1, LRU_WIDTH)),
                  full((LRU_BLOCKS, LRU_BLOCK_W, LRU_BLOCK_W)), full((1, LRU_WIDTH)),
                  full((LRU_BLOCKS, LRU_BLOCK_W, LRU_BLOCK_W)), full((1, LRU_WIDTH)), full((1, LRU_WIDTH)),
                  full((n, LRU_WIDTH))],
        out_specs=[full((n, LRU_WIDTH)), full((n, LRU_WIDTH)), full((n, CONV_W - 1, LRU_WIDTH))],
        out_shape=[jax.ShapeDtypeStruct((n, LRU_WIDTH), F32), jax.ShapeDtypeStruct((n, LRU_WIDTH), F32),
                   jax.ShapeDtypeStruct(cstate.shape, F32)],
        compiler_params=_params(("arbitrary",)),
        name="lru_step",
    )(xl, xl, cstate, cw, cbias, wa, ba, wx, bx, lam, hstate)


def _gdn_dec_body(q_ref, k_ref, v_ref, z_ref, sm_ref, cst_ref, cw, alog_ref, dtb_ref, nw_ref, s_in,
                  o_ref, s_out, cst_out, o_scr, *, nb):
    q = _silu(_conv_step(cst_ref, cst_out, q_ref[...], cw, None, 0, 1024))
    k = _silu(_conv_step(cst_ref, cst_out, k_ref[...], cw, None, 1024, 2048))
    v = _silu(_conv_step(cst_ref, cst_out, v_ref[...], cw, None, 2048, 3072))
    sm = sm_ref[...]
    eg_all = jnp.exp(-jnp.exp(alog_ref[...]) * _softplus(sm + dtb_ref[...]))
    beta_all = _sigmoid(sm)
    qn = jnp.concatenate([_l2norm(q[:, h * GDN_DK:(h + 1) * GDN_DK]) for h in range(GDN_HEADS)], axis=-1)
    qn = qn * (GDN_DK ** -0.5)
    kn = jnp.concatenate([_l2norm(k[:, h * GDN_DK:(h + 1) * GDN_DK]) for h in range(GDN_HEADS)], axis=-1)
    kq = jnp.concatenate([kn, qn], axis=0).astype(BF16)
    seq = lax.broadcasted_iota(jnp.int32, (nb, 1), 0)
    pairs = [(b, h) for b in range(nb) for h in range(GDN_HEADS)]
    ksqs = [_dot(kq[:, h * GDN_DK:(h + 1) * GDN_DK], s_in[b, h].astype(BF16)) for b, h in pairs]
    for (b, h), kq_s in zip(pairs, ksqs):
        hd = slice(h * GDN_DK, (h + 1) * GDN_DK)
        eg = eg_all[b:b + 1, h:h + 1]
        beta = beta_all[b:b + 1, GDN_HEADS + h:GDN_HEADS + h + 1]
        delta = beta * (v[b:b + 1, hd] - eg * kq_s[b:b + 1])
        qk = jnp.sum(qn[b:b + 1, hd] * kn[b:b + 1, hd], axis=-1, keepdims=True)
        o_scr[b:b + 1, hd] = eg * kq_s[nb + b:nb + b + 1] + qk * delta
        d_only = jnp.where(seq == b, jnp.broadcast_to(delta, (nb, GDN_DV)), 0.0).astype(BF16)
        s_out[b, h] = eg * s_in[b, h] + _dot_tn(kq[:nb, hd], d_only)
    o = o_scr[...]
    z = z_ref[...]
    for h in range(GDN_HEADS):
        hd = slice(h * GDN_DK, (h + 1) * GDN_DK)
        o_ref[:, hd] = _rms(o[:, hd], nw_ref[...]) * _silu(z[:, hd])


def _gdn_step(xg, sm, state, cstate, cw, alog, dtb, normw, *, nb):
    n = xg.shape[0]

    def seg(width, off):
        return pl.BlockSpec((nb, width), lambda i: (i, off // width))

    def full(shape):
        return pl.BlockSpec(shape, lambda i: (0,) * len(shape))

    st = pl.BlockSpec((nb, GDN_HEADS, GDN_DK, GDN_DV), lambda i: (i, 0, 0, 0))
    cst = pl.BlockSpec((nb, CONV_W - 1, 3072), lambda i: (i, 0, 0))
    return pl.pallas_call(
        functools.partial(_gdn_dec_body, nb=nb),
        grid=(n // nb,),
        in_specs=[seg(1024, GDN_Q), seg(1024, GDN_K), seg(1024, GDN_V), seg(1024, GDN_Z), seg(LANES, 0), cst,
                  full((CONV_W, 3072)), full((1, LANES)), full((1, LANES)), full((1, GDN_DV)), st],
        out_specs=[pl.BlockSpec((nb, 1024), lambda i: (i, 0)), st, cst],
        out_shape=[jax.ShapeDtypeStruct((n, 1024), F32), jax.ShapeDtypeStruct(state.shape, F32),
                   jax.ShapeDtypeStruct(cstate.shape, F32)],
        scratch_shapes=[pltpu.VMEM((nb, 1024), F32)],
        compiler_params=_params(("parallel",)),
        name="gdn_step",
    )(xg, xg, xg, xg, sm, cstate, cw, alog, dtb, normw, state)


def _trunk_sample(x, s_gla, s_ssd, s_ssd_conv, s_lru, s_lru_conv, s_gdn, s_gdn_conv, norms, final_norm, w, *, nb=8):
    n = x.shape[0]
    f3 = (F32, F32, F32)
    h = x.reshape(n, D_MODEL)
    h = _ffn(h, norms[0, 0], w["ffn_in"], w["ffn_out"], (0, 0), tm=n)
    xa, xb, sm = _proj(h, norms[0, 1], w["l0_in"], L0_WIDTHS, f3, tm=n)
    o_a, n_gla = _gla_step(xa, sm, s_gla, w["gla_wg2"], w["gla_bg"], w["gla_norm"], nb=nb)
    y, n_ssd, n_ssd_conv = _ssd_step(xb, sm, s_ssd.reshape(n, SSD_D_INNER, SSD_D_STATE), s_ssd_conv, w["ssd_cw"],
                                     w["ssd_cb"], w["ssd_dtb"], w["ssd_alog"], w["ssd_e"], w["ssd_dvec"],
                                     w["ssd_norm"], nb=nb)
    h = _ffn(h, norms[0, 2], w["ffn_in"], w["ffn_out"], (0, 1), pre=(o_a, y, w["l0_out"]), tm=n)
    h = _ffn(h, norms[1, 0], w["ffn_in"], w["ffn_out"], (1, 0), tm=n)
    xg, xl, sm = _proj(h, norms[1, 1], w["l1_in"], L1_WIDTHS, f3, tm=n)
    o_c, n_lru, n_lru_conv = _lru_step(xl, s_lru, s_lru_conv, w["lru_cw"], w["lru_cb"], w["lru_wa"], w["lru_ba"],
                                       w["lru_wx"], w["lru_bx"], w["lru_lam"])
    o_d, n_gdn, n_gdn_conv = _gdn_step(xg, sm, s_gdn, s_gdn_conv, w["gdn_cw"], w["gdn_alog"], w["gdn_dtb"],
                                       w["gdn_norm"], nb=nb)
    h = _ffn(h, norms[1, 2], w["ffn_in"], w["ffn_out"], (1, 1), final_norm,
             pre=(o_c, o_d, w["l1_out"]), tm=n)
    return (h.reshape(n, 1, D_MODEL), n_gla, n_ssd.reshape(s_ssd.shape), n_ssd_conv, n_lru, n_lru_conv, n_gdn,
            n_gdn_conv)


def kernel(x_prompt, x_sample, state_gla, state_ssd, state_ssd_conv, state_lru, state_lru_conv, state_gdn,
           state_gdn_conv, norms, final_norm, ffn_w_in, ffn_w_out, l0_w_in, l0_w_out, gla_w_gate2, gla_b_gate,
           gla_norm, ssd_conv_w, ssd_conv_b, ssd_dt_bias, ssd_a_log, ssd_d, ssd_norm, l1_w_in, l1_w_out,
           lru_conv_w, lru_conv_b, lru_w_a, lru_b_a, lru_w_x, lru_b_x, lru_lambda, gdn_conv_w, gdn_a_log,
           gdn_dt_bias, gdn_norm):
    w = _prep(ffn_w_in, ffn_w_out, l0_w_in, l0_w_out, gla_w_gate2, gla_b_gate, gla_norm, ssd_conv_w, ssd_conv_b,
              ssd_dt_bias, ssd_a_log, ssd_d, ssd_norm, l1_w_in, l1_w_out, lru_conv_w, lru_conv_b, lru_w_a, lru_b_a,
              lru_w_x, lru_b_x, lru_lambda, gdn_conv_w, gdn_a_log, gdn_dt_bias, gdn_norm)
    prompt = _trunk_prompt(x_prompt, norms, final_norm, w, tm=512, cs=128)
    sample = _trunk_sample(x_sample, state_gla, state_ssd, state_ssd_conv, state_lru, state_lru_conv, state_gdn,
                           state_gdn_conv, norms, final_norm, w)
    return (prompt[0], sample[0]) + tuple(prompt[1:]) + tuple(sample[1:])
```

```python
import functools
import math

import jax
import jax.numpy as jnp
from jax import lax
from jax.experimental import pallas as pl
from jax.experimental.pallas import tpu as pltpu

F32 = jnp.float32
BF16 = jnp.bfloat16

D_MODEL = 1024
NORM_EPS = 1e-6
CONV_W = 4
D_FF = 2816
FFN_RES = 0.5

GLA_HEADS = 4
GLA_DK = 128
GLA_DV = 256
GLA_RANK = 16
GLA_GATE_NORM = 16.0
GLA_SUB = 16
GLA_FAST_MAX = 60.0

SSD_D_INNER = 2048
SSD_HEAD_DIM = 64
SSD_HEADS = 32
SSD_GROUPS = 4
SSD_HPG = 8
SSD_D_STATE = 128
SSD_GROUP_W = SSD_HPG * SSD_HEAD_DIM

LRU_WIDTH = 1280
LRU_BLOCKS = 10
LRU_BLOCK_W = 128
LRU_C = 8.0

GDN_HEADS = 8
GDN_DK = 128
GDN_DV = 128

LANES = 128
SUBLANES = 8
NEG_BIG = -1e30
VMEM_LIMIT = 56 * 1024 * 1024

GLA_W = 3072
GLA_GV, GLA_GG, GLA_GQ, GLA_GK = 0, 1024, 2048, 2560
SSD_W = 5120
SSD_Z, SSD_X, SSD_B, SSD_C = 0, 2048, 4096, 4608
GDN_W = 4096
GDN_Q, GDN_K, GDN_V, GDN_Z = 0, 1024, 2048, 3072
LRU_W2 = 2560
LRU_RG, LRU_RX = 0, 1280


def _sigmoid(x):
    return 0.5 * jnp.tanh(0.5 * x) + 0.5


def _silu(x):
    return _silu_of_half(0.5 * x)


def _silu_of_half(hx):
    return hx * jnp.tanh(hx) + hx


def _softplus(x):
    return jnp.maximum(x, 0.0) + jnp.log(1.0 + jnp.exp(-jnp.abs(x)))


def _rms(x, g):
    return x * lax.rsqrt(jnp.mean(x * x, axis=-1, keepdims=True) + NORM_EPS) * g


def _dot(a, b):
    return jnp.dot(a, b, preferred_element_type=F32)


def _dot_nt(a, b):
    return lax.dot_general(a, b, (((1,), (1,)), ((), ())), preferred_element_type=F32)


def _dot_tn(a, b):
    return lax.dot_general(a, b, (((0,), (0,)), ((), ())), preferred_element_type=F32)


def _split2(x):
    hi = x.astype(BF16)
    lo = (x - hi.astype(F32)).astype(BF16)
    return hi, lo


def _split3(x):
    hi = x.astype(BF16)
    r = x - hi.astype(F32)
    mid = r.astype(BF16)
    lo = (r - mid.astype(F32)).astype(BF16)
    return hi, mid, lo


def _dot01_exact(m01, x):
    hi, mid, lo = _split3(x)
    return _dot(m01, hi) + _dot(m01, mid) + _dot(m01, lo)


def _dot_x01(x, m01):
    hi, lo = _split2(x)
    return _dot(hi, m01) + _dot(lo, m01)


def _lower_mask(n, strict=False):
    row = lax.broadcasted_iota(jnp.int32, (n, n), 0)
    col = lax.broadcasted_iota(jnp.int32, (n, n), 1)
    return row > col if strict else row >= col


def _params(sem):
    return pltpu.CompilerParams(dimension_semantics=sem, vmem_limit_bytes=VMEM_LIMIT)


FFN_CHUNK = 256


def _resident(shape):
    return pl.BlockSpec(shape, lambda i: (0,) * len(shape), pipeline_mode=pl.Buffered(1))


def _ffn_body(*refs, final, pre):
    refs = list(refs)
    x_ref = refs.pop(0)
    if pre:
        a_ref, b_ref, wab_ref = refs[:3]
        refs = refs[3:]
    g_ref, wi_ref, wo_ref = refs[:3]
    refs = refs[3:]
    fg_ref = refs.pop(0) if final else None
    o_ref, act_ref = refs
    x = x_ref[...]
    if pre:
        ka = a_ref.shape[1]
        x = x + _dot(a_ref[...].astype(BF16), wab_ref[:ka, :]) + _dot(b_ref[...].astype(BF16), wab_ref[ka:, :])
    hn = _rms(x, g_ref[...]).astype(BF16)
    for c in range(D_FF // FFN_CHUNK):
        cols = slice(c * FFN_CHUNK, (c + 1) * FFN_CHUNK)
        gate = _dot(hn, wi_ref[:, cols])
        up = _dot(hn, wi_ref[:, D_FF + c * FFN_CHUNK:D_FF + (c + 1) * FFN_CHUNK])
        act_ref[:, cols] = (_silu(gate) * up).astype(BF16)
    y = x + FFN_RES * _dot(act_ref[...], wo_ref[...])
    if final:
        y = _rms(y, fg_ref[...])
    o_ref[...] = y


def _ffn(x, g, w_in, w_out, sel, final_g=None, pre=None, *, tm):
    t = x.shape[0]
    final = final_g is not None
    in_specs = [pl.BlockSpec((tm, D_MODEL), lambda i: (i, 0))]
    args = [x]
    if pre is not None:
        a, b, wab = pre
        in_specs += [pl.BlockSpec((tm, a.shape[1]), lambda i: (i, 0)), pl.BlockSpec((tm, b.shape[1]), lambda i: (i, 0)),
                     _resident(wab.shape)]
        args += [a, b, wab]
    in_specs += [_resident((1, D_MODEL)),
                 pl.BlockSpec((None, None) + w_in.shape[2:], lambda i: sel + (0, 0), pipeline_mode=pl.Buffered(1)),
                 pl.BlockSpec((None, None) + w_out.shape[2:], lambda i: sel + (0, 0), pipeline_mode=pl.Buffered(1))]
    args += [g.reshape(1, D_MODEL), w_in, w_out]
    if final:
        in_specs.append(_resident((1, D_MODEL)))
        args.append(final_g.reshape(1, D_MODEL))
    return pl.pallas_call(
        functools.partial(_ffn_body, final=final, pre=pre is not None),
        grid=(t // tm,),
        in_specs=in_specs,
        out_specs=pl.BlockSpec((tm, D_MODEL), lambda i: (i, 0)),
        out_shape=jax.ShapeDtypeStruct((t, D_MODEL), F32),
        scratch_shapes=[pltpu.VMEM((tm, D_FF), BF16)],
        compiler_params=_params(("parallel",)),
        name="ffn",
    )(*args)


PROJ_CHUNK = 1024


def _proj_body(x_ref, g_ref, w_ref, *o_refs, widths):
    hn = _rms(x_ref[...], g_ref[...]).astype(BF16)
    off = 0
    for o_ref, width in zip(o_refs, widths):
        for c in range(0, width, PROJ_CHUNK):
            n = min(PROJ_CHUNK, width - c)
            o_ref[:, c:c + n] = _dot(hn, w_ref[:, off + c:off + c + n]).astype(o_ref.dtype)
        off += width


def _proj(x, g, w, widths, dtypes, *, tm):
    t = x.shape[0]
    return pl.pallas_call(
        functools.partial(_proj_body, widths=tuple(widths)),
        grid=(t // tm,),
        in_specs=[pl.BlockSpec((tm, D_MODEL), lambda i: (i, 0)), _resident((1, D_MODEL)), _resident(w.shape)],
        out_specs=[pl.BlockSpec((tm, n), lambda i: (i, 0)) for n in widths],
        out_shape=[jax.ShapeDtypeStruct((t, n), dt) for n, dt in zip(widths, dtypes)],
        compiler_params=_params(("parallel",)),
        name="proj",
    )(x, g.reshape(1, D_MODEL), w)


def _shift_matrix(c):
    out_row = jnp.arange(3 * c)
    k = out_row // c + 1
    src = out_row % c - k
    return (src[:, None] == jnp.arange(c)[None, :]).astype(BF16)


def _conv_chunk(tail, src, shift_ref, w_ref, bias_ref, c):
    x = src.astype(F32)
    sh = _dot(shift_ref[...], src)
    w0, w1, w2, w3 = (0.5 * w_ref[j:j + 1, :] for j in range(CONV_W))
    acc = w3 * x + w2 * sh[:c] + w1 * sh[c:2 * c] + w0 * sh[2 * c:]
    if bias_ref is not None:
        acc = acc + 0.5 * bias_ref[...]
    t3, t2, t1 = tail[5:6, :], tail[6:7, :], tail[7:8, :]
    row = lax.broadcasted_iota(jnp.int32, (SUBLANES, x.shape[1]), 0)
    head = jnp.where(row == 0, w0 * t3 + w1 * t2 + w2 * t1,
                     jnp.where(row == 1, w0 * t2 + w1 * t1, jnp.where(row == 2, w0 * t1, 0.0)))
    tail[pl.ds(5, 3), :] = x[c - 3:c, :]
    return jnp.concatenate([acc[:SUBLANES] + head, acc[SUBLANES:]], axis=0)


def _conv_state(tail):
    return tail[pl.ds(5, 3), :]


def _gla_body(q_in, k_in, v_in, gg_ref, sm_ref, wg2_ref, bg_ref, nw_ref, o_ref, s_out, st_ref, b_ref,
              q_ref, k_ref, v_ref, *, lg, nc):
    c = pl.program_id(1)

    @pl.when(c == 0)
    def _():
        st_ref[...] = jnp.zeros(st_ref.shape, F32)

    pre = _dot(sm_ref[0].astype(BF16), wg2_ref[...]) + bg_ref[...]
    log_a = -_softplus(-pre) * (1.0 / GLA_GATE_NORM)
    lower = _lower_mask(lg)
    b_all = _dot01_exact(jnp.where(lower, 1.0, 0.0).astype(BF16), log_a)
    b_ref[...] = b_all
    fast = jnp.max(-b_all[lg - 1:lg, :]) <= GLA_FAST_MAX

    q = q_in[0].astype(F32) * (GLA_DK ** -0.5)
    k = k_in[0].astype(F32)
    vb = v_in[0]
    blast = b_all[lg - 1:lg, :]
    qe = (q * jnp.exp(b_all)).astype(BF16)
    ke = (k * jnp.exp(-b_all)).astype(BF16)
    kd = (k * jnp.exp(blast - b_all)).astype(BF16)
    elast = jnp.exp(blast)
    gg = gg_ref[0].astype(F32)
    for h in range(GLA_HEADS):
        dk = slice(h * GLA_DK, (h + 1) * GLA_DK)
        dv = slice(h * GLA_DV, (h + 1) * GLA_DV)
        att = jnp.where(lower, _dot_nt(qe[:, dk], ke[:, dk]), 0.0).astype(BF16)
        st = st_ref[h]
        oh = _dot(att, vb[:, dv]) + _dot_nt(qe[:, dk], st.astype(BF16))
        st_ref[h] = jnp.where(fast, st * elast[:, dk] + _dot_tn(vb[:, dv], kd[:, dk]), st)
        o_ref[0, :, dv] = (_rms(oh, nw_ref[...]) * _silu(gg[:, dv])).astype(BF16)

    @pl.when(jnp.logical_not(fast))
    def _():
        q_ref[...] = q_in[0].astype(F32) * (GLA_DK ** -0.5)
        k_ref[...] = k_in[0].astype(F32)
        v_ref[...] = v_in[0].astype(F32)
        rowi = lax.broadcasted_iota(jnp.int32, (GLA_SUB, 1), 0)

        def block(i, carry):
            base = pl.multiple_of(i * GLA_SUB, GLA_SUB)
            q = q_ref[pl.ds(base, GLA_SUB), :]
            k = k_ref[pl.ds(base, GLA_SUB), :]
            v = v_ref[pl.ds(base, GLA_SUB), :]
            b = b_ref[pl.ds(base, GLA_SUB), :]
            o = [jnp.zeros((GLA_SUB, GLA_DV), F32) for _ in range(GLA_HEADS)]
            for s in range(GLA_SUB):
                ks = k_ref[pl.ds(base + s, 1), :]
                bs = b_ref[pl.ds(base + s, 1), :]
                vs = v_ref[pl.ds(base + s, 1), :]
                p = q * ks * jnp.exp(b - bs)
                for h in range(GLA_HEADS):
                    att = jnp.sum(p[:, h * GLA_DK:(h + 1) * GLA_DK], axis=-1, keepdims=True)
                    att = jnp.where(rowi >= s, att, 0.0)
                    o[h] = o[h] + att * vs[:, h * GLA_DV:(h + 1) * GLA_DV]
            bprev = jnp.where(i > 0, b_ref[pl.ds(jnp.maximum(base - 1, 0), 1), :], 0.0)
            blast = b_ref[pl.ds(base + GLA_SUB - 1, 1), :]
            qe = (q * jnp.exp(b - bprev)).astype(BF16)
            kd = (k * jnp.exp(blast - b)).astype(BF16)
            elast = jnp.exp(blast - bprev)
            vb = v.astype(BF16)
            gg = gg_ref[0, pl.ds(base, GLA_SUB), :].astype(F32)
            for h in range(GLA_HEADS):
                dk = slice(h * GLA_DK, (h + 1) * GLA_DK)
                dv = slice(h * GLA_DV, (h + 1) * GLA_DV)
                st = st_ref[h]
                oh = o[h] + _dot_nt(qe[:, dk], st.astype(BF16))
                st_ref[h] = st * elast[:, dk] + _dot_tn(vb[:, dv], kd[:, dk])
                o_ref[0, pl.ds(base, GLA_SUB), dv] = (_rms(oh, nw_ref[...]) * _silu(gg[:, dv])).astype(BF16)
            return carry

        lax.fori_loop(0, lg // GLA_SUB, block, 0)

    @pl.when(c == nc - 1)
    def _():
        for h in range(GLA_HEADS):
            s_out[0, h] = st_ref[h].T


def _gla_prompt(xa, sm, wg2, bgate, normw, *, lg):
    nb, l = xa.shape[0], xa.shape[1]
    nc = l // lg

    def seg(width, off):
        return pl.BlockSpec((1, lg, width), lambda b, c: (b, c, off // width))

    def full(shape):
        return pl.BlockSpec(shape, lambda b, c: (0,) * len(shape))

    return pl.pallas_call(
        functools.partial(_gla_body, lg=lg, nc=nc),
        grid=(nb, nc),
        in_specs=[seg(512, GLA_GQ), seg(512, GLA_GK), seg(1024, GLA_GV), seg(1024, GLA_GG),
                  pl.BlockSpec((1, lg, LANES), lambda b, c: (b, c, 0)),
                  full((LANES, 512)), full((1, 512)), full((1, GLA_DV))],
        out_specs=[pl.BlockSpec((1, lg, 1024), lambda b, c: (b, c, 0)),
                   pl.BlockSpec((1, GLA_HEADS, GLA_DK, GLA_DV), lambda b, c: (b, 0, 0, 0))],
        out_shape=[jax.ShapeDtypeStruct((nb, l, 1024), BF16),
                   jax.ShapeDtypeStruct((nb, GLA_HEADS, GLA_DK, GLA_DV), F32)],
        scratch_shapes=[pltpu.VMEM((GLA_HEADS, GLA_DV, GLA_DK), F32), pltpu.VMEM((lg, 512), F32),
                        pltpu.VMEM((lg, 512), F32), pltpu.VMEM((lg, 512), F32), pltpu.VMEM((lg, 1024), F32)],
        compiler_params=_params(("parallel", "arbitrary")),
        name="gla_prompt",
    )(xa, xa, xa, xa, sm, wg2, bgate, normw)


def _ssd_body(z_ref, x_ref, b_ref, c_ref, sm_ref, shift_ref, cwx, cbx, cwb, cbb, cwc, cbc, dtb_ref, alog_ref, e_ref, et_ref,
              dvec_ref, nw_ref, y_ref, s_out, cx_out, cb_out, cc_out, s_ref, bufx, bufb, bufc, y_scr, *, cs, nc):
    c = pl.program_id(1)

    @pl.when(c == 0)
    def _():
        s_ref[...] = jnp.zeros(s_ref.shape, F32)
        for buf in (bufx, bufb, bufc):
            buf[...] = jnp.zeros(buf.shape, F32)

    xs = _silu_of_half(_conv_chunk(bufx, x_ref[0], shift_ref, cwx, cbx, cs))
    bm = _silu_of_half(_conv_chunk(bufb, b_ref[0], shift_ref, cwb, cbb, cs)).astype(BF16)
    cm = _silu_of_half(_conv_chunk(bufc, c_ref[0], shift_ref, cwc, cbc, cs)).astype(BF16)

    dt = _softplus(sm_ref[0] + dtb_ref[...])
    da = dt * (-jnp.exp(alog_ref[...]))
    lower = _lower_mask(cs)
    tri = jnp.where(lower, 1.0, 0.0).astype(BF16)
    cum = _dot01_exact(tri, da)
    cum_t = cum.T
    dt_t = dt.T
    last = cum[cs - 1:cs, :]
    ecum_x = _dot_x01(jnp.exp(cum), e_ref[...])
    w_x = _dot_x01(jnp.exp(last - cum) * dt, e_ref[...])
    elast = jnp.broadcast_to(jnp.exp(last), (SUBLANES, LANES))
    eh, el = _split2(elast)
    ecol = _dot_nt(et_ref[...], eh) + _dot_nt(et_ref[...], el)

    xs_bf = xs.astype(BF16)
    xw = (xs * w_x).astype(BF16)
    z = z_ref[0].astype(F32)
    lane = lax.broadcasted_iota(jnp.int32, (cs, LANES), 1)
    for g in range(SSD_GROUPS):
        gn = slice(g * SSD_D_STATE, (g + 1) * SSD_D_STATE)
        gw = slice(g * SSD_GROUP_W, (g + 1) * SSD_GROUP_W)
        cg, bg = cm[:, gn], bm[:, gn]
        cb = _dot_nt(cg, bg)
        sg = s_ref[pl.ds(g * SSD_GROUP_W, SSD_GROUP_W), :]
        y_inter = _dot_nt(cg, sg.astype(BF16)) * ecum_x[:, gw]
        ss = jnp.zeros((cs, 1), F32)
        for p in range(SSD_HPG // 2):
            pair = slice(g * SSD_GROUP_W + p * LANES, g * SSD_GROUP_W + (p + 1) * LANES)
            xp = xs_bf[:, pair]
            ys = []
            for h in (g * SSD_HPG + 2 * p, g * SSD_HPG + 2 * p + 1):
                seg = jnp.exp(jnp.where(lower, cum[:, h:h + 1] - cum_t[h:h + 1, :], NEG_BIG))
                m = cb * seg * dt_t[h:h + 1, :]
                ys.append(_dot(m.astype(BF16), xp))
            yp = jnp.where(lane < SSD_HEAD_DIM, ys[0], ys[1])
            yp = yp + y_inter[:, p * LANES:(p + 1) * LANES] + dvec_ref[:, pair] * xs[:, pair]
            yp = yp * _silu(z[:, pair])
            ss = ss + jnp.sum(yp * yp, axis=-1, keepdims=True)
            y_scr[:, pair] = yp
        inv = lax.rsqrt(ss * (1.0 / SSD_GROUP_W) + NORM_EPS)
        y_ref[0, :, gw] = (y_scr[:, gw] * inv * nw_ref[:, gw]).astype(BF16)
        s_ref[pl.ds(g * SSD_GROUP_W, SSD_GROUP_W), :] = (
            ecol[g * SSD_GROUP_W:(g + 1) * SSD_GROUP_W, 0:1] * sg + _dot_tn(xw[:, gw], bg))

    @pl.when(c == nc - 1)
    def _():
        s_out[0] = s_ref[...]
        cx_out[0] = _conv_state(bufx)
        cb_out[0] = _conv_state(bufb)
        cc_out[0] = _conv_state(bufc)


def _ssd_prompt(xb, sm, cw, cb, dtb, alog, emat, emat_t, dvec, normw, *, cs):
    nb, l = xb.shape[0], xb.shape[1]
    nc = l // cs

    def seg(width, off):
        return pl.BlockSpec((1, cs, width), lambda b, c: (b, c, off // width))

    def full(shape):
        return pl.BlockSpec(shape, lambda b, c: (0,) * len(shape))

    def tail(width):
        return pl.BlockSpec((1, CONV_W - 1, width), lambda b, c: (b, 0, 0))

    cwx, cwb, cwc = cw[:, :2048], cw[:, 2048:2560], cw[:, 2560:]
    cbx, cbb, cbc = cb[:, :2048], cb[:, 2048:2560], cb[:, 2560:]
    return pl.pallas_call(
        functools.partial(_ssd_body, cs=cs, nc=nc),
        grid=(nb, nc),
        in_specs=[seg(2048, SSD_Z), seg(2048, SSD_X), seg(512, SSD_B), seg(512, SSD_C),
                  pl.BlockSpec((1, cs, LANES), lambda b, c: (b, c, 0)), full((3 * cs, cs)),
                  full((CONV_W, 2048)), full((1, 2048)), full((CONV_W, 512)), full((1, 512)),
                  full((CONV_W, 512)), full((1, 512)), full((1, LANES)), full((1, LANES)),
                  full((LANES, 2048)), full((2048, LANES)), full((1, 2048)), full((1, 2048))],
        out_specs=[pl.BlockSpec((1, cs, 2048), lambda b, c: (b, c, 0)),
                   pl.BlockSpec((1, 2048, SSD_D_STATE), lambda b, c: (b, 0, 0)),
                   tail(2048), tail(512), tail(512)],
        out_shape=[jax.ShapeDtypeStruct((nb, l, 2048), BF16),
                   jax.ShapeDtypeStruct((nb, 2048, SSD_D_STATE), F32),
                   jax.ShapeDtypeStruct((nb, CONV_W - 1, 2048), F32),
                   jax.ShapeDtypeStruct((nb, CONV_W - 1, 512), F32),
                   jax.ShapeDtypeStruct((nb, CONV_W - 1, 512), F32)],
        scratch_shapes=[pltpu.VMEM((2048, SSD_D_STATE), F32),
                        pltpu.VMEM((SUBLANES, 2048), F32),
                        pltpu.VMEM((SUBLANES, 512), F32),
                        pltpu.VMEM((SUBLANES, 512), F32),
                        pltpu.VMEM((cs, 2048), F32)],
        compiler_params=_params(("parallel", "arbitrary")),
        name="ssd_prompt",
    )(xb, xb, xb, xb, sm, _shift_matrix(cs), cwx, cbx, cwb, cbb, cwc, cbc, dtb, alog, emat, emat_t, dvec, normw)


def _lru_gates(hxc, wa_ref, ba_ref, wx_ref, bx_ref, lam_ref):
    xb = hxc.astype(BF16)
    rl, il = [], []
    for n in range(LRU_BLOCKS):
        blk = xb[:, n * LRU_BLOCK_W:(n + 1) * LRU_BLOCK_W]
        rl.append(_dot(blk, wa_ref[n]))
        il.append(_dot(blk, wx_ref[n]))
    tr = jnp.tanh(jnp.concatenate(rl, axis=-1) + 0.5 * ba_ref[...])
    ti = jnp.tanh(jnp.concatenate(il, axis=-1) + 0.5 * bx_ref[...])
    rate = (-0.5 * LRU_C * math.log2(math.e)) * _softplus(-lam_ref[...])
    a = jnp.exp2(rate * (tr + 1.0))
    b = jnp.sqrt(1.0 - a * a) * (hxc * ti + hxc)
    return a, b


def _gelu_tanh(x):
    return 0.5 * x * (1.0 + jnp.tanh(math.sqrt(2.0 / math.pi) * (x + 0.044715 * (x * x * x))))


def _lru_body(rg_ref, rx_ref, shift_ref, cw, cbias, wa_ref, ba_ref, wx_ref, bx_ref, lam_ref, o_ref, h_out, c_out,
              h_ref, buf, a_scr, b_scr, o_scr, *, cs, nc):
    c = pl.program_id(1)

    @pl.when(c == 0)
    def _():
        h_ref[...] = jnp.zeros(h_ref.shape, F32)
        buf[...] = jnp.zeros(buf.shape, F32)

    hxc = _conv_chunk(buf, rx_ref[0], shift_ref, cw, cbias, cs)
    a, b = _lru_gates(hxc, wa_ref, ba_ref, wx_ref, bx_ref, lam_ref)
    tiles = (cs // SUBLANES, SUBLANES, LRU_WIDTH)
    a, b = a.reshape(tiles), b.reshape(tiles)
    rowm = lax.broadcasted_iota(jnp.int32, tiles, 1)
    for sh in (1, 2, 4):
        keep = rowm >= sh
        a_prev = jnp.where(keep, pltpu.roll(a, sh, 1), 1.0)
        b_prev = jnp.where(keep, pltpu.roll(b, sh, 1), 0.0)
        b = a * b_prev + b
        a = a * a_prev
    a_scr[...] = a.reshape(cs, LRU_WIDTH)
    b_scr[...] = b.reshape(cs, LRU_WIDTH)
    h = h_ref[...]
    for t in range(cs // SUBLANES):
        rows = pl.ds(t * SUBLANES, SUBLANES)
        ht = a_scr[rows, :] * h + b_scr[rows, :]
        o_scr[rows, :] = ht
        h = jnp.broadcast_to(ht[SUBLANES - 1:SUBLANES, :], (SUBLANES, LRU_WIDTH))
    h_ref[...] = h
    o_ref[0] = (_gelu_tanh(rg_ref[0].astype(F32)) * o_scr[...]).astype(BF16)

    @pl.when(c == nc - 1)
    def _():
        h_out[0] = h[0:1, :]
        c_out[0] = _conv_state(buf)


def _lru_prompt(xl, cw, cbias, wa, ba, wx, bx, lam, *, cs):
    nb, l = xl.shape[0], xl.shape[1]
    nc = l // cs

    def full(shape):
        return pl.BlockSpec(shape, lambda b, c: (0,) * len(shape))

    return pl.pallas_call(
        functools.partial(_lru_body, cs=cs, nc=nc),
        grid=(nb, nc),
        in_specs=[pl.BlockSpec((1, cs, LRU_WIDTH), lambda b, c: (b, c, LRU_RG // LRU_WIDTH)),
                  pl.BlockSpec((1, cs, LRU_WIDTH), lambda b, c: (b, c, LRU_RX // LRU_WIDTH)),
                  full((3 * cs, cs)), full((CONV_W, LRU_WIDTH)), full((1, LRU_WIDTH)),
                  full((LRU_BLOCKS, LRU_BLOCK_W, LRU_BLOCK_W)), full((1, LRU_WIDTH)),
                  full((LRU_BLOCKS, LRU_BLOCK_W, LRU_BLOCK_W)), full((1, LRU_WIDTH)), full((1, LRU_WIDTH))],
        out_specs=[pl.BlockSpec((1, cs, LRU_WIDTH), lambda b, c: (b, c, 0)),
                   pl.BlockSpec((1, 1, LRU_WIDTH), lambda b, c: (b, 0, 0)),
                   pl.BlockSpec((1, CONV_W - 1, LRU_WIDTH), lambda b, c: (b, 0, 0))],
        out_shape=[jax.ShapeDtypeStruct((nb, l, LRU_WIDTH), BF16),
                   jax.ShapeDtypeStruct((nb, 1, LRU_WIDTH), F32),
                   jax.ShapeDtypeStruct((nb, CONV_W - 1, LRU_WIDTH), F32)],
        scratch_shapes=[pltpu.VMEM((SUBLANES, LRU_WIDTH), F32),
                        pltpu.VMEM((SUBLANES, LRU_WIDTH), F32),
                        pltpu.VMEM((cs, LRU_WIDTH), F32),
                        pltpu.VMEM((cs, LRU_WIDTH), F32),
                        pltpu.VMEM((cs, LRU_WIDTH), F32)],
        compiler_params=_params(("parallel", "arbitrary")),
        name="lru_prompt",
    )(xl, xl, _shift_matrix(cs), cw, cbias, wa, ba, wx, bx, lam)


def _l2norm(x):
    return x * lax.rsqrt(jnp.sum(x * x, axis=-1, keepdims=True) + NORM_EPS)


def _cat3(a, b, c, axis):
    return jnp.concatenate([a, b, c], axis=axis)


def _gdn_body(q_ref, k_ref, v_ref, z_ref, sm_ref, shift_ref, cwq, cwk, cwv, alog_ref, dtb_ref, nw_ref,
              o_ref, s_out, cq_out, ck_out, cv_out, s_ref, bufq, bufk, bufv,
              ph_scr, pl_scr, t_scr, qb_scr, kd_scr, aqk_scr, rh_scr, rl_scr, *, cs, nc):
    c = pl.program_id(1)

    @pl.when(c == 0)
    def _():
        s_ref[...] = jnp.zeros(s_ref.shape, F32)
        for buf in (bufq, bufk, bufv):
            buf[...] = jnp.zeros(buf.shape, F32)

    q = _silu_of_half(_conv_chunk(bufq, q_ref[0], shift_ref, cwq, None, cs))
    k = _silu_of_half(_conv_chunk(bufk, k_ref[0], shift_ref, cwk, None, cs))
    v = _silu_of_half(_conv_chunk(bufv, v_ref[0], shift_ref, cwv, None, cs))
    sm = sm_ref[0]
    g_all = -jnp.exp(alog_ref[...]) * _softplus(sm + dtb_ref[...])
    beta_all = _sigmoid(sm)
    lower = _lower_mask(cs)
    strict = _lower_mask(cs, strict=True)
    diag = lower & jnp.logical_not(strict)
    tri = jnp.where(lower, 1.0, 0.0).astype(BF16)
    gc = _dot01_exact(tri, g_all)
    gc_t = gc.T
    egc_all = jnp.exp(gc)

    for h in range(GDN_HEADS):
        hd = slice(h * GDN_DK, (h + 1) * GDN_DK)
        qh = _l2norm(q[:, hd]) * (GDN_DK ** -0.5)
        kh = _l2norm(k[:, hd])
        beta = beta_all[:, GDN_HEADS + h:GDN_HEADS + h + 1]
        gcol = gc[:, h:h + 1]
        dec = jnp.exp(jnp.where(lower, gcol - gc_t[h:h + 1, :], NEG_BIG))
        qb, kb = qh.astype(BF16), kh.astype(BF16)
        both = _dot_nt(jnp.concatenate([kb, qb], axis=0), kb)
        p = -(beta * jnp.where(strict, dec, 0.0) * both[:cs])
        hi, lo = _split2(p)
        ph_scr[h] = hi
        pl_scr[h] = lo
        t_scr[h] = jnp.where(diag, 1.0, p)
        qb_scr[h] = qb
        aqk_scr[h] = (dec * both[cs:]).astype(BF16)
        glast = gc[cs - 1:cs, h:h + 1]
        kd_scr[h] = (kh * jnp.exp(glast - gcol)).astype(BF16)
        rhs = jnp.concatenate([v[:, hd] * beta, kh * (beta * egc_all[:, h:h + 1])], axis=-1)
        hi, lo = _split2(rhs)
        rh_scr[h] = hi
        rl_scr[h] = lo

    levels = int(math.log2(cs))
    for lvl in range(levels):
        for h in range(GDN_HEADS):
            hi, lo = ph_scr[h], pl_scr[h]
            lhs = _cat3(hi, hi, lo, 1)
            if lvl == 0:
                p2 = _dot(lhs, _cat3(hi, lo, hi, 0))
                t_new = None
            else:
                t = t_scr[h]
                th, tl = _split2(t)
                if lvl < levels - 1:
                    rhs = _cat3(jnp.concatenate([hi, th], axis=1), jnp.concatenate([lo, tl], axis=1),
                                jnp.concatenate([hi, th], axis=1), 0)
                    both = _dot(lhs, rhs)
                    p2, t_new = both[:, :cs], t + both[:, cs:]
                else:
                    p2, t_new = None, t + _dot(lhs, _cat3(th, tl, th, 0))
            if p2 is not None:
                hi2, lo2 = _split2(p2)
                ph_scr[h] = hi2
                pl_scr[h] = lo2
            if t_new is not None:
                t_scr[h] = t_new

    z = z_ref[0].astype(F32)
    us, wqs = [], []
    for h in range(GDN_HEADS):
        th, tl = _split2(t_scr[h])
        rh, rl = rh_scr[h], rl_scr[h]
        sol = _dot(_cat3(th, th, tl, 1), _cat3(rh, rl, rh, 0))
        us.append(sol[:, :GDN_DV])
        wqs.append(jnp.concatenate([sol[:, GDN_DV:].astype(BF16), qb_scr[h]], axis=0))
    dbs, qss = [], []
    for h in range(GDN_HEADS):
        ws_qs = _dot(wqs[h], s_ref[h].astype(BF16))
        dbs.append((us[h] - ws_qs[:cs]).astype(BF16))
        qss.append(ws_qs[cs:])
    for h in range(GDN_HEADS):
        hd = slice(h * GDN_DK, (h + 1) * GDN_DK)
        o = egc_all[:, h:h + 1] * qss[h] + _dot(aqk_scr[h], dbs[h])
        s_ref[h] = jnp.exp(gc[cs - 1:cs, h:h + 1]) * s_ref[h] + _dot_tn(kd_scr[h], dbs[h])
        o_ref[0, :, hd] = (_rms(o, nw_ref[...]) * _silu(z[:, hd])).astype(BF16)

    @pl.when(c == nc - 1)
    def _():
        s_out[0] = s_ref[...]
        cq_out[0] = _conv_state(bufq)
        ck_out[0] = _conv_state(bufk)
        cv_out[0] = _conv_state(bufv)


def _gdn_prompt(xg, sm, cw, alog, dtb, normw, *, cs):
    nb, l = xg.shape[0], xg.shape[1]
    nc = l // cs

    def seg(width, off):
        return pl.BlockSpec((1, cs, width), lambda b, c: (b, c, off // width))

    def full(shape):
        return pl.BlockSpec(shape, lambda b, c: (0,) * len(shape))

    def tail(width):
        return pl.BlockSpec((1, CONV_W - 1, width), lambda b, c: (b, 0, 0))

    hm = (GDN_HEADS, cs, cs)
    return pl.pallas_call(
        functools.partial(_gdn_body, cs=cs, nc=nc),
        grid=(nb, nc),
        in_specs=[seg(1024, GDN_Q), seg(1024, GDN_K), seg(1024, GDN_V), seg(1024, GDN_Z),
                  pl.BlockSpec((1, cs, LANES), lambda b, c: (b, c, 0)), full((3 * cs, cs)),
                  full((CONV_W, 1024)), full((CONV_W, 1024)), full((CONV_W, 1024)),
                  full((1, LANES)), full((1, LANES)), full((1, GDN_DV))],
        out_specs=[pl.BlockSpec((1, cs, 1024), lambda b, c: (b, c, 0)),
                   pl.BlockSpec((1, GDN_HEADS, GDN_DK, GDN_DV), lambda b, c: (b, 0, 0, 0)),
                   tail(1024), tail(1024), tail(1024)],
        out_shape=[jax.ShapeDtypeStruct((nb, l, 1024), BF16),
                   jax.ShapeDtypeStruct((nb, GDN_HEADS, GDN_DK, GDN_DV), F32)]
        + [jax.ShapeDtypeStruct((nb, CONV_W - 1, 1024), F32)] * 3,
        scratch_shapes=[pltpu.VMEM((GDN_HEADS, GDN_DK, GDN_DV), F32)]
        + [pltpu.VMEM((SUBLANES, 1024), F32)] * 3
        + [pltpu.VMEM(hm, BF16), pltpu.VMEM(hm, BF16), pltpu.VMEM(hm, F32),
           pltpu.VMEM((GDN_HEADS, cs, GDN_DK), BF16), pltpu.VMEM((GDN_HEADS, cs, GDN_DK), BF16),
           pltpu.VMEM(hm, BF16), pltpu.VMEM((GDN_HEADS, cs, 2 * GDN_DV), BF16),
           pltpu.VMEM((GDN_HEADS, cs, 2 * GDN_DV), BF16)],
        compiler_params=_params(("parallel", "arbitrary")),
        name="gdn_prompt",
    )(xg, xg, xg, xg, sm, _shift_matrix(cs), cw[:, :1024], cw[:, 1024:2048], cw[:, 2048:], alog, dtb, normw)


def _pad_lanes(v, width, offset=0):
    return jnp.zeros((1, width), F32).at[0, offset:offset + v.shape[0]].set(v.astype(F32))


def _prep(ffn_w_in, ffn_w_out, l0_w_in, l0_w_out, gla_w_gate2, gla_b_gate, gla_norm, ssd_conv_w, ssd_conv_b,
          ssd_dt_bias, ssd_a_log, ssd_d, ssd_norm, l1_w_in, l1_w_out, lru_conv_w, lru_conv_b, lru_w_a, lru_b_a,
          lru_w_x, lru_b_x, lru_lambda, gdn_conv_w, gdn_a_log, gdn_dt_bias, gdn_norm):
    w = {}
    w["ffn_in"] = ffn_w_in.astype(BF16)
    w["ffn_out"] = ffn_w_out.astype(BF16)
    c = l0_w_in
    small0 = jnp.concatenate([c[:, 8208:8240], c[:, 3072:3088], jnp.zeros((D_MODEL, LANES - 48), F32)], axis=1)
    w["l0_in"] = jnp.concatenate(
        [c[:, 1024:2048], c[:, 2048:3072], c[:, 0:512], c[:, 512:1024],
         c[:, 3088:5136], c[:, 5136:7184], c[:, 7184:7696], c[:, 7696:8208],
         small0], axis=1).astype(BF16)
    w["l0_out"] = l0_w_out.astype(BF16)
    w["gla_wg2"] = jnp.zeros((LANES, 512), F32).at[32:48].set(gla_w_gate2).astype(BF16)
    w["gla_bg"] = gla_b_gate.reshape(1, 512)
    w["gla_norm"] = gla_norm.reshape(1, GLA_DV)
    w["ssd_cw"] = ssd_conv_w
    w["ssd_cb"] = ssd_conv_b.reshape(1, -1)
    w["ssd_dtb"] = _pad_lanes(ssd_dt_bias, LANES)
    w["ssd_alog"] = _pad_lanes(ssd_a_log, LANES)
    head_of_lane = jnp.arange(SSD_D_INNER) // SSD_HEAD_DIM
    emat = (jnp.arange(LANES)[:, None] == head_of_lane[None, :])
    w["ssd_e"] = emat.astype(BF16)
    w["ssd_et"] = emat.T.astype(BF16)
    w["ssd_dvec"] = jnp.repeat(ssd_d, SSD_HEAD_DIM).reshape(1, SSD_D_INNER)
    w["ssd_norm"] = ssd_norm.reshape(1, SSD_D_INNER)
    c = l1_w_in
    small1 = jnp.concatenate([c[:, 6656:6672], jnp.zeros((D_MODEL, LANES - 16), F32)], axis=1)
    w["l1_in"] = jnp.concatenate(
        [c[:, 2560:3584], c[:, 3584:4608], c[:, 4608:5632], c[:, 5632:6656],
         c[:, 0:1280], c[:, 1280:2560],
         small1], axis=1).astype(BF16)
    w["l1_out"] = l1_w_out.astype(BF16)
    w["lru_cw"] = lru_conv_w
    w["lru_cb"] = lru_conv_b.reshape(1, -1)
    w["lru_wa"] = lru_w_a.astype(BF16)
    w["lru_ba"] = lru_b_a.reshape(1, -1)
    w["lru_wx"] = lru_w_x.astype(BF16)
    w["lru_bx"] = lru_b_x.reshape(1, -1)
    w["lru_lam"] = lru_lambda.reshape(1, -1)
    w["gdn_cw"] = gdn_conv_w
    w["gdn_alog"] = _pad_lanes(gdn_a_log, LANES)
    w["gdn_dtb"] = _pad_lanes(gdn_dt_bias, LANES)
    w["gdn_norm"] = gdn_norm.reshape(1, GDN_DV)
    return w


L0_WIDTHS = (GLA_W, SSD_W, LANES)
L1_WIDTHS = (GDN_W, LRU_W2, LANES)


def _trunk_prompt(x, norms, final_norm, w, *, tm, cs):
    nb, l = x.shape[0], x.shape[1]
    t = nb * l
    h = x.reshape(t, D_MODEL)
    h = _ffn(h, norms[0, 0], w["ffn_in"], w["ffn_out"], (0, 0), tm=tm)
    xa, xb, sm = _proj(h, norms[0, 1], w["l0_in"], L0_WIDTHS, (BF16, BF16, F32), tm=tm)
    sm = sm.reshape(nb, l, LANES)
    o_a, s_gla = _gla_prompt(xa.reshape(nb, l, GLA_W), sm, w["gla_wg2"], w["gla_bg"], w["gla_norm"], lg=cs)
    y, s_ssd, cx, cb, cc = _ssd_prompt(xb.reshape(nb, l, SSD_W), sm, w["ssd_cw"], w["ssd_cb"], w["ssd_dtb"],
                                       w["ssd_alog"], w["ssd_e"], w["ssd_et"], w["ssd_dvec"], w["ssd_norm"], cs=cs)
    h = _ffn(h, norms[0, 2], w["ffn_in"], w["ffn_out"], (0, 1),
             pre=(o_a.reshape(t, 1024), y.reshape(t, 2048), w["l0_out"]), tm=tm)
    h = _ffn(h, norms[1, 0], w["ffn_in"], w["ffn_out"], (1, 0), tm=tm)
    xg, xl, sm = _proj(h, norms[1, 1], w["l1_in"], L1_WIDTHS, (BF16, BF16, F32), tm=tm)
    sm = sm.reshape(nb, l, LANES)
    o_c, s_lru, s_lru_conv = _lru_prompt(xl.reshape(nb, l, LRU_W2), w["lru_cw"], w["lru_cb"], w["lru_wa"],
                                         w["lru_ba"], w["lru_wx"], w["lru_bx"], w["lru_lam"], cs=2 * cs)
    o_d, s_gdn, cq, ck, cv = _gdn_prompt(xg.reshape(nb, l, GDN_W), sm, w["gdn_cw"], w["gdn_alog"], w["gdn_dtb"],
                                         w["gdn_norm"], cs=cs)
    h = _ffn(h, norms[1, 2], w["ffn_in"], w["ffn_out"], (1, 1), final_norm,
             pre=(o_c.reshape(t, LRU_WIDTH), o_d.reshape(t, 1024), w["l1_out"]), tm=tm)
    return (h.reshape(nb, l, D_MODEL), s_gla, s_ssd.reshape(nb, SSD_HEADS, SSD_HEAD_DIM, SSD_D_STATE),
            jnp.concatenate([cx, cb, cc], axis=-1), s_lru.reshape(nb, LRU_WIDTH), s_lru_conv, s_gdn,
            jnp.concatenate([cq, ck, cv], axis=-1))


def _conv_step(cst_ref, cst_out, cur, w_ref, bias_ref, lo, hi):
    acc = w_ref[CONV_W - 1:CONV_W, lo:hi] * cur
    for j in range(CONV_W - 1):
        acc = acc + w_ref[j:j + 1, lo:hi] * cst_ref[:, j, lo:hi]
    if bias_ref is not None:
        acc = acc + bias_ref[:, lo:hi]
    for j in range(CONV_W - 2):
        cst_out[:, j, lo:hi] = cst_ref[:, j + 1, lo:hi]
    cst_out[:, CONV_W - 2, lo:hi] = cur
    return acc


def _gla_dec_body(q_ref, k_ref, v_ref, gg_ref, sm_ref, wg2_ref, bg_ref, nw_ref, s_in, o_ref, s_out, o_scr, *, nb):
    pre = _dot(sm_ref[...].astype(BF16), wg2_ref[...]) + bg_ref[...]
    a = jnp.exp(-_softplus(-pre) * (1.0 / GLA_GATE_NORM))
    q = q_ref[...] * (GLA_DK ** -0.5)
    v = v_ref[...]
    a_t, q_t, k_t = a.T, q.T, k_ref[...].T
    for b in range(nb):
        for h in range(GLA_HEADS):
            dk = slice(h * GLA_DK, (h + 1) * GLA_DK)
            dv = slice(h * GLA_DV, (h + 1) * GLA_DV)
            sn = a_t[dk, b:b + 1] * s_in[b, h] + k_t[dk, b:b + 1] * v[b:b + 1, dv]
            s_out[b, h] = sn
            o_scr[b:b + 1, dv] = jnp.sum(q_t[dk, b:b + 1] * sn, axis=0, keepdims=True)
    o = o_scr[...]
    gg = gg_ref[...]
    for h in range(GLA_HEADS):
        dv = slice(h * GLA_DV, (h + 1) * GLA_DV)
        o_ref[:, dv] = _rms(o[:, dv], nw_ref[...]) * _silu(gg[:, dv])


def _gla_step(xa, sm, state, wg2, bgate, normw, *, nb):
    n = xa.shape[0]

    def seg(width, off):
        return pl.BlockSpec((nb, width), lambda i: (i, off // width))

    def full(shape):
        return pl.BlockSpec(shape, lambda i: (0,) * len(shape))

    st = pl.BlockSpec((nb, GLA_HEADS, GLA_DK, GLA_DV), lambda i: (i, 0, 0, 0))
    return pl.pallas_call(
        functools.partial(_gla_dec_body, nb=nb),
        grid=(n // nb,),
        in_specs=[seg(512, GLA_GQ), seg(512, GLA_GK), seg(1024, GLA_GV), seg(1024, GLA_GG), seg(LANES, 0),
                  full((LANES, 512)), full((1, 512)), full((1, GLA_DV)), st],
        out_specs=[pl.BlockSpec((nb, 1024), lambda i: (i, 0)), st],
        out_shape=[jax.ShapeDtypeStruct((n, 1024), F32), jax.ShapeDtypeStruct(state.shape, F32)],
        scratch_shapes=[pltpu.VMEM((nb, 1024), F32)],
        compiler_params=_params(("parallel",)),
        name="gla_step",
    )(xa, xa, xa, xa, sm, wg2, bgate, normw, state)


def _ssd_dec_body(z_ref, x_ref, b_ref, c_ref, sm_ref, cst_ref, cw, cbias, dtb_ref, alog_ref, e_ref, dvec_ref,
                  nw_ref, s_in, y_ref, s_out, cst_out, yt_scr, *, nb):
    xs = _silu(_conv_step(cst_ref, cst_out, x_ref[...], cw, cbias, 0, 2048))
    bm = _silu(_conv_step(cst_ref, cst_out, b_ref[...], cw, cbias, 2048, 2560))
    cm = _silu(_conv_step(cst_ref, cst_out, c_ref[...], cw, cbias, 2560, 3072)).astype(BF16)
    dt = _softplus(sm_ref[...] + dtb_ref[...])
    decay = jnp.exp(dt * (-jnp.exp(alog_ref[...])))
    dtx = (_dot_x01(dt, e_ref[...]) * xs).astype(BF16)
    lane = lax.broadcasted_iota(jnp.int32, (SSD_GROUP_W, nb), 1)
    seq = lax.broadcasted_iota(jnp.int32, (nb, 1), 0)
    tile = (SSD_HEAD_DIM, SSD_D_STATE)
    for g in range(SSD_GROUPS):
        gn = slice(g * SSD_D_STATE, (g + 1) * SSD_D_STATE)
        gw = slice(g * SSD_GROUP_W, (g + 1) * SSD_GROUP_W)
        rows = pl.ds(g * SSD_GROUP_W, SSD_GROUP_W)
        ycol = jnp.zeros((SSD_GROUP_W, nb), F32)
        for b in range(nb):
            b_only = jnp.where(seq == b, bm[:, gn], 0.0).astype(BF16)
            dec = jnp.concatenate([jnp.broadcast_to(decay[b:b + 1, h:h + 1], tile)
                                   for h in range(g * SSD_HPG, (g + 1) * SSD_HPG)], axis=0)
            sn = dec * s_in[b, rows, :] + _dot_tn(dtx[:, gw], b_only)
            s_out[b, rows, :] = sn
            ycol = jnp.where(lane == b, _dot_nt(sn.astype(BF16), cm[:, gn]), ycol)
        yt_scr[rows, :] = ycol
    y = yt_scr[...].T + dvec_ref[...] * xs
    y = y * _silu(z_ref[...])
    for g in range(SSD_GROUPS):
        gw = slice(g * SSD_GROUP_W, (g + 1) * SSD_GROUP_W)
        y_ref[:, gw] = _rms(y[:, gw], nw_ref[:, gw])


def _ssd_step(xb, sm, state, cstate, cw, cb, dtb, alog, emat, dvec, normw, *, nb):
    n = xb.shape[0]

    def seg(width, off):
        return pl.BlockSpec((nb, width), lambda i: (i, off // width))

    def full(shape):
        return pl.BlockSpec(shape, lambda i: (0,) * len(shape))

    st = pl.BlockSpec((nb, SSD_D_INNER, SSD_D_STATE), lambda i: (i, 0, 0))
    cst = pl.BlockSpec((nb, CONV_W - 1, 3072), lambda i: (i, 0, 0))
    return pl.pallas_call(
        functools.partial(_ssd_dec_body, nb=nb),
        grid=(n // nb,),
        in_specs=[seg(2048, SSD_Z), seg(2048, SSD_X), seg(512, SSD_B), seg(512, SSD_C), seg(LANES, 0), cst,
                  full((CONV_W, 3072)), full((1, 3072)), full((1, LANES)), full((1, LANES)), full((LANES, 2048)),
                  full((1, 2048)), full((1, 2048)), st],
        out_specs=[pl.BlockSpec((nb, 2048), lambda i: (i, 0)), st, cst],
        out_shape=[jax.ShapeDtypeStruct((n, 2048), F32), jax.ShapeDtypeStruct(state.shape, F32),
                   jax.ShapeDtypeStruct(cstate.shape, F32)],
        scratch_shapes=[pltpu.VMEM((SSD_D_INNER, nb), F32)],
        compiler_params=_params(("parallel",)),
        name="ssd_step",
    )(xb, xb, xb, xb, sm, cstate, cw, cb, dtb, alog, emat, dvec, normw, state)


def _lru_dec_body(rg_ref, rx_ref, cst_ref, cw, cbias, wa_ref, ba_ref, wx_ref, bx_ref, lam_ref, h_in,
                  o_ref, h_out, cst_out):
    xc = _conv_step(cst_ref, cst_out, rx_ref[...], cw, cbias, 0, LRU_WIDTH)
    a, b = _lru_gates(0.5 * xc, wa_ref, ba_ref, wx_ref, bx_ref, lam_ref)
    h = a * h_in[...] + b
    h_out[...] = h
    o_ref[...] = _gelu_tanh(rg_ref[...]) * h


def _lru_step(xl, hstate, cstate, cw, cbias, wa, ba, wx, bx, lam):
    n = xl.shape[0]

    def full(shape):
        return pl.BlockSpec(shape, lambda i: (0,) * len(shape))

    return pl.pallas_call(
        _lru_dec_body,
        grid=(1,),
        in_specs=[pl.BlockSpec((n, LRU_WIDTH), lambda i: (0, LRU_RG // LRU_WIDTH)),
                  pl.BlockSpec((n, LRU_WIDTH), lambda i: (0, LRU_RX // LRU_WIDTH)),
                  full((n, CONV_W - 1, LRU_WIDTH)), full((CONV_W, LRU_WIDTH)), full((1, LRU_WIDTH)),
                  full((LRU_BLOCKS, LRU_BLOCK_W, LRU_BLOCK_W)), full((1, LRU_WIDTH)),
                  full((LRU_BLOCKS, LRU_BLOCK_W, LRU_BLOCK_W)), full((1, LRU_WIDTH)), full((1, LRU_WIDTH)),
                  full((n, LRU_WIDTH))],
        out_specs=[full((n, LRU_WIDTH)), full((n, LRU_WIDTH)), full((n, CONV_W - 1, LRU_WIDTH))],
        out_shape=[jax.ShapeDtypeStruct((n, LRU_WIDTH), F32), jax.ShapeDtypeStruct((n, LRU_WIDTH), F32),
                   jax.ShapeDtypeStruct(cstate.shape, F32)],
        compiler_params=_params(("arbitrary",)),
        name="lru_step",
    )(xl, xl, cstate, cw, cbias, wa, ba, wx, bx, lam, hstate)


def _gdn_dec_body(q_ref, k_ref, v_ref, z_ref, sm_ref, cst_ref, cw, alog_ref, dtb_ref, nw_ref, s_in,
                  o_ref, s_out, cst_out, o_scr, *, nb):
    q = _silu(_conv_step(cst_ref, cst_out, q_ref[...], cw, None, 0, 1024))
    k = _silu(_conv_step(cst_ref, cst_out, k_ref[...], cw, None, 1024, 2048))
    v = _silu(_conv_step(cst_ref, cst_out, v_ref[...], cw, None, 2048, 3072))
    sm = sm_ref[...]
    eg_all = jnp.exp(-jnp.exp(alog_ref[...]) * _softplus(sm + dtb_ref[...]))
    beta_all = _sigmoid(sm)
    qn = jnp.concatenate([_l2norm(q[:, h * GDN_DK:(h + 1) * GDN_DK]) for h in range(GDN_HEADS)], axis=-1)
    qn = qn * (GDN_DK ** -0.5)
    kn = jnp.concatenate([_l2norm(k[:, h * GDN_DK:(h + 1) * GDN_DK]) for h in range(GDN_HEADS)], axis=-1)
    kq = jnp.concatenate([kn, qn], axis=0).astype(BF16)
    seq = lax.broadcasted_iota(jnp.int32, (nb, 1), 0)
    pairs = [(b, h) for b in range(nb) for h in range(GDN_HEADS)]
    ksqs = [_dot(kq[:, h * GDN_DK:(h + 1) * GDN_DK], s_in[b, h].astype(BF16)) for b, h in pairs]
    for (b, h), kq_s in zip(pairs, ksqs):
        hd = slice(h * GDN_DK, (h + 1) * GDN_DK)
        eg = eg_all[b:b + 1, h:h + 1]
        beta = beta_all[b:b + 1, GDN_HEADS + h:GDN_HEADS + h + 1]
        delta = beta * (v[b:b + 1, hd] - eg * kq_s[b:b + 1])
        qk = jnp.sum(qn[b:b + 1, hd] * kn[b:b + 1, hd], axis=-1, keepdims=True)
        o_scr[b:b + 1, hd] = eg * kq_s[nb + b:nb + b + 1] + qk * delta
        d_only = jnp.where(seq == b, jnp.broadcast_to(delta, (nb, GDN_DV)), 0.0).astype(BF16)
        s_out[b, h] = eg * s_in[b, h] + _dot_tn(kq[:nb, hd], d_only)
    o = o_scr[...]
    z = z_ref[...]
    for h in range(GDN_HEADS):
        hd = slice(h * GDN_DK, (h + 1) * GDN_DK)
        o_ref[:, hd] = _rms(o[:, hd], nw_ref[...]) * _silu(z[:, hd])


def _gdn_step(xg, sm, state, cstate, cw, alog, dtb, normw, *, nb):
    n = xg.shape[0]

    def seg(width, off):
        return pl.BlockSpec((nb, width), lambda i: (i, off // width))

    def full(shape):
        return pl.BlockSpec(shape, lambda i: (0,) * len(shape))

    st = pl.BlockSpec((nb, GDN_HEADS, GDN_DK, GDN_DV), lambda i: (i, 0, 0, 0))
    cst = pl.BlockSpec((nb, CONV_W - 1, 3072), lambda i: (i, 0, 0))
    return pl.pallas_call(
        functools.partial(_gdn_dec_body, nb=nb),
        grid=(n // nb,),
        in_specs=[seg(1024, GDN_Q), seg(1024, GDN_K), seg(1024, GDN_V), seg(1024, GDN_Z), seg(LANES, 0), cst,
                  full((CONV_W, 3072)), full((1, LANES)), full((1, LANES)), full((1, GDN_DV)), st],
        out_specs=[pl.BlockSpec((nb, 1024), lambda i: (i, 0)), st, cst],
        out_shape=[jax.ShapeDtypeStruct((n, 1024), F32), jax.ShapeDtypeStruct(state.shape, F32),
                   jax.ShapeDtypeStruct(cstate.shape, F32)],
        scratch_shapes=[pltpu.VMEM((nb, 1024), F32)],
        compiler_params=_params(("parallel",)),
        name="gdn_step",
    )(xg, xg, xg, xg, sm, cstate, cw, alog, dtb, normw, state)


def _trunk_sample(x, s_gla, s_ssd, s_ssd_conv, s_lru, s_lru_conv, s_gdn, s_gdn_conv, norms, final_norm, w, *, nb=8):
    n = x.shape[0]
    f3 = (F32, F32, F32)
    h = x.reshape(n, D_MODEL)
    h = _ffn(h, norms[0, 0], w["ffn_in"], w["ffn_out"], (0, 0), tm=n)
    xa, xb, sm = _proj(h, norms[0, 1], w["l0_in"], L0_WIDTHS, f3, tm=n)
    o_a, n_gla = _gla_step(xa, sm, s_gla, w["gla_wg2"], w["gla_bg"], w["gla_norm"], nb=nb)
    y, n_ssd, n_ssd_conv = _ssd_step(xb, sm, s_ssd.reshape(n, SSD_D_INNER, SSD_D_STATE), s_ssd_conv, w["ssd_cw"],
                                     w["ssd_cb"], w["ssd_dtb"], w["ssd_alog"], w["ssd_e"], w["ssd_dvec"],
                                     w["ssd_norm"], nb=nb)
    h = _ffn(h, norms[0, 2], w["ffn_in"], w["ffn_out"], (0, 1), pre=(o_a, y, w["l0_out"]), tm=n)
    h = _ffn(h, norms[1, 0], w["ffn_in"], w["ffn_out"], (1, 0), tm=n)
    xg, xl, sm = _proj(h, norms[1, 1], w["l1_in"], L1_WIDTHS, f3, tm=n)
    o_c, n_lru, n_lru_conv = _lru_step(xl, s_lru, s_lru_conv, w["lru_cw"], w["lru_cb"], w["lru_wa"], w["lru_ba"],
                                       w["lru_wx"], w["lru_bx"], w["lru_lam"])
    o_d, n_gdn, n_gdn_conv = _gdn_step(xg, sm, s_gdn, s_gdn_conv, w["gdn_cw"], w["gdn_alog"], w["gdn_dtb"],
                                       w["gdn_norm"], nb=nb)
    h = _ffn(h, norms[1, 2], w["ffn_in"], w["ffn_out"], (1, 1), final_norm,
             pre=(o_c, o_d, w["l1_out"]), tm=n)
    return (h.reshape(n, 1, D_MODEL), n_gla, n_ssd.reshape(s_ssd.shape), n_ssd_conv, n_lru, n_lru_conv, n_gdn,
            n_gdn_conv)


def kernel(x_prompt, x_sample, state_gla, state_ssd, state_ssd_conv, state_lru, state_lru_conv, state_gdn,
           state_gdn_conv, norms, final_norm, ffn_w_in, ffn_w_out, l0_w_in, l0_w_out, gla_w_gate2, gla_b_gate,
           gla_norm, ssd_conv_w, ssd_conv_b, ssd_dt_bias, ssd_a_log, ssd_d, ssd_norm, l1_w_in, l1_w_out,
           lru_conv_w, lru_conv_b, lru_w_a, lru_b_a, lru_w_x, lru_b_x, lru_lambda, gdn_conv_w, gdn_a_log,
           gdn_dt_bias, gdn_norm):
    w = _prep(ffn_w_in, ffn_w_out, l0_w_in, l0_w_out, gla_w_gate2, gla_b_gate, gla_norm, ssd_conv_w, ssd_conv_b,
              ssd_dt_bias, ssd_a_log, ssd_d, ssd_norm, l1_w_in, l1_w_out, lru_conv_w, lru_conv_b, lru_w_a, lru_b_a,
              lru_w_x, lru_b_x, lru_lambda, gdn_conv_w, gdn_a_log, gdn_dt_bias, gdn_norm)
    prompt = _trunk_prompt(x_prompt, norms, final_norm, w, tm=512, cs=128)
    sample = _trunk_sample(x_sample, state_gla, state_ssd, state_ssd_conv, state_lru, state_lru_conv, state_gdn,
                           state_gdn_conv, norms, final_norm, w)
    return (prompt[0], sample[0]) + tuple(prompt[1:]) + tuple(sample[1:])
```

```python
import functools
import math

import jax
import jax.numpy as jnp
from jax import lax
from jax.experimental import pallas as pl
from jax.experimental.pallas import tpu as pltpu

F32 = jnp.float32
BF16 = jnp.bfloat16

D_MODEL = 1024
NORM_EPS = 1e-6
CONV_W = 4
D_FF = 2816
FFN_RES = 0.5

GLA_HEADS = 4
GLA_DK = 128
GLA_DV = 256
GLA_RANK = 16
GLA_GATE_NORM = 16.0
GLA_SUB = 16
GLA_FAST_MAX = 60.0

SSD_D_INNER = 2048
SSD_HEAD_DIM = 64
SSD_HEADS = 32
SSD_GROUPS = 4
SSD_HPG = 8
SSD_D_STATE = 128
SSD_GROUP_W = SSD_HPG * SSD_HEAD_DIM

LRU_WIDTH = 1280
LRU_BLOCKS = 10
LRU_BLOCK_W = 128
LRU_C = 8.0

GDN_HEADS = 8
GDN_DK = 128
GDN_DV = 128

LANES = 128
SUBLANES = 8
NEG_BIG = -1e30
VMEM_LIMIT = 56 * 1024 * 1024

GLA_W = 3072
GLA_GV, GLA_GG, GLA_GQ, GLA_GK = 0, 1024, 2048, 2560
SSD_W = 5120
SSD_Z, SSD_X, SSD_B, SSD_C = 0, 2048, 4096, 4608
GDN_W = 4096
GDN_Q, GDN_K, GDN_V, GDN_Z = 0, 1024, 2048, 3072
LRU_W2 = 2560
LRU_RG, LRU_RX = 0, 1280


def _sigmoid(x):
    return 0.5 * jnp.tanh(0.5 * x) + 0.5


def _silu(x):
    return _silu_of_half(0.5 * x)


def _silu_of_half(hx):
    return hx * jnp.tanh(hx) + hx


def _softplus(x):
    return jnp.maximum(x, 0.0) + jnp.log(1.0 + jnp.exp(-jnp.abs(x)))


def _rms(x, g):
    return x * lax.rsqrt(jnp.mean(x * x, axis=-1, keepdims=True) + NORM_EPS) * g


def _dot(a, b):
    return jnp.dot(a, b, preferred_element_type=F32)


def _dot_nt(a, b):
    return lax.dot_general(a, b, (((1,), (1,)), ((), ())), preferred_element_type=F32)


def _dot_tn(a, b):
    return lax.dot_general(a, b, (((0,), (0,)), ((), ())), preferred_element_type=F32)


def _split2(x):
    hi = x.astype(BF16)
    lo = (x - hi.astype(F32)).astype(BF16)
    return hi, lo


def _split3(x):
    hi = x.astype(BF16)
    r = x - hi.astype(F32)
    mid = r.astype(BF16)
    lo = (r - mid.astype(F32)).astype(BF16)
    return hi, mid, lo


def _dot01_exact(m01, x):
    hi, mid, lo = _split3(x)
    return _dot(m01, hi) + _dot(m01, mid) + _dot(m01, lo)


def _dot_x01(x, m01):
    hi, lo = _split2(x)
    return _dot(hi, m01) + _dot(lo, m01)


def _lower_mask(n, strict=False):
    row = lax.broadcasted_iota(jnp.int32, (n, n), 0)
    col = lax.broadcasted_iota(jnp.int32, (n, n), 1)
    return row > col if strict else row >= col


def _params(sem):
    return pltpu.CompilerParams(dimension_semantics=sem, vmem_limit_bytes=VMEM_LIMIT)


FFN_CHUNK = 256


def _resident(shape):
    return pl.BlockSpec(shape, lambda i: (0,) * len(shape), pipeline_mode=pl.Buffered(1))


def _ffn_body(*refs, final, pre):
    refs = list(refs)
    x_ref = refs.pop(0)
    if pre:
        a_ref, b_ref, wab_ref = refs[:3]
        refs = refs[3:]
    g_ref, wi_ref, wo_ref = refs[:3]
    refs = refs[3:]
    fg_ref = refs.pop(0) if final else None
    o_ref, act_ref = refs
    x = x_ref[...]
    if pre:
        ka = a_ref.shape[1]
        x = x + _dot(a_ref[...].astype(BF16), wab_ref[:ka, :]) + _dot(b_ref[...].astype(BF16), wab_ref[ka:, :])
    hn = _rms(x, g_ref[...]).astype(BF16)
    for c in range(D_FF // FFN_CHUNK):
        cols = slice(c * FFN_CHUNK, (c + 1) * FFN_CHUNK)
        gate = _dot(hn, wi_ref[:, cols])
        up = _dot(hn, wi_ref[:, D_FF + c * FFN_CHUNK:D_FF + (c + 1) * FFN_CHUNK])
        act_ref[:, cols] = (_silu(gate) * up).astype(BF16)
    y = x + FFN_RES * _dot(act_ref[...], wo_ref[...])
    if final:
        y = _rms(y, fg_ref[...])
    o_ref[...] = y


def _ffn(x, g, w_in, w_out, sel, final_g=None, pre=None, *, tm):
    t = x.shape[0]
    final = final_g is not None
    in_specs = [pl.BlockSpec((tm, D_MODEL), lambda i: (i, 0))]
    args = [x]
    if pre is not None:
        a, b, wab = pre
        in_specs += [pl.BlockSpec((tm, a.shape[1]), lambda i: (i, 0)), pl.BlockSpec((tm, b.shape[1]), lambda i: (i, 0)),
                     _resident(wab.shape)]
        args += [a, b, wab]
    in_specs += [_resident((1, D_MODEL)),
                 pl.BlockSpec((None, None) + w_in.shape[2:], lambda i: sel + (0, 0), pipeline_mode=pl.Buffered(1)),
                 pl.BlockSpec((None, None) + w_out.shape[2:], lambda i: sel + (0, 0), pipeline_mode=pl.Buffered(1))]
    args += [g.reshape(1, D_MODEL), w_in, w_out]
    if final:
        in_specs.append(_resident((1, D_MODEL)))
        args.append(final_g.reshape(1, D_MODEL))
    return pl.pallas_call(
        functools.partial(_ffn_body, final=final, pre=pre is not None),
        grid=(t // tm,),
        in_specs=in_specs,
        out_specs=pl.BlockSpec((tm, D_MODEL), lambda i: (i, 0)),
        out_shape=jax.ShapeDtypeStruct((t, D_MODEL), F32),
        scratch_shapes=[pltpu.VMEM((tm, D_FF), BF16)],
        compiler_params=_params(("parallel",)),
        name="ffn",
    )(*args)


PROJ_CHUNK = 1024


def _proj_body(x_ref, g_ref, w_ref, *o_refs, widths):
    hn = _rms(x_ref[...], g_ref[...]).astype(BF16)
    off = 0
    for o_ref, width in zip(o_refs, widths):
        for c in range(0, width, PROJ_CHUNK):
            n = min(PROJ_CHUNK, width - c)
            o_ref[:, c:c + n] = _dot(hn, w_ref[:, off + c:off + c + n]).astype(o_ref.dtype)
        off += width


def _proj(x, g, w, widths, dtypes, *, tm):
    t = x.shape[0]
    return pl.pallas_call(
        functools.partial(_proj_body, widths=tuple(widths)),
        grid=(t // tm,),
        in_specs=[pl.BlockSpec((tm, D_MODEL), lambda i: (i, 0)), _resident((1, D_MODEL)), _resident(w.shape)],
        out_specs=[pl.BlockSpec((tm, n), lambda i: (i, 0)) for n in widths],
        out_shape=[jax.ShapeDtypeStruct((t, n), dt) for n, dt in zip(widths, dtypes)],
        compiler_params=_params(("parallel",)),
        name="proj",
    )(x, g.reshape(1, D_MODEL), w)


def _shift_matrix(c):
    out_row = jnp.arange(3 * c)
    k = out_row // c + 1
    src = out_row % c - k
    return (src[:, None] == jnp.arange(c)[None, :]).astype(BF16)


def _conv_chunk(tail, src, shift_ref, w_ref, bias_ref, c):
    x = src.astype(F32)
    sh = _dot(shift_ref[...], src)
    w0, w1, w2, w3 = (0.5 * w_ref[j:j + 1, :] for j in range(CONV_W))
    acc = w3 * x + w2 * sh[:c] + w1 * sh[c:2 * c] + w0 * sh[2 * c:]
    if bias_ref is not None:
        acc = acc + 0.5 * bias_ref[...]
    t3, t2, t1 = tail[5:6, :], tail[6:7, :], tail[7:8, :]
    row = lax.broadcasted_iota(jnp.int32, (SUBLANES, x.shape[1]), 0)
    head = jnp.where(row == 0, w0 * t3 + w1 * t2 + w2 * t1,
                     jnp.where(row == 1, w0 * t2 + w1 * t1, jnp.where(row == 2, w0 * t1, 0.0)))
    tail[pl.ds(5, 3), :] = x[c - 3:c, :]
    return jnp.concatenate([acc[:SUBLANES] + head, acc[SUBLANES:]], axis=0)


def _conv_state(tail):
    return tail[pl.ds(5, 3), :]


def _gla_body(q_in, k_in, v_in, gg_ref, sm_ref, wg2_ref, bg_ref, nw_ref, o_ref, s_out, st_ref, b_ref,
              q_ref, k_ref, v_ref, *, lg, nc):
    c = pl.program_id(1)

    @pl.when(c == 0)
    def _():
        st_ref[...] = jnp.zeros(st_ref.shape, F32)

    pre = _dot(sm_ref[0].astype(BF16), wg2_ref[...]) + bg_ref[...]
    log_a = -_softplus(-pre) * (1.0 / GLA_GATE_NORM)
    lower = _lower_mask(lg)
    b_all = _dot01_exact(jnp.where(lower, 1.0, 0.0).astype(BF16), log_a)
    b_ref[...] = b_all
    fast = jnp.max(-b_all[lg - 1:lg, :]) <= GLA_FAST_MAX

    q = q_in[0].astype(F32) * (GLA_DK ** -0.5)
    k = k_in[0].astype(F32)
    vb = v_in[0]
    blast = b_all[lg - 1:lg, :]
    qe = (q * jnp.exp(b_all)).astype(BF16)
    ke = (k * jnp.exp(-b_all)).astype(BF16)
    kd = (k * jnp.exp(blast - b_all)).astype(BF16)
    elast = jnp.exp(blast)
    gg = gg_ref[0].astype(F32)
    for h in range(GLA_HEADS):
        dk = slice(h * GLA_DK, (h + 1) * GLA_DK)
        dv = slice(h * GLA_DV, (h + 1) * GLA_DV)
        att = jnp.where(lower, _dot_nt(qe[:, dk], ke[:, dk]), 0.0).astype(BF16)
        st = st_ref[h]
        oh = _dot(att, vb[:, dv]) + _dot_nt(qe[:, dk], st.astype(BF16))
        st_ref[h] = jnp.where(fast, st * elast[:, dk] + _dot_tn(vb[:, dv], kd[:, dk]), st)
        o_ref[0, :, dv] = (_rms(oh, nw_ref[...]) * _silu(gg[:, dv])).astype(BF16)

    @pl.when(jnp.logical_not(fast))
    def _():
        q_ref[...] = q_in[0].astype(F32) * (GLA_DK ** -0.5)
        k_ref[...] = k_in[0].astype(F32)
        v_ref[...] = v_in[0].astype(F32)
        rowi = lax.broadcasted_iota(jnp.int32, (GLA_SUB, 1), 0)

        def block(i, carry):
            base = pl.multiple_of(i * GLA_SUB, GLA_SUB)
            q = q_ref[pl.ds(base, GLA_SUB), :]
            k = k_ref[pl.ds(base, GLA_SUB), :]
            v = v_ref[pl.ds(base, GLA_SUB), :]
            b = b_ref[pl.ds(base, GLA_SUB), :]
            o = [jnp.zeros((GLA_SUB, GLA_DV), F32) for _ in range(GLA_HEADS)]
            for s in range(GLA_SUB):
                ks = k_ref[pl.ds(base + s, 1), :]
                bs = b_ref[pl.ds(base + s, 1), :]
                vs = v_ref[pl.ds(base + s, 1), :]
                p = q * ks * jnp.exp(b - bs)
                for h in range(GLA_HEADS):
                    att = jnp.sum(p[:, h * GLA_DK:(h + 1) * GLA_DK], axis=-1, keepdims=True)
                    att = jnp.where(rowi >= s, att, 0.0)
                    o[h] = o[h] + att * vs[:, h * GLA_DV:(h + 1) * GLA_DV]
            bprev = jnp.where(i > 0, b_ref[pl.ds(jnp.maximum(base - 1, 0), 1), :], 0.0)
            blast = b_ref[pl.ds(base + GLA_SUB - 1, 1), :]
            qe = (q * jnp.exp(b - bprev)).astype(BF16)
            kd = (k * jnp.exp(blast - b)).astype(BF16)
            elast = jnp.exp(blast - bprev)
            vb = v.astype(BF16)
            gg = gg_ref[0, pl.ds(base, GLA_SUB), :].astype(F32)
            for h in range(GLA_HEADS):
                dk = slice(h * GLA_DK, (h + 1) * GLA_DK)
                dv = slice(h * GLA_DV, (h + 1) * GLA_DV)
                st = st_ref[h]
                oh = o[h] + _dot_nt(qe[:, dk], st.astype(BF16))
                st_ref[h] = st * elast[:, dk] + _dot_tn(vb[:, dv], kd[:, dk])
                o_ref[0, pl.ds(base, GLA_SUB), dv] = (_rms(oh, nw_ref[...]) * _silu(gg[:, dv])).astype(BF16)
            return carry

        lax.fori_loop(0, lg // GLA_SUB, block, 0)

    @pl.when(c == nc - 1)
    def _():
        for h in range(GLA_HEADS):
            s_out[0, h] = st_ref[h].T


def _gla_prompt(xa, sm, wg2, bgate, normw, *, lg):
    nb, l = xa.shape[0], xa.shape[1]
    nc = l // lg

    def seg(width, off):
        return pl.BlockSpec((1, lg, width), lambda b, c: (b, c, off // width))

    def full(shape):
        return pl.BlockSpec(shape, lambda b, c: (0,) * len(shape))

    return pl.pallas_call(
        functools.partial(_gla_body, lg=lg, nc=nc),
        grid=(nb, nc),
        in_specs=[seg(512, GLA_GQ), seg(512, GLA_GK), seg(1024, GLA_GV), seg(1024, GLA_GG),
                  pl.BlockSpec((1, lg, LANES), lambda b, c: (b, c, 0)),
                  full((LANES, 512)), full((1, 512)), full((1, GLA_DV))],
        out_specs=[pl.BlockSpec((1, lg, 1024), lambda b, c: (b, c, 0)),
                   pl.BlockSpec((1, GLA_HEADS, GLA_DK, GLA_DV), lambda b, c: (b, 0, 0, 0))],
        out_shape=[jax.ShapeDtypeStruct((nb, l, 1024), BF16),
                   jax.ShapeDtypeStruct((nb, GLA_HEADS, GLA_DK, GLA_DV), F32)],
        scratch_shapes=[pltpu.VMEM((GLA_HEADS, GLA_DV, GLA_DK), F32), pltpu.VMEM((lg, 512), F32),
                        pltpu.VMEM((lg, 512), F32), pltpu.VMEM((lg, 512), F32), pltpu.VMEM((lg, 1024), F32)],
        compiler_params=_params(("parallel", "arbitrary")),
        name="gla_prompt",
    )(xa, xa, xa, xa, sm, wg2, bgate, normw)


def _ssd_body(z_ref, x_ref, b_ref, c_ref, sm_ref, shift_ref, cwx, cbx, cwb, cbb, cwc, cbc, dtb_ref, alog_ref, e_ref, et_ref,
              dvec_ref, nw_ref, y_ref, s_out, cx_out, cb_out, cc_out, s_ref, bufx, bufb, bufc, y_scr, *, cs, nc):
    c = pl.program_id(1)

    @pl.when(c == 0)
    def _():
        s_ref[...] = jnp.zeros(s_ref.shape, F32)
        for buf in (bufx, bufb, bufc):
            buf[...] = jnp.zeros(buf.shape, F32)

    xs = _silu_of_half(_conv_chunk(bufx, x_ref[0], shift_ref, cwx, cbx, cs))
    bm = _silu_of_half(_conv_chunk(bufb, b_ref[0], shift_ref, cwb, cbb, cs)).astype(BF16)
    cm = _silu_of_half(_conv_chunk(bufc, c_ref[0], shift_ref, cwc, cbc, cs)).astype(BF16)

    dt = _softplus(sm_ref[0] + dtb_ref[...])
    da = dt * (-jnp.exp(alog_ref[...]))
    lower = _lower_mask(cs)
    tri = jnp.where(lower, 1.0, 0.0).astype(BF16)
    cum = _dot01_exact(tri, da)
    cum_t = cum.T
    dt_t = dt.T
    last = cum[cs - 1:cs, :]
    ecum_x = _dot_x01(jnp.exp(cum), e_ref[...])
    w_x = _dot_x01(jnp.exp(last - cum) * dt, e_ref[...])
    elast = jnp.broadcast_to(jnp.exp(last), (SUBLANES, LANES))
    eh, el = _split2(elast)
    ecol = _dot_nt(et_ref[...], eh) + _dot_nt(et_ref[...], el)

    xs_bf = xs.astype(BF16)
    xw = (xs * w_x).astype(BF16)
    z = z_ref[0].astype(F32)
    lane = lax.broadcasted_iota(jnp.int32, (cs, LANES), 1)
    for g in range(SSD_GROUPS):
        gn = slice(g * SSD_D_STATE, (g + 1) * SSD_D_STATE)
        gw = slice(g * SSD_GROUP_W, (g + 1) * SSD_GROUP_W)
        cg, bg = cm[:, gn], bm[:, gn]
        cb = _dot_nt(cg, bg)
        sg = s_ref[pl.ds(g * SSD_GROUP_W, SSD_GROUP_W), :]
        y_inter = _dot_nt(cg, sg.astype(BF16)) * ecum_x[:, gw]
        ss = jnp.zeros((cs, 1), F32)
        for p in range(SSD_HPG // 2):
            pair = slice(g * SSD_GROUP_W + p * LANES, g * SSD_GROUP_W + (p + 1) * LANES)
            xp = xs_bf[:, pair]
            ys = []
            for h in (g * SSD_HPG + 2 * p, g * SSD_HPG + 2 * p + 1):
                seg = jnp.exp(jnp.where(lower, cum[:, h:h + 1] - cum_t[h:h + 1, :], NEG_BIG))
                m = cb * seg * dt_t[h:h + 1, :]
                ys.append(_dot(m.astype(BF16), xp))
            yp = jnp.where(lane < SSD_HEAD_DIM, ys[0], ys[1])
            yp = yp + y_inter[:, p * LANES:(p + 1) * LANES] + dvec_ref[:, pair] * xs[:, pair]
            yp = yp * _silu(z[:, pair])
            ss = ss + jnp.sum(yp * yp, axis=-1, keepdims=True)
            y_scr[:, pair] = yp
        inv = lax.rsqrt(ss * (1.0 / SSD_GROUP_W) + NORM_EPS)
        y_ref[0, :, gw] = (y_scr[:, gw] * inv * nw_ref[:, gw]).astype(BF16)
        s_ref[pl.ds(g * SSD_GROUP_W, SSD_GROUP_W), :] = (
            ecol[g * SSD_GROUP_W:(g + 1) * SSD_GROUP_W, 0:1] * sg + _dot_tn(xw[:, gw], bg))

    @pl.when(c == nc - 1)
    def _():
        s_out[0] = s_ref[...]
        cx_out[0] = _conv_state(bufx)
        cb_out[0] = _conv_state(bufb)
        cc_out[0] = _conv_state(bufc)


def _ssd_prompt(xb, sm, cw, cb, dtb, alog, emat, emat_t, dvec, normw, *, cs):
    nb, l = xb.shape[0], xb.shape[1]
    nc = l // cs

    def seg(width, off):
        return pl.BlockSpec((1, cs, width), lambda b, c: (b, c, off // width))

    def full(shape):
        return pl.BlockSpec(shape, lambda b, c: (0,) * len(shape))

    def tail(width):
        return pl.BlockSpec((1, CONV_W - 1, width), lambda b, c: (b, 0, 0))

    cwx, cwb, cwc = cw[:, :2048], cw[:, 2048:2560], cw[:, 2560:]
    cbx, cbb, cbc = cb[:, :2048], cb[:, 2048:2560], cb[:, 2560:]
    return pl.pallas_call(
        functools.partial(_ssd_body, cs=cs, nc=nc),
        grid=(nb, nc),
        in_specs=[seg(2048, SSD_Z), seg(2048, SSD_X), seg(512, SSD_B), seg(512, SSD_C),
                  pl.BlockSpec((1, cs, LANES), lambda b, c: (b, c, 0)), full((3 * cs, cs)),
                  full((CONV_W, 2048)), full((1, 2048)), full((CONV_W, 512)), full((1, 512)),
                  full((CONV_W, 512)), full((1, 512)), full((1, LANES)), full((1, LANES)),
                  full((LANES, 2048)), full((2048, LANES)), full((1, 2048)), full((1, 2048))],
        out_specs=[pl.BlockSpec((1, cs, 2048), lambda b, c: (b, c, 0)),
                   pl.BlockSpec((1, 2048, SSD_D_STATE), lambda b, c: (b, 0, 0)),
                   tail(2048), tail(512), tail(512)],
        out_shape=[jax.ShapeDtypeStruct((nb, l, 2048), BF16),
                   jax.ShapeDtypeStruct((nb, 2048, SSD_D_STATE), F32),
                   jax.ShapeDtypeStruct((nb, CONV_W - 1, 2048), F32),
                   jax.ShapeDtypeStruct((nb, CONV_W - 1, 512), F32),
                   jax.ShapeDtypeStruct((nb, CONV_W - 1, 512), F32)],
        scratch_shapes=[pltpu.VMEM((2048, SSD_D_STATE), F32),
                        pltpu.VMEM((SUBLANES, 2048), F32),
                        pltpu.VMEM((SUBLANES, 512), F32),
                        pltpu.VMEM((SUBLANES, 512), F32),
                        pltpu.VMEM((cs, 2048), F32)],
        compiler_params=_params(("parallel", "arbitrary")),
        name="ssd_prompt",
    )(xb, xb, xb, xb, sm, _shift_matrix(cs), cwx, cbx, cwb, cbb, cwc, cbc, dtb, alog, emat, emat_t, dvec, normw)


def _lru_gates(hxc, wa_ref, ba_ref, wx_ref, bx_ref, lam_ref):
    xb = hxc.astype(BF16)
    rl, il = [], []
    for n in range(LRU_BLOCKS):
        blk = xb[:, n * LRU_BLOCK_W:(n + 1) * LRU_BLOCK_W]
        rl.append(_dot(blk, wa_ref[n]))
        il.append(_dot(blk, wx_ref[n]))
    tr = jnp.tanh(jnp.concatenate(rl, axis=-1) + 0.5 * ba_ref[...])
    ti = jnp.tanh(jnp.concatenate(il, axis=-1) + 0.5 * bx_ref[...])
    rate = (-0.5 * LRU_C * math.log2(math.e)) * _softplus(-lam_ref[...])
    a = jnp.exp2(rate * (tr + 1.0))
    b = jnp.sqrt(1.0 - a * a) * (hxc * ti + hxc)
    return a, b


def _gelu_tanh(x):
    return 0.5 * x * (1.0 + jnp.tanh(math.sqrt(2.0 / math.pi) * (x + 0.044715 * (x * x * x))))


def _lru_body(rg_ref, rx_ref, shift_ref, cw, cbias, wa_ref, ba_ref, wx_ref, bx_ref, lam_ref, o_ref, h_out, c_out,
              h_ref, buf, a_scr, b_scr, o_scr, *, cs, nc):
    c = pl.program_id(1)

    @pl.when(c == 0)
    def _():
        h_ref[...] = jnp.zeros(h_ref.shape, F32)
        buf[...] = jnp.zeros(buf.shape, F32)

    hxc = _conv_chunk(buf, rx_ref[0], shift_ref, cw, cbias, cs)
    a, b = _lru_gates(hxc, wa_ref, ba_ref, wx_ref, bx_ref, lam_ref)
    tiles = (cs // SUBLANES, SUBLANES, LRU_WIDTH)
    a, b = a.reshape(tiles), b.reshape(tiles)
    rowm = lax.broadcasted_iota(jnp.int32, tiles, 1)
    for sh in (1, 2, 4):
        keep = rowm >= sh
        a_prev = jnp.where(keep, pltpu.roll(a, sh, 1), 1.0)
        b_prev = jnp.where(keep, pltpu.roll(b, sh, 1), 0.0)
        b = a * b_prev + b
        a = a * a_prev
    a_scr[...] = a.reshape(cs, LRU_WIDTH)
    b_scr[...] = b.reshape(cs, LRU_WIDTH)
    h = h_ref[...]
    for t in range(cs // SUBLANES):
        rows = pl.ds(t * SUBLANES, SUBLANES)
        ht = a_scr[rows, :] * h + b_scr[rows, :]
        o_scr[rows, :] = ht
        h = jnp.broadcast_to(ht[SUBLANES - 1:SUBLANES, :], (SUBLANES, LRU_WIDTH))
    h_ref[...] = h
    o_ref[0] = (_gelu_tanh(rg_ref[0].astype(F32)) * o_scr[...]).astype(BF16)

    @pl.when(c == nc - 1)
    def _():
        h_out[0] = h[0:1, :]
        c_out[0] = _conv_state(buf)


def _lru_prompt(xl, cw, cbias, wa, ba, wx, bx, lam, *, cs):
    nb, l = xl.shape[0], xl.shape[1]
    nc = l // cs

    def full(shape):
        return pl.BlockSpec(shape, lambda b, c: (0,) * len(shape))

    return pl.pallas_call(
        functools.partial(_lru_body, cs=cs, nc=nc),
        grid=(nb, nc),
        in_specs=[pl.BlockSpec((1, cs, LRU_WIDTH), lambda b, c: (b, c, LRU_RG // LRU_WIDTH)),
                  pl.BlockSpec((1, cs, LRU_WIDTH), lambda b, c: (b, c, LRU_RX // LRU_WIDTH)),
                  full((3 * cs, cs)), full((CONV_W, LRU_WIDTH)), full((1, LRU_WIDTH)),
                  full((LRU_BLOCKS, LRU_BLOCK_W, LRU_BLOCK_W)), full((1, LRU_WIDTH)),
                  full((LRU_BLOCKS, LRU_BLOCK_W, LRU_BLOCK_W)), full((1, LRU_WIDTH)), full((1, LRU_WIDTH))],
        out_specs=[pl.BlockSpec((1, cs, LRU_WIDTH), lambda b, c: (b, c, 0)),
                   pl.BlockSpec((1, 1, LRU_WIDTH), lambda b, c: (b, 0, 0)),
                   pl.BlockSpec((1, CONV_W - 1, LRU_WIDTH), lambda b, c: (b, 0, 0))],
        out_shape=[jax.ShapeDtypeStruct((nb, l, LRU_WIDTH), BF16),
                   jax.ShapeDtypeStruct((nb, 1, LRU_WIDTH), F32),
                   jax.ShapeDtypeStruct((nb, CONV_W - 1, LRU_WIDTH), F32)],
        scratch_shapes=[pltpu.VMEM((SUBLANES, LRU_WIDTH), F32),
                        pltpu.VMEM((SUBLANES, LRU_WIDTH), F32),
                        pltpu.VMEM((cs, LRU_WIDTH), F32),
                        pltpu.VMEM((cs, LRU_WIDTH), F32),
                        pltpu.VMEM((cs, LRU_WIDTH), F32)],
        compiler_params=_params(("parallel", "arbitrary")),
        name="lru_prompt",
    )(xl, xl, _shift_matrix(cs), cw, cbias, wa, ba, wx, bx, lam)


def _l2norm(x):
    return x * lax.rsqrt(jnp.sum(x * x, axis=-1, keepdims=True) + NORM_EPS)


def _cat3(a, b, c, axis):
    return jnp.concatenate([a, b, c], axis=axis)


def _gdn_body(q_ref, k_ref, v_ref, z_ref, sm_ref, shift_ref, cwq, cwk, cwv, alog_ref, dtb_ref, nw_ref,
              o_ref, s_out, cq_out, ck_out, cv_out, s_ref, bufq, bufk, bufv,
              ph_scr, pl_scr, t_scr, qb_scr, kd_scr, aqk_scr, rh_scr, rl_scr, *, cs, nc):
    c = pl.program_id(1)

    @pl.when(c == 0)
    def _():
        s_ref[...] = jnp.zeros(s_ref.shape, F32)
        for buf in (bufq, bufk, bufv):
            buf[...] = jnp.zeros(buf.shape, F32)

    q = _silu_of_half(_conv_chunk(bufq, q_ref[0], shift_ref, cwq, None, cs))
    k = _silu_of_half(_conv_chunk(bufk, k_ref[0], shift_ref, cwk, None, cs))
    v = _silu_of_half(_conv_chunk(bufv, v_ref[0], shift_ref, cwv, None, cs))
    sm = sm_ref[0]
    g_all = -jnp.exp(alog_ref[...]) * _softplus(sm + dtb_ref[...])
    beta_all = _sigmoid(sm)
    lower = _lower_mask(cs)
    strict = _lower_mask(cs, strict=True)
    diag = lower & jnp.logical_not(strict)
    tri = jnp.where(lower, 1.0, 0.0).astype(BF16)
    gc = _dot01_exact(tri, g_all)
    gc_t = gc.T
    egc_all = jnp.exp(gc)

    for h in range(GDN_HEADS):
        hd = slice(h * GDN_DK, (h + 1) * GDN_DK)
        qh = _l2norm(q[:, hd]) * (GDN_DK ** -0.5)
        kh = _l2norm(k[:, hd])
        beta = beta_all[:, GDN_HEADS + h:GDN_HEADS + h + 1]
        gcol = gc[:, h:h + 1]
        dec = jnp.exp(jnp.where(lower, gcol - gc_t[h:h + 1, :], NEG_BIG))
        qb, kb = qh.astype(BF16), kh.astype(BF16)
        both = _dot_nt(jnp.concatenate([kb, qb], axis=0), kb)
        p = -(beta * jnp.where(strict, dec, 0.0) * both[:cs])
        hi, lo = _split2(p)
        ph_scr[h] = hi
        pl_scr[h] = lo
        t_scr[h] = jnp.where(diag, 1.0, p)
        qb_scr[h] = qb
        aqk_scr[h] = (dec * both[cs:]).astype(BF16)
        glast = gc[cs - 1:cs, h:h + 1]
        kd_scr[h] = (kh * jnp.exp(glast - gcol)).astype(BF16)
        rhs = jnp.concatenate([v[:, hd] * beta, kh * (beta * egc_all[:, h:h + 1])], axis=-1)
        hi, lo = _split2(rhs)
        rh_scr[h] = hi
        rl_scr[h] = lo

    levels = int(math.log2(cs))
    for lvl in range(levels):
        for h in range(GDN_HEADS):
            hi, lo = ph_scr[h], pl_scr[h]
            lhs = _cat3(hi, hi, lo, 1)
            if lvl == 0:
                p2 = _dot(lhs, _cat3(hi, lo, hi, 0))
                t_new = None
            else:
                t = t_scr[h]
                th, tl = _split2(t)
                if lvl < levels - 1:
                    rhs = _cat3(jnp.concatenate([hi, th], axis=1), jnp.concatenate([lo, tl], axis=1),
                                jnp.concatenate([hi, th], axis=1), 0)
                    both = _dot(lhs, rhs)
                    p2, t_new = both[:, :cs], t + both[:, cs:]
                else:
                    p2, t_new = None, t + _dot(lhs, _cat3(th, tl, th, 0))
            if p2 is not None:
                hi2, lo2 = _split2(p2)
                ph_scr[h] = hi2
                pl_scr[h] = lo2
            if t_new is not None:
                t_scr[h] = t_new

    z = z_ref[0].astype(F32)
    us, wqs = [], []
    for h in range(GDN_HEADS):
        th, tl = _split2(t_scr[h])
        rh, rl = rh_scr[h], rl_scr[h]
        sol = _dot(_cat3(th, th, tl, 1), _cat3(rh, rl, rh, 0))
        us.append(sol[:, :GDN_DV])
        wqs.append(jnp.concatenate([sol[:, GDN_DV:].astype(BF16), qb_scr[h]], axis=0))
    dbs, qss = [], []
    for h in range(GDN_HEADS):
        ws_qs = _dot(wqs[h], s_ref[h].astype(BF16))
        dbs.append((us[h] - ws_qs[:cs]).astype(BF16))
        qss.append(ws_qs[cs:])
    for h in range(GDN_HEADS):
        hd = slice(h * GDN_DK, (h + 1) * GDN_DK)
        o = egc_all[:, h:h + 1] * qss[h] + _dot(aqk_scr[h], dbs[h])
        s_ref[h] = jnp.exp(gc[cs - 1:cs, h:h + 1]) * s_ref[h] + _dot_tn(kd_scr[h], dbs[h])
        o_ref[0, :, hd] = (_rms(o, nw_ref[...]) * _silu(z[:, hd])).astype(BF16)

    @pl.when(c == nc - 1)
    def _():
        s_out[0] = s_ref[...]
        cq_out[0] = _conv_state(bufq)
        ck_out[0] = _conv_state(bufk)
        cv_out[0] = _conv_state(bufv)


def _gdn_prompt(xg, sm, cw, alog, dtb, normw, *, cs):
    nb, l = xg.shape[0], xg.shape[1]
    nc = l // cs

    def seg(width, off):
        return pl.BlockSpec((1, cs, width), lambda b, c: (b, c, off // width))

    def full(shape):
        return pl.BlockSpec(shape, lambda b, c: (0,) * len(shape))

    def tail(width):
        return pl.BlockSpec((1, CONV_W - 1, width), lambda b, c: (b, 0, 0))

    hm = (GDN_HEADS, cs, cs)
    return pl.pallas_call(
        functools.partial(_gdn_body, cs=cs, nc=nc),
        grid=(nb, nc),
        in_specs=[seg(1024, GDN_Q), seg(1024, GDN_K), seg(1024, GDN_V), seg(1024, GDN_Z),
                  pl.BlockSpec((1, cs, LANES), lambda b, c: (b, c, 0)), full((3 * cs, cs)),
                  full((CONV_W, 1024)), full((CONV_W, 1024)), full((CONV_W, 1024)),
                  full((1, LANES)), full((1, LANES)), full((1, GDN_DV))],
        out_specs=[pl.BlockSpec((1, cs, 1024), lambda b, c: (b, c, 0)),
                   pl.BlockSpec((1, GDN_HEADS, GDN_DK, GDN_DV), lambda b, c: (b, 0, 0, 0)),
                   tail(1024), tail(1024), tail(1024)],
        out_shape=[jax.ShapeDtypeStruct((nb, l, 1024), BF16),
                   jax.ShapeDtypeStruct((nb, GDN_HEADS, GDN_DK, GDN_DV), F32)]
        + [jax.ShapeDtypeStruct((nb, CONV_W - 1, 1024), F32)] * 3,
        scratch_shapes=[pltpu.VMEM((GDN_HEADS, GDN_DK, GDN_DV), F32)]
        + [pltpu.VMEM((SUBLANES, 1024), F32)] * 3
        + [pltpu.VMEM(hm, BF16), pltpu.VMEM(hm, BF16), pltpu.VMEM(hm, F32),
           pltpu.VMEM((GDN_HEADS, cs, GDN_DK), BF16), pltpu.VMEM((GDN_HEADS, cs, GDN_DK), BF16),
           pltpu.VMEM(hm, BF16), pltpu.VMEM((GDN_HEADS, cs, 2 * GDN_DV), BF16),
           pltpu.VMEM((GDN_HEADS, cs, 2 * GDN_DV), BF16)],
        compiler_params=_params(("parallel", "arbitrary")),
        name="gdn_prompt",
    )(xg, xg, xg, xg, sm, _shift_matrix(cs), cw[:, :1024], cw[:, 1024:2048], cw[:, 2048:], alog, dtb, normw)


def _pad_lanes(v, width, offset=0):
    return jnp.zeros((1, width), F32).at[0, offset:offset + v.shape[0]].set(v.astype(F32))


def _prep(ffn_w_in, ffn_w_out, l0_w_in, l0_w_out, gla_w_gate2, gla_b_gate, gla_norm, ssd_conv_w, ssd_conv_b,
          ssd_dt_bias, ssd_a_log, ssd_d, ssd_norm, l1_w_in, l1_w_out, lru_conv_w, lru_conv_b, lru_w_a, lru_b_a,
          lru_w_x, lru_b_x, lru_lambda, gdn_conv_w, gdn_a_log, gdn_dt_bias, gdn_norm):
    w = {}
    w["ffn_in"] = ffn_w_in.astype(BF16)
    w["ffn_out"] = ffn_w_out.astype(BF16)
    c = l0_w_in
    small0 = jnp.concatenate([c[:, 8208:8240], c[:, 3072:3088], jnp.zeros((D_MODEL, LANES - 48), F32)], axis=1)
    w["l0_in"] = jnp.concatenate(
        [c[:, 1024:2048], c[:, 2048:3072], c[:, 0:512], c[:, 512:1024],
         c[:, 3088:5136], c[:, 5136:7184], c[:, 7184:7696], c[:, 7696:8208],
         small0], axis=1).astype(BF16)
    w["l0_out"] = l0_w_out.astype(BF16)
    w["gla_wg2"] = jnp.zeros((LANES, 512), F32).at[32:48].set(gla_w_gate2).astype(BF16)
    w["gla_bg"] = gla_b_gate.reshape(1, 512)
    w["gla_norm"] = gla_norm.reshape(1, GLA_DV)
    w["ssd_cw"] = ssd_conv_w
    w["ssd_cb"] = ssd_conv_b.reshape(1, -1)
    w["ssd_dtb"] = _pad_lanes(ssd_dt_bias, LANES)
    w["ssd_alog"] = _pad_lanes(ssd_a_log, LANES)
    head_of_lane = jnp.arange(SSD_D_INNER) // SSD_HEAD_DIM
    emat = (jnp.arange(LANES)[:, None] == head_of_lane[None, :])
    w["ssd_e"] = emat.astype(BF16)
    w["ssd_et"] = emat.T.astype(BF16)
    w["ssd_dvec"] = jnp.repeat(ssd_d, SSD_HEAD_DIM).reshape(1, SSD_D_INNER)
    w["ssd_norm"] = ssd_norm.reshape(1, SSD_D_INNER)
    c = l1_w_in
    small1 = jnp.concatenate([c[:, 6656:6672], jnp.zeros((D_MODEL, LANES - 16), F32)], axis=1)
    w["l1_in"] = jnp.concatenate(
        [c[:, 2560:3584], c[:, 3584:4608], c[:, 4608:5632], c[:, 5632:6656],
         c[:, 0:1280], c[:, 1280:2560],
         small1], axis=1).astype(BF16)
    w["l1_out"] = l1_w_out.astype(BF16)
    w["lru_cw"] = lru_conv_w
    w["lru_cb"] = lru_conv_b.reshape(1, -1)
    w["lru_wa"] = lru_w_a.astype(BF16)
    w["lru_ba"] = lru_b_a.reshape(1, -1)
    w["lru_wx"] = lru_w_x.astype(BF16)
    w["lru_bx"] = lru_b_x.reshape(1, -1)
    w["lru_lam"] = lru_lambda.reshape(1, -1)
    w["gdn_cw"] = gdn_conv_w
    w["gdn_alog"] = _pad_lanes(gdn_a_log, LANES)
    w["gdn_dtb"] = _pad_lanes(gdn_dt_bias, LANES)
    w["gdn_norm"] = gdn_norm.reshape(1, GDN_DV)
    return w


L0_WIDTHS = (GLA_W, SSD_W, LANES)
L1_WIDTHS = (GDN_W, LRU_W2, LANES)


def _trunk_prompt(x, norms, final_norm, w, *, tm, cs):
    nb, l = x.shape[0], x.shape[1]
    t = nb * l
    h = x.reshape(t, D_MODEL)
    h = _ffn(h, norms[0, 0], w["ffn_in"], w["ffn_out"], (0, 0), tm=tm)
    xa, xb, sm = _proj(h, norms[0, 1], w["l0_in"], L0_WIDTHS, (BF16, BF16, F32), tm=tm)
    sm = sm.reshape(nb, l, LANES)
    o_a, s_gla = _gla_prompt(xa.reshape(nb, l, GLA_W), sm, w["gla_wg2"], w["gla_bg"], w["gla_norm"], lg=2 * cs)
    y, s_ssd, cx, cb, cc = _ssd_prompt(xb.reshape(nb, l, SSD_W), sm, w["ssd_cw"], w["ssd_cb"], w["ssd_dtb"],
                                       w["ssd_alog"], w["ssd_e"], w["ssd_et"], w["ssd_dvec"], w["ssd_norm"], cs=cs)
    h = _ffn(h, norms[0, 2], w["ffn_in"], w["ffn_out"], (0, 1),
             pre=(o_a.reshape(t, 1024), y.reshape(t, 2048), w["l0_out"]), tm=tm)
    h = _ffn(h, norms[1, 0], w["ffn_in"], w["ffn_out"], (1, 0), tm=tm)
    xg, xl, sm = _proj(h, norms[1, 1], w["l1_in"], L1_WIDTHS, (BF16, BF16, F32), tm=tm)
    sm = sm.reshape(nb, l, LANES)
    o_c, s_lru, s_lru_conv = _lru_prompt(xl.reshape(nb, l, LRU_W2), w["lru_cw"], w["lru_cb"], w["lru_wa"],
                                         w["lru_ba"], w["lru_wx"], w["lru_bx"], w["lru_lam"], cs=2 * cs)
    o_d, s_gdn, cq, ck, cv = _gdn_prompt(xg.reshape(nb, l, GDN_W), sm, w["gdn_cw"], w["gdn_alog"], w["gdn_dtb"],
                                         w["gdn_norm"], cs=cs)
    h = _ffn(h, norms[1, 2], w["ffn_in"], w["ffn_out"], (1, 1), final_norm,
             pre=(o_c.reshape(t, LRU_WIDTH), o_d.reshape(t, 1024), w["l1_out"]), tm=tm)
    return (h.reshape(nb, l, D_MODEL), s_gla, s_ssd.reshape(nb, SSD_HEADS, SSD_HEAD_DIM, SSD_D_STATE),
            jnp.concatenate([cx, cb, cc], axis=-1), s_lru.reshape(nb, LRU_WIDTH), s_lru_conv, s_gdn,
            jnp.concatenate([cq, ck, cv], axis=-1))


def _conv_step(cst_ref, cst_out, cur, w_ref, bias_ref, lo, hi):
    acc = w_ref[CONV_W - 1:CONV_W, lo:hi] * cur
    for j in range(CONV_W - 1):
        acc = acc + w_ref[j:j + 1, lo:hi] * cst_ref[:, j, lo:hi]
    if bias_ref is not None:
        acc = acc + bias_ref[:, lo:hi]
    for j in range(CONV_W - 2):
        cst_out[:, j, lo:hi] = cst_ref[:, j + 1, lo:hi]
    cst_out[:, CONV_W - 2, lo:hi] = cur
    return acc


def _gla_dec_body(q_ref, k_ref, v_ref, gg_ref, sm_ref, wg2_ref, bg_ref, nw_ref, s_in, o_ref, s_out, o_scr, *, nb):
    pre = _dot(sm_ref[...].astype(BF16), wg2_ref[...]) + bg_ref[...]
    a = jnp.exp(-_softplus(-pre) * (1.0 / GLA_GATE_NORM))
    q = q_ref[...] * (GLA_DK ** -0.5)
    v = v_ref[...]
    a_t, q_t, k_t = a.T, q.T, k_ref[...].T
    for b in range(nb):
        for h in range(GLA_HEADS):
            dk = slice(h * GLA_DK, (h + 1) * GLA_DK)
            dv = slice(h * GLA_DV, (h + 1) * GLA_DV)
            sn = a_t[dk, b:b + 1] * s_in[b, h] + k_t[dk, b:b + 1] * v[b:b + 1, dv]
            s_out[b, h] = sn
            o_scr[b:b + 1, dv] = jnp.sum(q_t[dk, b:b + 1] * sn, axis=0, keepdims=True)
    o = o_scr[...]
    gg = gg_ref[...]
    for h in range(GLA_HEADS):
        dv = slice(h * GLA_DV, (h + 1) * GLA_DV)
        o_ref[:, dv] = _rms(o[:, dv], nw_ref[...]) * _silu(gg[:, dv])


def _gla_step(xa, sm, state, wg2, bgate, normw, *, nb):
    n = xa.shape[0]

    def seg(width, off):
        return pl.BlockSpec((nb, width), lambda i: (i, off // width))

    def full(shape):
        return pl.BlockSpec(shape, lambda i: (0,) * len(shape))

    st = pl.BlockSpec((nb, GLA_HEADS, GLA_DK, GLA_DV), lambda i: (i, 0, 0, 0))
    return pl.pallas_call(
        functools.partial(_gla_dec_body, nb=nb),
        grid=(n // nb,),
        in_specs=[seg(512, GLA_GQ), seg(512, GLA_GK), seg(1024, GLA_GV), seg(1024, GLA_GG), seg(LANES, 0),
                  full((LANES, 512)), full((1, 512)), full((1, GLA_DV)), st],
        out_specs=[pl.BlockSpec((nb, 1024), lambda i: (i, 0)), st],
        out_shape=[jax.ShapeDtypeStruct((n, 1024), F32), jax.ShapeDtypeStruct(state.shape, F32)],
        scratch_shapes=[pltpu.VMEM((nb, 1024), F32)],
        compiler_params=_params(("parallel",)),
        name="gla_step",
    )(xa, xa, xa, xa, sm, wg2, bgate, normw, state)


def _ssd_dec_body(z_ref, x_ref, b_ref, c_ref, sm_ref, cst_ref, cw, cbias, dtb_ref, alog_ref, e_ref, dvec_ref,
                  nw_ref, s_in, y_ref, s_out, cst_out, yt_scr, *, nb):
    xs = _silu(_conv_step(cst_ref, cst_out, x_ref[...], cw, cbias, 0, 2048))
    bm = _silu(_conv_step(cst_ref, cst_out, b_ref[...], cw, cbias, 2048, 2560))
    cm = _silu(_conv_step(cst_ref, cst_out, c_ref[...], cw, cbias, 2560, 3072)).astype(BF16)
    dt = _softplus(sm_ref[...] + dtb_ref[...])
    decay = jnp.exp(dt * (-jnp.exp(alog_ref[...])))
    dtx = (_dot_x01(dt, e_ref[...]) * xs).astype(BF16)
    lane = lax.broadcasted_iota(jnp.int32, (SSD_GROUP_W, nb), 1)
    seq = lax.broadcasted_iota(jnp.int32, (nb, 1), 0)
    tile = (SSD_HEAD_DIM, SSD_D_STATE)
    for g in range(SSD_GROUPS):
        gn = slice(g * SSD_D_STATE, (g + 1) * SSD_D_STATE)
        gw = slice(g * SSD_GROUP_W, (g + 1) * SSD_GROUP_W)
        rows = pl.ds(g * SSD_GROUP_W, SSD_GROUP_W)
        ycol = jnp.zeros((SSD_GROUP_W, nb), F32)
        for b in range(nb):
            b_only = jnp.where(seq == b, bm[:, gn], 0.0).astype(BF16)
            dec = jnp.concatenate([jnp.broadcast_to(decay[b:b + 1, h:h + 1], tile)
                                   for h in range(g * SSD_HPG, (g + 1) * SSD_HPG)], axis=0)
            sn = dec * s_in[b, rows, :] + _dot_tn(dtx[:, gw], b_only)
            s_out[b, rows, :] = sn
            ycol = jnp.where(lane == b, _dot_nt(sn.astype(BF16), cm[:, gn]), ycol)
        yt_scr[rows, :] = ycol
    y = yt_scr[...].T + dvec_ref[...] * xs
    y = y * _silu(z_ref[...])
    for g in range(SSD_GROUPS):
        gw = slice(g * SSD_GROUP_W, (g + 1) * SSD_GROUP_W)
        y_ref[:, gw] = _rms(y[:, gw], nw_ref[:, gw])


def _ssd_step(xb, sm, state, cstate, cw, cb, dtb, alog, emat, dvec, normw, *, nb):
    n = xb.shape[0]

    def seg(width, off):
        return pl.BlockSpec((nb, width), lambda i: (i, off // width))

    def full(shape):
        return pl.BlockSpec(shape, lambda i: (0,) * len(shape))

    st = pl.BlockSpec((nb, SSD_D_INNER, SSD_D_STATE), lambda i: (i, 0, 0))
    cst = pl.BlockSpec((nb, CONV_W - 1, 3072), lambda i: (i, 0, 0))
    return pl.pallas_call(
        functools.partial(_ssd_dec_body, nb=nb),
        grid=(n // nb,),
        in_specs=[seg(2048, SSD_Z), seg(2048, SSD_X), seg(512, SSD_B), seg(512, SSD_C), seg(LANES, 0), cst,
                  full((CONV_W, 3072)), full((1, 3072)), full((1, LANES)), full((1, LANES)), full((LANES, 2048)),
                  full((1, 2048)), full((1, 2048)), st],
        out_specs=[pl.BlockSpec((nb, 2048), lambda i: (i, 0)), st, cst],
        out_shape=[jax.ShapeDtypeStruct((n, 2048), F32), jax.ShapeDtypeStruct(state.shape, F32),
                   jax.ShapeDtypeStruct(cstate.shape, F32)],
        scratch_shapes=[pltpu.VMEM((SSD_D_INNER, nb), F32)],
        compiler_params=_params(("parallel",)),
        name="ssd_step",
    )(xb, xb, xb, xb, sm, cstate, cw, cb, dtb, alog, emat, dvec, normw, state)


def _lru_dec_body(rg_ref, rx_ref, cst_ref, cw, cbias, wa_ref, ba_ref, wx_ref, bx_ref, lam_ref, h_in,
                  o_ref, h_out, cst_out):
    xc = _conv_step(cst_ref, cst_out, rx_ref[...], cw, cbias, 0, LRU_WIDTH)
    a, b = _lru_gates(0.5 * xc, wa_ref, ba_ref, wx_ref, bx_ref, lam_ref)
    h = a * h_in[...] + b
    h_out[...] = h
    o_ref[...] = _gelu_tanh(rg_ref[...]) * h


def _lru_step(xl, hstate, cstate, cw, cbias, wa, ba, wx, bx, lam):
    n = xl.shape[0]

    def full(shape):
        return pl.BlockSpec(shape, lambda i: (0,) * len(shape))

    return pl.pallas_call(
        _lru_dec_body,
        grid=(1,),
        in_specs=[pl.BlockSpec((n, LRU_WIDTH), lambda i: (0, LRU_RG // LRU_WIDTH)),
                  pl.BlockSpec((n, LRU_WIDTH), lambda i: (0, LRU_RX // LRU_WIDTH)),
                  full((n, CONV_W - 1, LRU_WIDTH)), full((CONV_W, LRU_WIDTH)), full((1, LRU_WIDTH)),
                  full((LRU_BLOCKS, LRU_BLOCK_W, LRU_BLOCK_W)), full((1, LRU_WIDTH)),
                  full((LRU_BLOCKS, LRU_BLOCK_W, LRU_BLOCK_W)), full((1, LRU_WIDTH)), full((1, LRU_WIDTH)),
                  full((n, LRU_WIDTH))],
        out_specs=[full((n, LRU_WIDTH)), full((n, LRU_WIDTH)), full((n, CONV_W - 1, LRU_WIDTH))],
        out_shape=[jax.ShapeDtypeStruct((n, LRU_WIDTH), F32), jax.ShapeDtypeStruct((n, LRU_WIDTH), F32),
                   jax.ShapeDtypeStruct(cstate.shape, F32)],
        compiler_params=_params(("arbitrary",)),
        name="lru_step",
    )(xl, xl, cstate, cw, cbias, wa, ba, wx, bx, lam, hstate)


def _gdn_dec_body(q_ref, k_ref, v_ref, z_ref, sm_ref, cst_ref, cw, alog_ref, dtb_ref, nw_ref, s_in,
                  o_ref, s_out, cst_out, o_scr, *, nb):
    q = _silu(_conv_step(cst_ref, cst_out, q_ref[...], cw, None, 0, 1024))
    k = _silu(_conv_step(cst_ref, cst_out, k_ref[...], cw, None, 1024, 2048))
    v = _silu(_conv_step(cst_ref, cst_out, v_ref[...], cw, None, 2048, 3072))
    sm = sm_ref[...]
    eg_all = jnp.exp(-jnp.exp(alog_ref[...]) * _softplus(sm + dtb_ref[...]))
    beta_all = _sigmoid(sm)
    qn = jnp.concatenate([_l2norm(q[:, h * GDN_DK:(h + 1) * GDN_DK]) for h in range(GDN_HEADS)], axis=-1)
    qn = qn * (GDN_DK ** -0.5)
    kn = jnp.concatenate([_l2norm(k[:, h * GDN_DK:(h + 1) * GDN_DK]) for h in range(GDN_HEADS)], axis=-1)
    kq = jnp.concatenate([kn, qn], axis=0).astype(BF16)
    seq = lax.broadcasted_iota(jnp.int32, (nb, 1), 0)
    pairs = [(b, h) for b in range(nb) for h in range(GDN_HEADS)]
    ksqs = [_dot(kq[:, h * GDN_DK:(h + 1) * GDN_DK], s_in[b, h].astype(BF16)) for b, h in pairs]
    for (b, h), kq_s in zip(pairs, ksqs):
        hd = slice(h * GDN_DK, (h + 1) * GDN_DK)
        eg = eg_all[b:b + 1, h:h + 1]
        beta = beta_all[b:b + 1, GDN_HEADS + h:GDN_HEADS + h + 1]
        delta = beta * (v[b:b + 1, hd] - eg * kq_s[b:b + 1])
        qk = jnp.sum(qn[b:b + 1, hd] * kn[b:b + 1, hd], axis=-1, keepdims=True)
        o_scr[b:b + 1, hd] = eg * kq_s[nb + b:nb + b + 1] + qk * delta
        d_only = jnp.where(seq == b, jnp.broadcast_to(delta, (nb, GDN_DV)), 0.0).astype(BF16)
        s_out[b, h] = eg * s_in[b, h] + _dot_tn(kq[:nb, hd], d_only)
    o = o_scr[...]
    z = z_ref[...]
    for h in range(GDN_HEADS):
        hd = slice(h * GDN_DK, (h + 1) * GDN_DK)
        o_ref[:, hd] = _rms(o[:, hd], nw_ref[...]) * _silu(z[:, hd])


def _gdn_step(xg, sm, state, cstate, cw, alog, dtb, normw, *, nb):
    n = xg.shape[0]

    def seg(width, off):
        return pl.BlockSpec((nb, width), lambda i: (i, off // width))

    def full(shape):
        return pl.BlockSpec(shape, lambda i: (0,) * len(shape))

    st = pl.BlockSpec((nb, GDN_HEADS, GDN_DK, GDN_DV), lambda i: (i, 0, 0, 0))
    cst = pl.BlockSpec((nb, CONV_W - 1, 3072), lambda i: (i, 0, 0))
    return pl.pallas_call(
        functools.partial(_gdn_dec_body, nb=nb),
        grid=(n // nb,),
        in_specs=[seg(1024, GDN_Q), seg(1024, GDN_K), seg(1024, GDN_V), seg(1024, GDN_Z), seg(LANES, 0), cst,
                  full((CONV_W, 3072)), full((1, LANES)), full((1, LANES)), full((1, GDN_DV)), st],
        out_specs=[pl.BlockSpec((nb, 1024), lambda i: (i, 0)), st, cst],
        out_shape=[jax.ShapeDtypeStruct((n, 1024), F32), jax.ShapeDtypeStruct(state.shape, F32),
                   jax.ShapeDtypeStruct(cstate.shape, F32)],
        scratch_shapes=[pltpu.VMEM((nb, 1024), F32)],
        compiler_params=_params(("parallel",)),
        name="gdn_step",
    )(xg, xg, xg, xg, sm, cstate, cw, alog, dtb, normw, state)


def _trunk_sample(x, s_gla, s_ssd, s_ssd_conv, s_lru, s_lru_conv, s_gdn, s_gdn_conv, norms, final_norm, w, *, nb=8):
    n = x.shape[0]
    f3 = (F32, F32, F32)
    h = x.reshape(n, D_MODEL)
    h = _ffn(h, norms[0, 0], w["ffn_in"], w["ffn_out"], (0, 0), tm=n)
    xa, xb, sm = _proj(h, norms[0, 1], w["l0_in"], L0_WIDTHS, f3, tm=n)
    o_a, n_gla = _gla_step(xa, sm, s_gla, w["gla_wg2"], w["gla_bg"], w["gla_norm"], nb=nb)
    y, n_ssd, n_ssd_conv = _ssd_step(xb, sm, s_ssd.reshape(n, SSD_D_INNER, SSD_D_STATE), s_ssd_conv, w["ssd_cw"],
                                     w["ssd_cb"], w["ssd_dtb"], w["ssd_alog"], w["ssd_e"], w["ssd_dvec"],
                                     w["ssd_norm"], nb=nb)
    h = _ffn(h, norms[0, 2], w["ffn_in"], w["ffn_out"], (0, 1), pre=(o_a, y, w["l0_out"]), tm=n)
    h = _ffn(h, norms[1, 0], w["ffn_in"], w["ffn_out"], (1, 0), tm=n)
    xg, xl, sm = _proj(h, norms[1, 1], w["l1_in"], L1_WIDTHS, f3, tm=n)
    o_c, n_lru, n_lru_conv = _lru_step(xl, s_lru, s_lru_conv, w["lru_cw"], w["lru_cb"], w["lru_wa"], w["lru_ba"],
                                       w["lru_wx"], w["lru_bx"], w["lru_lam"])
    o_d, n_gdn, n_gdn_conv = _gdn_step(xg, sm, s_gdn, s_gdn_conv, w["gdn_cw"], w["gdn_alog"], w["gdn_dtb"],
                                       w["gdn_norm"], nb=nb)
    h = _ffn(h, norms[1, 2], w["ffn_in"], w["ffn_out"], (1, 1), final_norm,
             pre=(o_c, o_d, w["l1_out"]), tm=n)
    return (h.reshape(n, 1, D_MODEL), n_gla, n_ssd.reshape(s_ssd.shape), n_ssd_conv, n_lru, n_lru_conv, n_gdn,
            n_gdn_conv)


def kernel(x_prompt, x_sample, state_gla, state_ssd, state_ssd_conv, state_lru, state_lru_conv, state_gdn,
           state_gdn_conv, norms, final_norm, ffn_w_in, ffn_w_out, l0_w_in, l0_w_out, gla_w_gate2, gla_b_gate,
           gla_norm, ssd_conv_w, ssd_conv_b, ssd_dt_bias, ssd_a_log, ssd_d, ssd_norm, l1_w_in, l1_w_out,
           lru_conv_w, lru_conv_b, lru_w_a, lru_b_a, lru_w_x, lru_b_x, lru_lambda, gdn_conv_w, gdn_a_log,
           gdn_dt_bias, gdn_norm):
    w = _prep(ffn_w_in, ffn_w_out, l0_w_in, l0_w_out, gla_w_gate2, gla_b_gate, gla_norm, ssd_conv_w, ssd_conv_b,
              ssd_dt_bias, ssd_a_log, ssd_d, ssd_norm, l1_w_in, l1_w_out, lru_conv_w, lru_conv_b, lru_w_a, lru_b_a,
              lru_w_x, lru_b_x, lru_lambda, gdn_conv_w, gdn_a_log, gdn_dt_bias, gdn_norm)
    prompt = _trunk_prompt(x_prompt, norms, final_norm, w, tm=512, cs=128)
    sample = _trunk_sample(x_sample, state_gla, state_ssd, state_ssd_conv, state_lru, state_lru_conv, state_gdn,
                           state_gdn_conv, norms, final_norm, w)
    return (prompt[0], sample[0]) + tuple(prompt[1:]) + tuple(sample[1:])
```

```python
import functools
import math

import jax
import jax.numpy as jnp
from jax import lax
from jax.experimental import pallas as pl
from jax.experimental.pallas import tpu as pltpu

F32 = jnp.float32
BF16 = jnp.bfloat16

D_MODEL = 1024
NORM_EPS = 1e-6
CONV_W = 4
D_FF = 2816
FFN_RES = 0.5

GLA_HEADS = 4
GLA_DK = 128
GLA_DV = 256
GLA_RANK = 16
GLA_GATE_NORM = 16.0
GLA_SUB = 16
GLA_FAST_MAX = 60.0

SSD_D_INNER = 2048
SSD_HEAD_DIM = 64
SSD_HEADS = 32
SSD_GROUPS = 4
SSD_HPG = 8
SSD_D_STATE = 128
SSD_GROUP_W = SSD_HPG * SSD_HEAD_DIM

LRU_WIDTH = 1280
LRU_BLOCKS = 10
LRU_BLOCK_W = 128
LRU_C = 8.0

GDN_HEADS = 8
GDN_DK = 128
GDN_DV = 128

LANES = 128
SUBLANES = 8
NEG_BIG = -1e30
VMEM_LIMIT = 56 * 1024 * 1024

GLA_W = 3072
GLA_GV, GLA_GG, GLA_GQ, GLA_GK = 0, 1024, 2048, 2560
SSD_W = 5120
SSD_Z, SSD_X, SSD_B, SSD_C = 0, 2048, 4096, 4608
GDN_W = 4096
GDN_Q, GDN_K, GDN_V, GDN_Z = 0, 1024, 2048, 3072
LRU_W2 = 2560
LRU_RG, LRU_RX = 0, 1280


def _sigmoid(x):
    return 0.5 * jnp.tanh(0.5 * x) + 0.5


def _silu(x):
    return _silu_of_half(0.5 * x)


def _silu_of_half(hx):
    return hx * jnp.tanh(hx) + hx


def _softplus(x):
    return jnp.maximum(x, 0.0) + jnp.log(1.0 + jnp.exp(-jnp.abs(x)))


def _rms(x, g):
    return x * lax.rsqrt(jnp.mean(x * x, axis=-1, keepdims=True) + NORM_EPS) * g


def _dot(a, b):
    return jnp.dot(a, b, preferred_element_type=F32)


def _dot_nt(a, b):
    return lax.dot_general(a, b, (((1,), (1,)), ((), ())), preferred_element_type=F32)


def _dot_tn(a, b):
    return lax.dot_general(a, b, (((0,), (0,)), ((), ())), preferred_element_type=F32)


def _split2(x):
    hi = x.astype(BF16)
    lo = (x - hi.astype(F32)).astype(BF16)
    return hi, lo


def _split3(x):
    hi = x.astype(BF16)
    r = x - hi.astype(F32)
    mid = r.astype(BF16)
    lo = (r - mid.astype(F32)).astype(BF16)
    return hi, mid, lo


def _dot01_exact(m01, x):
    hi, mid, lo = _split3(x)
    return _dot(m01, hi) + _dot(m01, mid) + _dot(m01, lo)


def _dot_x01(x, m01):
    hi, lo = _split2(x)
    return _dot(hi, m01) + _dot(lo, m01)


def _lower_mask(n, strict=False):
    row = lax.broadcasted_iota(jnp.int32, (n, n), 0)
    col = lax.broadcasted_iota(jnp.int32, (n, n), 1)
    return row > col if strict else row >= col


def _params(sem):
    return pltpu.CompilerParams(dimension_semantics=sem, vmem_limit_bytes=VMEM_LIMIT)


FFN_CHUNK = 256


def _resident(shape):
    return pl.BlockSpec(shape, lambda i: (0,) * len(shape), pipeline_mode=pl.Buffered(1))


def _ffn_body(*refs, final, pre):
    refs = list(refs)
    x_ref = refs.pop(0)
    if pre:
        a_ref, b_ref, wab_ref = refs[:3]
        refs = refs[3:]
    g_ref, wi_ref, wo_ref = refs[:3]
    refs = refs[3:]
    fg_ref = refs.pop(0) if final else None
    o_ref, act_ref = refs
    x = x_ref[...]
    if pre:
        ka = a_ref.shape[1]
        x = x + _dot(a_ref[...].astype(BF16), wab_ref[:ka, :]) + _dot(b_ref[...].astype(BF16), wab_ref[ka:, :])
    hn = _rms(x, g_ref[...]).astype(BF16)
    for c in range(D_FF // FFN_CHUNK):
        cols = slice(c * FFN_CHUNK, (c + 1) * FFN_CHUNK)
        gate = _dot(hn, wi_ref[:, cols])
        up = _dot(hn, wi_ref[:, D_FF + c * FFN_CHUNK:D_FF + (c + 1) * FFN_CHUNK])
        act_ref[:, cols] = (_silu(gate) * up).astype(BF16)
    y = x + FFN_RES * _dot(act_ref[...], wo_ref[...])
    if final:
        y = _rms(y, fg_ref[...])
    o_ref[...] = y


def _ffn(x, g, w_in, w_out, sel, final_g=None, pre=None, *, tm):
    t = x.shape[0]
    final = final_g is not None
    in_specs = [pl.BlockSpec((tm, D_MODEL), lambda i: (i, 0))]
    args = [x]
    if pre is not None:
        a, b, wab = pre
        in_specs += [pl.BlockSpec((tm, a.shape[1]), lambda i: (i, 0)), pl.BlockSpec((tm, b.shape[1]), lambda i: (i, 0)),
                     _resident(wab.shape)]
        args += [a, b, wab]
    in_specs += [_resident((1, D_MODEL)),
                 pl.BlockSpec((None, None) + w_in.shape[2:], lambda i: sel + (0, 0), pipeline_mode=pl.Buffered(1)),
                 pl.BlockSpec((None, None) + w_out.shape[2:], lambda i: sel + (0, 0), pipeline_mode=pl.Buffered(1))]
    args += [g.reshape(1, D_MODEL), w_in, w_out]
    if final:
        in_specs.append(_resident((1, D_MODEL)))
        args.append(final_g.reshape(1, D_MODEL))
    return pl.pallas_call(
        functools.partial(_ffn_body, final=final, pre=pre is not None),
        grid=(t // tm,),
        in_specs=in_specs,
        out_specs=pl.BlockSpec((tm, D_MODEL), lambda i: (i, 0)),
        out_shape=jax.ShapeDtypeStruct((t, D_MODEL), F32),
        scratch_shapes=[pltpu.VMEM((tm, D_FF), BF16)],
        compiler_params=_params(("parallel",)),
        name="ffn",
    )(*args)


PROJ_CHUNK = 1024


def _proj_body(x_ref, g_ref, w_ref, *o_refs, widths):
    hn = _rms(x_ref[...], g_ref[...]).astype(BF16)
    off = 0
    for o_ref, width in zip(o_refs, widths):
        for c in range(0, width, PROJ_CHUNK):
            n = min(PROJ_CHUNK, width - c)
            o_ref[:, c:c + n] = _dot(hn, w_ref[:, off + c:off + c + n]).astype(o_ref.dtype)
        off += width


def _proj(x, g, w, widths, dtypes, *, tm):
    t = x.shape[0]
    return pl.pallas_call(
        functools.partial(_proj_body, widths=tuple(widths)),
        grid=(t // tm,),
        in_specs=[pl.BlockSpec((tm, D_MODEL), lambda i: (i, 0)), _resident((1, D_MODEL)), _resident(w.shape)],
        out_specs=[pl.BlockSpec((tm, n), lambda i: (i, 0)) for n in widths],
        out_shape=[jax.ShapeDtypeStruct((t, n), dt) for n, dt in zip(widths, dtypes)],
        compiler_params=_params(("parallel",)),
        name="proj",
    )(x, g.reshape(1, D_MODEL), w)


def _shift_matrix(c):
    out_row = jnp.arange(3 * c)
    k = out_row // c + 1
    src = out_row % c - k
    return (src[:, None] == jnp.arange(c)[None, :]).astype(BF16)


def _conv_chunk(tail, src, shift_ref, w_ref, bias_ref, c):
    x = src.astype(F32)
    sh = _dot(shift_ref[...], src)
    w0, w1, w2, w3 = (0.5 * w_ref[j:j + 1, :] for j in range(CONV_W))
    acc = w3 * x + w2 * sh[:c] + w1 * sh[c:2 * c] + w0 * sh[2 * c:]
    if bias_ref is not None:
        acc = acc + 0.5 * bias_ref[...]
    t3, t2, t1 = tail[5:6, :], tail[6:7, :], tail[7:8, :]
    row = lax.broadcasted_iota(jnp.int32, (SUBLANES, x.shape[1]), 0)
    head = jnp.where(row == 0, w0 * t3 + w1 * t2 + w2 * t1,
                     jnp.where(row == 1, w0 * t2 + w1 * t1, jnp.where(row == 2, w0 * t1, 0.0)))
    tail[pl.ds(5, 3), :] = x[c - 3:c, :]
    return jnp.concatenate([acc[:SUBLANES] + head, acc[SUBLANES:]], axis=0)


def _conv_state(tail):
    return tail[pl.ds(5, 3), :]


def _gla_body(q_in, k_in, v_in, gg_ref, sm_ref, wg2_ref, bg_ref, nw_ref, o_ref, s_out, st_ref, b_ref,
              q_ref, k_ref, v_ref, *, lg, nc):
    c = pl.program_id(1)

    @pl.when(c == 0)
    def _():
        st_ref[...] = jnp.zeros(st_ref.shape, F32)

    pre = _dot(sm_ref[0].astype(BF16), wg2_ref[...]) + bg_ref[...]
    log_a = -_softplus(-pre) * (1.0 / GLA_GATE_NORM)
    lower = _lower_mask(lg)
    b_all = _dot01_exact(jnp.where(lower, 1.0, 0.0).astype(BF16), log_a)
    b_ref[...] = b_all
    fast = jnp.max(-b_all[lg - 1:lg, :]) <= GLA_FAST_MAX

    q = q_in[0].astype(F32) * (GLA_DK ** -0.5)
    k = k_in[0].astype(F32)
    vb = v_in[0]
    blast = b_all[lg - 1:lg, :]
    qe = (q * jnp.exp(b_all)).astype(BF16)
    ke = (k * jnp.exp(-b_all)).astype(BF16)
    kd = (k * jnp.exp(blast - b_all)).astype(BF16)
    elast = jnp.exp(blast)
    gg = gg_ref[0].astype(F32)
    for h in range(GLA_HEADS):
        dk = slice(h * GLA_DK, (h + 1) * GLA_DK)
        dv = slice(h * GLA_DV, (h + 1) * GLA_DV)
        att = jnp.where(lower, _dot_nt(qe[:, dk], ke[:, dk]), 0.0).astype(BF16)
        st = st_ref[h]
        oh = _dot(att, vb[:, dv]) + _dot_nt(qe[:, dk], st.astype(BF16))
        st_ref[h] = jnp.where(fast, st * elast[:, dk] + _dot_tn(vb[:, dv], kd[:, dk]), st)
        o_ref[0, :, dv] = (_rms(oh, nw_ref[...]) * _silu(gg[:, dv])).astype(BF16)

    @pl.when(jnp.logical_not(fast))
    def _():
        q_ref[...] = q_in[0].astype(F32) * (GLA_DK ** -0.5)
        k_ref[...] = k_in[0].astype(F32)
        v_ref[...] = v_in[0].astype(F32)
        rowi = lax.broadcasted_iota(jnp.int32, (GLA_SUB, 1), 0)

        def block(i, carry):
            base = pl.multiple_of(i * GLA_SUB, GLA_SUB)
            q = q_ref[pl.ds(base, GLA_SUB), :]
            k = k_ref[pl.ds(base, GLA_SUB), :]
            v = v_ref[pl.ds(base, GLA_SUB), :]
            b = b_ref[pl.ds(base, GLA_SUB), :]
            o = [jnp.zeros((GLA_SUB, GLA_DV), F32) for _ in range(GLA_HEADS)]
            for s in range(GLA_SUB):
                ks = k_ref[pl.ds(base + s, 1), :]
                bs = b_ref[pl.ds(base + s, 1), :]
                vs = v_ref[pl.ds(base + s, 1), :]
                p = q * ks * jnp.exp(b - bs)
                for h in range(GLA_HEADS):
                    att = jnp.sum(p[:, h * GLA_DK:(h + 1) * GLA_DK], axis=-1, keepdims=True)
                    att = jnp.where(rowi >= s, att, 0.0)
                    o[h] = o[h] + att * vs[:, h * GLA_DV:(h + 1) * GLA_DV]
            bprev = jnp.where(i > 0, b_ref[pl.ds(jnp.maximum(base - 1, 0), 1), :], 0.0)
            blast = b_ref[pl.ds(base + GLA_SUB - 1, 1), :]
            qe = (q * jnp.exp(b - bprev)).astype(BF16)
            kd = (k * jnp.exp(blast - b)).astype(BF16)
            elast = jnp.exp(blast - bprev)
            vb = v.astype(BF16)
            gg = gg_ref[0, pl.ds(base, GLA_SUB), :].astype(F32)
            for h in range(GLA_HEADS):
                dk = slice(h * GLA_DK, (h + 1) * GLA_DK)
                dv = slice(h * GLA_DV, (h + 1) * GLA_DV)
                st = st_ref[h]
                oh = o[h] + _dot_nt(qe[:, dk], st.astype(BF16))
                st_ref[h] = st * elast[:, dk] + _dot_tn(vb[:, dv], kd[:, dk])
                o_ref[0, pl.ds(base, GLA_SUB), dv] = (_rms(oh, nw_ref[...]) * _silu(gg[:, dv])).astype(BF16)
            return carry

        lax.fori_loop(0, lg // GLA_SUB, block, 0)

    @pl.when(c == nc - 1)
    def _():
        for h in range(GLA_HEADS):
            s_out[0, h] = st_ref[h].T


def _gla_prompt(xa, sm, wg2, bgate, normw, *, lg):
    nb, l = xa.shape[0], xa.shape[1]
    nc = l // lg

    def seg(width, off):
        return pl.BlockSpec((1, lg, width), lambda b, c: (b, c, off // width))

    def full(shape):
        return pl.BlockSpec(shape, lambda b, c: (0,) * len(shape))

    return pl.pallas_call(
        functools.partial(_gla_body, lg=lg, nc=nc),
        grid=(nb, nc),
        in_specs=[seg(512, GLA_GQ), seg(512, GLA_GK), seg(1024, GLA_GV), seg(1024, GLA_GG),
                  pl.BlockSpec((1, lg, LANES), lambda b, c: (b, c, 0)),
                  full((LANES, 512)), full((1, 512)), full((1, GLA_DV))],
        out_specs=[pl.BlockSpec((1, lg, 1024), lambda b, c: (b, c, 0)),
                   pl.BlockSpec((1, GLA_HEADS, GLA_DK, GLA_DV), lambda b, c: (b, 0, 0, 0))],
        out_shape=[jax.ShapeDtypeStruct((nb, l, 1024), BF16),
                   jax.ShapeDtypeStruct((nb, GLA_HEADS, GLA_DK, GLA_DV), F32)],
        scratch_shapes=[pltpu.VMEM((GLA_HEADS, GLA_DV, GLA_DK), F32), pltpu.VMEM((lg, 512), F32),
                        pltpu.VMEM((lg, 512), F32), pltpu.VMEM((lg, 512), F32), pltpu.VMEM((lg, 1024), F32)],
        compiler_params=_params(("parallel", "arbitrary")),
        name="gla_prompt",
    )(xa, xa, xa, xa, sm, wg2, bgate, normw)


def _ssd_body(z_ref, x_ref, b_ref, c_ref, sm_ref, shift_ref, cwx, cbx, cwb, cbb, cwc, cbc, dtb_ref, alog_ref, e_ref, et_ref,
              dvec_ref, nw_ref, y_ref, s_out, cx_out, cb_out, cc_out, s_ref, bufx, bufb, bufc, y_scr, *, cs, nc):
    c = pl.program_id(1)

    @pl.when(c == 0)
    def _():
        s_ref[...] = jnp.zeros(s_ref.shape, F32)
        for buf in (bufx, bufb, bufc):
            buf[...] = jnp.zeros(buf.shape, F32)

    xs = _silu_of_half(_conv_chunk(bufx, x_ref[0], shift_ref, cwx, cbx, cs))
    bm = _silu_of_half(_conv_chunk(bufb, b_ref[0], shift_ref, cwb, cbb, cs)).astype(BF16)
    cm = _silu_of_half(_conv_chunk(bufc, c_ref[0], shift_ref, cwc, cbc, cs)).astype(BF16)

    dt = _softplus(sm_ref[0] + dtb_ref[...])
    da = dt * (-jnp.exp(alog_ref[...]))
    lower = _lower_mask(cs)
    tri = jnp.where(lower, 1.0, 0.0).astype(BF16)
    cum = _dot01_exact(tri, da)
    cum_t = cum.T
    dt_t = dt.T
    last = cum[cs - 1:cs, :]
    ecum_x = _dot_x01(jnp.exp(cum), e_ref[...])
    w_x = _dot_x01(jnp.exp(last - cum) * dt, e_ref[...])
    elast = jnp.broadcast_to(jnp.exp(last), (SUBLANES, LANES))
    eh, el = _split2(elast)
    ecol = _dot_nt(et_ref[...], eh) + _dot_nt(et_ref[...], el)

    xs_bf = xs.astype(BF16)
    xw = (xs * w_x).astype(BF16)
    z = z_ref[0].astype(F32)
    lane = lax.broadcasted_iota(jnp.int32, (cs, LANES), 1)
    for g in range(SSD_GROUPS):
        gn = slice(g * SSD_D_STATE, (g + 1) * SSD_D_STATE)
        gw = slice(g * SSD_GROUP_W, (g + 1) * SSD_GROUP_W)
        cg, bg = cm[:, gn], bm[:, gn]
        cb = _dot_nt(cg, bg)
        sg = s_ref[pl.ds(g * SSD_GROUP_W, SSD_GROUP_W), :]
        y_inter = _dot_nt(cg, sg.astype(BF16)) * ecum_x[:, gw]
        ss = jnp.zeros((cs, 1), F32)
        for p in range(SSD_HPG // 2):
            pair = slice(g * SSD_GROUP_W + p * LANES, g * SSD_GROUP_W + (p + 1) * LANES)
            xp = xs_bf[:, pair]
            ys = []
            for h in (g * SSD_HPG + 2 * p, g * SSD_HPG + 2 * p + 1):
                seg = jnp.exp(jnp.where(lower, cum[:, h:h + 1] - cum_t[h:h + 1, :], NEG_BIG))
                m = cb * seg * dt_t[h:h + 1, :]
                ys.append(_dot(m.astype(BF16), xp))
            yp = jnp.where(lane < SSD_HEAD_DIM, ys[0], ys[1])
            yp = yp + y_inter[:, p * LANES:(p + 1) * LANES] + dvec_ref[:, pair] * xs[:, pair]
            yp = yp * _silu(z[:, pair])
            ss = ss + jnp.sum(yp * yp, axis=-1, keepdims=True)
            y_scr[:, pair] = yp
        inv = lax.rsqrt(ss * (1.0 / SSD_GROUP_W) + NORM_EPS)
        y_ref[0, :, gw] = (y_scr[:, gw] * inv * nw_ref[:, gw]).astype(BF16)
        s_ref[pl.ds(g * SSD_GROUP_W, SSD_GROUP_W), :] = (
            ecol[g * SSD_GROUP_W:(g + 1) * SSD_GROUP_W, 0:1] * sg + _dot_tn(xw[:, gw], bg))

    @pl.when(c == nc - 1)
    def _():
        s_out[0] = s_ref[...]
        cx_out[0] = _conv_state(bufx)
        cb_out[0] = _conv_state(bufb)
        cc_out[0] = _conv_state(bufc)


def _ssd_prompt(xb, sm, cw, cb, dtb, alog, emat, emat_t, dvec, normw, *, cs):
    nb, l = xb.shape[0], xb.shape[1]
    nc = l // cs

    def seg(width, off):
        return pl.BlockSpec((1, cs, width), lambda b, c: (b, c, off // width))

    def full(shape):
        return pl.BlockSpec(shape, lambda b, c: (0,) * len(shape))

    def tail(width):
        return pl.BlockSpec((1, CONV_W - 1, width), lambda b, c: (b, 0, 0))

    cwx, cwb, cwc = cw[:, :2048], cw[:, 2048:2560], cw[:, 2560:]
    cbx, cbb, cbc = cb[:, :2048], cb[:, 2048:2560], cb[:, 2560:]
    return pl.pallas_call(
        functools.partial(_ssd_body, cs=cs, nc=nc),
        grid=(nb, nc),
        in_specs=[seg(2048, SSD_Z), seg(2048, SSD_X), seg(512, SSD_B), seg(512, SSD_C),
                  pl.BlockSpec((1, cs, LANES), lambda b, c: (b, c, 0)), full((3 * cs, cs)),
                  full((CONV_W, 2048)), full((1, 2048)), full((CONV_W, 512)), full((1, 512)),
                  full((CONV_W, 512)), full((1, 512)), full((1, LANES)), full((1, LANES)),
                  full((LANES, 2048)), full((2048, LANES)), full((1, 2048)), full((1, 2048))],
        out_specs=[pl.BlockSpec((1, cs, 2048), lambda b, c: (b, c, 0)),
                   pl.BlockSpec((1, 2048, SSD_D_STATE), lambda b, c: (b, 0, 0)),
                   tail(2048), tail(512), tail(512)],
        out_shape=[jax.ShapeDtypeStruct((nb, l, 2048), BF16),
                   jax.ShapeDtypeStruct((nb, 2048, SSD_D_STATE), F32),
                   jax.ShapeDtypeStruct((nb, CONV_W - 1, 2048), F32),
                   jax.ShapeDtypeStruct((nb, CONV_W - 1, 512), F32),
                   jax.ShapeDtypeStruct((nb, CONV_W - 1, 512), F32)],
        scratch_shapes=[pltpu.VMEM((2048, SSD_D_STATE), F32),
                        pltpu.VMEM((SUBLANES, 2048), F32),
                        pltpu.VMEM((SUBLANES, 512), F32),
                        pltpu.VMEM((SUBLANES, 512), F32),
                        pltpu.VMEM((cs, 2048), F32)],
        compiler_params=_params(("parallel", "arbitrary")),
        name="ssd_prompt",
    )(xb, xb, xb, xb, sm, _shift_matrix(cs), cwx, cbx, cwb, cbb, cwc, cbc, dtb, alog, emat, emat_t, dvec, normw)


def _lru_gates(hxc, wa_ref, ba_ref, wx_ref, bx_ref, lam_ref):
    xb = hxc.astype(BF16)
    rl, il = [], []
    for n in range(LRU_BLOCKS):
        blk = xb[:, n * LRU_BLOCK_W:(n + 1) * LRU_BLOCK_W]
        rl.append(_dot(blk, wa_ref[n]))
        il.append(_dot(blk, wx_ref[n]))
    tr = jnp.tanh(jnp.concatenate(rl, axis=-1) + 0.5 * ba_ref[...])
    ti = jnp.tanh(jnp.concatenate(il, axis=-1) + 0.5 * bx_ref[...])
    rate = (-0.5 * LRU_C * math.log2(math.e)) * _softplus(-lam_ref[...])
    a = jnp.exp2(rate * (tr + 1.0))
    b = jnp.sqrt(1.0 - a * a) * (hxc * ti + hxc)
    return a, b


def _gelu_tanh(x):
    return 0.5 * x * (1.0 + jnp.tanh(math.sqrt(2.0 / math.pi) * (x + 0.044715 * (x * x * x))))


def _lru_body(rg_ref, rx_ref, shift_ref, cw, cbias, wa_ref, ba_ref, wx_ref, bx_ref, lam_ref, o_ref, h_out, c_out,
              h_ref, buf, a_scr, b_scr, o_scr, *, cs, nc):
    c = pl.program_id(1)

    @pl.when(c == 0)
    def _():
        h_ref[...] = jnp.zeros(h_ref.shape, F32)
        buf[...] = jnp.zeros(buf.shape, F32)

    hxc = _conv_chunk(buf, rx_ref[0], shift_ref, cw, cbias, cs)
    a, b = _lru_gates(hxc, wa_ref, ba_ref, wx_ref, bx_ref, lam_ref)
    tiles = (cs // SUBLANES, SUBLANES, LRU_WIDTH)
    a, b = a.reshape(tiles), b.reshape(tiles)
    rowm = lax.broadcasted_iota(jnp.int32, tiles, 1)
    for sh in (1, 2, 4):
        keep = rowm >= sh
        a_prev = jnp.where(keep, pltpu.roll(a, sh, 1), 1.0)
        b_prev = jnp.where(keep, pltpu.roll(b, sh, 1), 0.0)
        b = a * b_prev + b
        a = a * a_prev
    a_scr[...] = a.reshape(cs, LRU_WIDTH)
    b_scr[...] = b.reshape(cs, LRU_WIDTH)
    h = h_ref[...]
    for t in range(cs // SUBLANES):
        rows = pl.ds(t * SUBLANES, SUBLANES)
        ht = a_scr[rows, :] * h + b_scr[rows, :]
        o_scr[rows, :] = ht
        h = jnp.broadcast_to(ht[SUBLANES - 1:SUBLANES, :], (SUBLANES, LRU_WIDTH))
    h_ref[...] = h
    o_ref[0] = (_gelu_tanh(rg_ref[0].astype(F32)) * o_scr[...]).astype(BF16)

    @pl.when(c == nc - 1)
    def _():
        h_out[0] = h[0:1, :]
        c_out[0] = _conv_state(buf)


def _lru_prompt(xl, cw, cbias, wa, ba, wx, bx, lam, *, cs):
    nb, l = xl.shape[0], xl.shape[1]
    nc = l // cs

    def full(shape):
        return pl.BlockSpec(shape, lambda b, c: (0,) * len(shape))

    return pl.pallas_call(
        functools.partial(_lru_body, cs=cs, nc=nc),
        grid=(nb, nc),
        in_specs=[pl.BlockSpec((1, cs, LRU_WIDTH), lambda b, c: (b, c, LRU_RG // LRU_WIDTH)),
                  pl.BlockSpec((1, cs, LRU_WIDTH), lambda b, c: (b, c, LRU_RX // LRU_WIDTH)),
                  full((3 * cs, cs)), full((CONV_W, LRU_WIDTH)), full((1, LRU_WIDTH)),
                  full((LRU_BLOCKS, LRU_BLOCK_W, LRU_BLOCK_W)), full((1, LRU_WIDTH)),
                  full((LRU_BLOCKS, LRU_BLOCK_W, LRU_BLOCK_W)), full((1, LRU_WIDTH)), full((1, LRU_WIDTH))],
        out_specs=[pl.BlockSpec((1, cs, LRU_WIDTH), lambda b, c: (b, c, 0)),
                   pl.BlockSpec((1, 1, LRU_WIDTH), lambda b, c: (b, 0, 0)),
                   pl.BlockSpec((1, CONV_W - 1, LRU_WIDTH), lambda b, c: (b, 0, 0))],
        out_shape=[jax.ShapeDtypeStruct((nb, l, LRU_WIDTH), BF16),
                   jax.ShapeDtypeStruct((nb, 1, LRU_WIDTH), F32),
                   jax.ShapeDtypeStruct((nb, CONV_W - 1, LRU_WIDTH), F32)],
        scratch_shapes=[pltpu.VMEM((SUBLANES, LRU_WIDTH), F32),
                        pltpu.VMEM((SUBLANES, LRU_WIDTH), F32),
                        pltpu.VMEM((cs, LRU_WIDTH), F32),
                        pltpu.VMEM((cs, LRU_WIDTH), F32),
                        pltpu.VMEM((cs, LRU_WIDTH), F32)],
        compiler_params=_params(("parallel", "arbitrary")),
        name="lru_prompt",
    )(xl, xl, _shift_matrix(cs), cw, cbias, wa, ba, wx, bx, lam)


def _l2norm(x):
    return x * lax.rsqrt(jnp.sum(x * x, axis=-1, keepdims=True) + NORM_EPS)


def _cat3(a, b, c, axis):
    return jnp.concatenate([a, b, c], axis=axis)


def _gdn_body(q_ref, k_ref, v_ref, z_ref, sm_ref, shift_ref, cwq, cwk, cwv, alog_ref, dtb_ref, nw_ref,
              o_ref, s_out, cq_out, ck_out, cv_out, s_ref, bufq, bufk, bufv,
              ph_scr, pl_scr, t_scr, qb_scr, kd_scr, aqk_scr, rh_scr, rl_scr, *, cs, nc):
    c = pl.program_id(1)

    @pl.when(c == 0)
    def _():
        s_ref[...] = jnp.zeros(s_ref.shape, F32)
        for buf in (bufq, bufk, bufv):
            buf[...] = jnp.zeros(buf.shape, F32)

    q = _silu_of_half(_conv_chunk(bufq, q_ref[0], shift_ref, cwq, None, cs))
    k = _silu_of_half(_conv_chunk(bufk, k_ref[0], shift_ref, cwk, None, cs))
    v = _silu_of_half(_conv_chunk(bufv, v_ref[0], shift_ref, cwv, None, cs))
    sm = sm_ref[0]
    g_all = -jnp.exp(alog_ref[...]) * _softplus(sm + dtb_ref[...])
    beta_all = _sigmoid(sm)
    lower = _lower_mask(cs)
    strict = _lower_mask(cs, strict=True)
    diag = lower & jnp.logical_not(strict)
    tri = jnp.where(lower, 1.0, 0.0).astype(BF16)
    gc = _dot01_exact(tri, g_all)
    gc_t = gc.T
    egc_all = jnp.exp(gc)

    for h in range(GDN_HEADS):
        hd = slice(h * GDN_DK, (h + 1) * GDN_DK)
        qh = _l2norm(q[:, hd]) * (GDN_DK ** -0.5)
        kh = _l2norm(k[:, hd])
        beta = beta_all[:, GDN_HEADS + h:GDN_HEADS + h + 1]
        gcol = gc[:, h:h + 1]
        dec = jnp.exp(jnp.where(lower, gcol - gc_t[h:h + 1, :], NEG_BIG))
        qb, kb = qh.astype(BF16), kh.astype(BF16)
        both = _dot_nt(jnp.concatenate([kb, qb], axis=0), kb)
        p = -(beta * jnp.where(strict, dec, 0.0) * both[:cs])
        hi, lo = _split2(p)
        ph_scr[h] = hi
        pl_scr[h] = lo
        t_scr[h] = jnp.where(diag, 1.0, p)
        qb_scr[h] = qb
        aqk_scr[h] = (dec * both[cs:]).astype(BF16)
        glast = gc[cs - 1:cs, h:h + 1]
        kd_scr[h] = (kh * jnp.exp(glast - gcol)).astype(BF16)
        rhs = jnp.concatenate([v[:, hd] * beta, kh * (beta * egc_all[:, h:h + 1])], axis=-1)
        hi, lo = _split2(rhs)
        rh_scr[h] = hi
        rl_scr[h] = lo

    levels = int(math.log2(cs))
    for lvl in range(levels):
        for h in range(GDN_HEADS):
            hi, lo = ph_scr[h], pl_scr[h]
            lhs = _cat3(hi, hi, lo, 1)
            if lvl == 0:
                p2 = _dot(lhs, _cat3(hi, lo, hi, 0))
                t_new = None
            else:
                t = t_scr[h]
                th, tl = _split2(t)
                if lvl < levels - 1:
                    rhs = _cat3(jnp.concatenate([hi, th], axis=1), jnp.concatenate([lo, tl], axis=1),
                                jnp.concatenate([hi, th], axis=1), 0)
                    both = _dot(lhs, rhs)
                    p2, t_new = both[:, :cs], t + both[:, cs:]
                else:
                    p2, t_new = None, t + _dot(lhs, _cat3(th, tl, th, 0))
            if p2 is not None:
                hi2, lo2 = _split2(p2)
                ph_scr[h] = hi2
                pl_scr[h] = lo2
            if t_new is not None:
                t_scr[h] = t_new

    z = z_ref[0].astype(F32)
    us, wqs = [], []
    for h in range(GDN_HEADS):
        th, tl = _split2(t_scr[h])
        rh, rl = rh_scr[h], rl_scr[h]
        sol = _dot(_cat3(th, th, tl, 1), _cat3(rh, rl, rh, 0))
        us.append(sol[:, :GDN_DV])
        wqs.append(jnp.concatenate([sol[:, GDN_DV:].astype(BF16), qb_scr[h]], axis=0))
    dbs, qss = [], []
    for h in range(GDN_HEADS):
        ws_qs = _dot(wqs[h], s_ref[h].astype(BF16))
        dbs.append((us[h] - ws_qs[:cs]).astype(BF16))
        qss.append(ws_qs[cs:])
    for h in range(GDN_HEADS):
        hd = slice(h * GDN_DK, (h + 1) * GDN_DK)
        o = egc_all[:, h:h + 1] * qss[h] + _dot(aqk_scr[h], dbs[h])
        s_ref[h] = jnp.exp(gc[cs - 1:cs, h:h + 1]) * s_ref[h] + _dot_tn(kd_scr[h], dbs[h])
        o_ref[0, :, hd] = (_rms(o, nw_ref[...]) * _silu(z[:, hd])).astype(BF16)

    @pl.when(c == nc - 1)
    def _():
        s_out[0] = s_ref[...]
        cq_out[0] = _conv_state(bufq)
        ck_out[0] = _conv_state(bufk)
        cv_out[0] = _conv_state(bufv)


def _gdn_prompt(xg, sm, cw, alog, dtb, normw, *, cs):
    nb, l = xg.shape[0], xg.shape[1]
    nc = l // cs

    def seg(width, off):
        return pl.BlockSpec((1, cs, width), lambda b, c: (b, c, off // width))

    def full(shape):
        return pl.BlockSpec(shape, lambda b, c: (0,) * len(shape))

    def tail(width):
        return pl.BlockSpec((1, CONV_W - 1, width), lambda b, c: (b, 0, 0))

    hm = (GDN_HEADS, cs, cs)
    return pl.pallas_call(
        functools.partial(_gdn_body, cs=cs, nc=nc),
        grid=(nb, nc),
        in_specs=[seg(1024, GDN_Q), seg(1024, GDN_K), seg(1024, GDN_V), seg(1024, GDN_Z),
                  pl.BlockSpec((1, cs, LANES), lambda b, c: (b, c, 0)), full((3 * cs, cs)),
                  full((CONV_W, 1024)), full((CONV_W, 1024)), full((CONV_W, 1024)),
                  full((1, LANES)), full((1, LANES)), full((1, GDN_DV))],
        out_specs=[pl.BlockSpec((1, cs, 1024), lambda b, c: (b, c, 0)),
                   pl.BlockSpec((1, GDN_HEADS, GDN_DK, GDN_DV), lambda b, c: (b, 0, 0, 0)),
                   tail(1024), tail(1024), tail(1024)],
        out_shape=[jax.ShapeDtypeStruct((nb, l, 1024), BF16),
                   jax.ShapeDtypeStruct((nb, GDN_HEADS, GDN_DK, GDN_DV), F32)]
        + [jax.ShapeDtypeStruct((nb, CONV_W - 1, 1024), F32)] * 3,
        scratch_shapes=[pltpu.VMEM((GDN_HEADS, GDN_DK, GDN_DV), F32)]
        + [pltpu.VMEM((SUBLANES, 1024), F32)] * 3
        + [pltpu.VMEM(hm, BF16), pltpu.VMEM(hm, BF16), pltpu.VMEM(hm, F32),
           pltpu.VMEM((GDN_HEADS, cs, GDN_DK), BF16), pltpu.VMEM((GDN_HEADS, cs, GDN_DK), BF16),
           pltpu.VMEM(hm, BF16), pltpu.VMEM((GDN_HEADS, cs, 2 * GDN_DV), BF16),
           pltpu.VMEM((GDN_HEADS, cs, 2 * GDN_DV), BF16)],
        compiler_params=_params(("parallel", "arbitrary")),
        name="gdn_prompt",
    )(xg, xg, xg, xg, sm, _shift_matrix(cs), cw[:, :1024], cw[:, 1024:2048], cw[:, 2048:], alog, dtb, normw)


def _pad_lanes(v, width, offset=0):
    return jnp.zeros((1, width), F32).at[0, offset:offset + v.shape[0]].set(v.astype(F32))


def _prep(ffn_w_in, ffn_w_out, l0_w_in, l0_w_out, gla_w_gate2, gla_b_gate, gla_norm, ssd_conv_w, ssd_conv_b,
          ssd_dt_bias, ssd_a_log, ssd_d, ssd_norm, l1_w_in, l1_w_out, lru_conv_w, lru_conv_b, lru_w_a, lru_b_a,
          lru_w_x, lru_b_x, lru_lambda, gdn_conv_w, gdn_a_log, gdn_dt_bias, gdn_norm):
    w = {}
    w["ffn_in"] = ffn_w_in.astype(BF16)
    w["ffn_out"] = ffn_w_out.astype(BF16)
    c = l0_w_in
    small0 = jnp.concatenate([c[:, 8208:8240], c[:, 3072:3088], jnp.zeros((D_MODEL, LANES - 48), F32)], axis=1)
    w["l0_in"] = jnp.concatenate(
        [c[:, 1024:2048], c[:, 2048:3072], c[:, 0:512], c[:, 512:1024],
         c[:, 3088:5136], c[:, 5136:7184], c[:, 7184:7696], c[:, 7696:8208],
         small0], axis=1).astype(BF16)
    w["l0_out"] = l0_w_out.astype(BF16)
    w["gla_wg2"] = jnp.zeros((LANES, 512), F32).at[32:48].set(gla_w_gate2).astype(BF16)
    w["gla_bg"] = gla_b_gate.reshape(1, 512)
    w["gla_norm"] = gla_norm.reshape(1, GLA_DV)
    w["ssd_cw"] = ssd_conv_w
    w["ssd_cb"] = ssd_conv_b.reshape(1, -1)
    w["ssd_dtb"] = _pad_lanes(ssd_dt_bias, LANES)
    w["ssd_alog"] = _pad_lanes(ssd_a_log, LANES)
    head_of_lane = jnp.arange(SSD_D_INNER) // SSD_HEAD_DIM
    emat = (jnp.arange(LANES)[:, None] == head_of_lane[None, :])
    w["ssd_e"] = emat.astype(BF16)
    w["ssd_et"] = emat.T.astype(BF16)
    w["ssd_dvec"] = jnp.repeat(ssd_d, SSD_HEAD_DIM).reshape(1, SSD_D_INNER)
    w["ssd_norm"] = ssd_norm.reshape(1, SSD_D_INNER)
    c = l1_w_in
    small1 = jnp.concatenate([c[:, 6656:6672], jnp.zeros((D_MODEL, LANES - 16), F32)], axis=1)
    w["l1_in"] = jnp.concatenate(
        [c[:, 2560:3584], c[:, 3584:4608], c[:, 4608:5632], c[:, 5632:6656],
         c[:, 0:1280], c[:, 1280:2560],
         small1], axis=1).astype(BF16)
    w["l1_out"] = l1_w_out.astype(BF16)
    w["lru_cw"] = lru_conv_w
    w["lru_cb"] = lru_conv_b.reshape(1, -1)
    w["lru_wa"] = lru_w_a.astype(BF16)
    w["lru_ba"] = lru_b_a.reshape(1, -1)
    w["lru_wx"] = lru_w_x.astype(BF16)
    w["lru_bx"] = lru_b_x.reshape(1, -1)
    w["lru_lam"] = lru_lambda.reshape(1, -1)
    w["gdn_cw"] = gdn_conv_w
    w["gdn_alog"] = _pad_lanes(gdn_a_log, LANES)
    w["gdn_dtb"] = _pad_lanes(gdn_dt_bias, LANES)
    w["gdn_norm"] = gdn_norm.reshape(1, GDN_DV)
    return w


L0_WIDTHS = (GLA_W, SSD_W, LANES)
L1_WIDTHS = (GDN_W, LRU_W2, LANES)


def _trunk_prompt(x, norms, final_norm, w, *, tm, cs):
    nb, l = x.shape[0], x.shape[1]
    t = nb * l
    h = x.reshape(t, D_MODEL)
    h = _ffn(h, norms[0, 0], w["ffn_in"], w["ffn_out"], (0, 0), tm=tm)
    xa, xb, sm = _proj(h, norms[0, 1], w["l0_in"], L0_WIDTHS, (BF16, BF16, F32), tm=tm)
    sm = sm.reshape(nb, l, LANES)
    o_a, s_gla = _gla_prompt(xa.reshape(nb, l, GLA_W), sm, w["gla_wg2"], w["gla_bg"], w["gla_norm"], lg=4 * cs)
    y, s_ssd, cx, cb, cc = _ssd_prompt(xb.reshape(nb, l, SSD_W), sm, w["ssd_cw"], w["ssd_cb"], w["ssd_dtb"],
                                       w["ssd_alog"], w["ssd_e"], w["ssd_et"], w["ssd_dvec"], w["ssd_norm"], cs=cs)
    h = _ffn(h, norms[0, 2], w["ffn_in"], w["ffn_out"], (0, 1),
             pre=(o_a.reshape(t, 1024), y.reshape(t, 2048), w["l0_out"]), tm=tm)
    h = _ffn(h, norms[1, 0], w["ffn_in"], w["ffn_out"], (1, 0), tm=tm)
    xg, xl, sm = _proj(h, norms[1, 1], w["l1_in"], L1_WIDTHS, (BF16, BF16, F32), tm=tm)
    sm = sm.reshape(nb, l, LANES)
    o_c, s_lru, s_lru_conv = _lru_prompt(xl.reshape(nb, l, LRU_W2), w["lru_cw"], w["lru_cb"], w["lru_wa"],
                                         w["lru_ba"], w["lru_wx"], w["lru_bx"], w["lru_lam"], cs=2 * cs)
    o_d, s_gdn, cq, ck, cv = _gdn_prompt(xg.reshape(nb, l, GDN_W), sm, w["gdn_cw"], w["gdn_alog"], w["gdn_dtb"],
                                         w["gdn_norm"], cs=cs)
    h = _ffn(h, norms[1, 2], w["ffn_in"], w["ffn_out"], (1, 1), final_norm,
             pre=(o_c.reshape(t, LRU_WIDTH), o_d.reshape(t, 1024), w["l1_out"]), tm=tm)
    return (h.reshape(nb, l, D_MODEL), s_gla, s_ssd.reshape(nb, SSD_HEADS, SSD_HEAD_DIM, SSD_D_STATE),
            jnp.concatenate([cx, cb, cc], axis=-1), s_lru.reshape(nb, LRU_WIDTH), s_lru_conv, s_gdn,
            jnp.concatenate([cq, ck, cv], axis=-1))


def _conv_step(cst_ref, cst_out, cur, w_ref, bias_ref, lo, hi):
    acc = w_ref[CONV_W - 1:CONV_W, lo:hi] * cur
    for j in range(CONV_W - 1):
        acc = acc + w_ref[j:j + 1, lo:hi] * cst_ref[:, j, lo:hi]
    if bias_ref is not None:
        acc = acc + bias_ref[:, lo:hi]
    for j in range(CONV_W - 2):
        cst_out[:, j, lo:hi] = cst_ref[:, j + 1, lo:hi]
    cst_out[:, CONV_W - 2, lo:hi] = cur
    return acc


def _gla_dec_body(q_ref, k_ref, v_ref, gg_ref, sm_ref, wg2_ref, bg_ref, nw_ref, s_in, o_ref, s_out, o_scr, *, nb):
    pre = _dot(sm_ref[...].astype(BF16), wg2_ref[...]) + bg_ref[...]
    a = jnp.exp(-_softplus(-pre) * (1.0 / GLA_GATE_NORM))
    q = q_ref[...] * (GLA_DK ** -0.5)
    v = v_ref[...]
    a_t, q_t, k_t = a.T, q.T, k_ref[...].T
    for b in range(nb):
        for h in range(GLA_HEADS):
            dk = slice(h * GLA_DK, (h + 1) * GLA_DK)
            dv = slice(h * GLA_DV, (h + 1) * GLA_DV)
            sn = a_t[dk, b:b + 1] * s_in[b, h] + k_t[dk, b:b + 1] * v[b:b + 1, dv]
            s_out[b, h] = sn
            o_scr[b:b + 1, dv] = jnp.sum(q_t[dk, b:b + 1] * sn, axis=0, keepdims=True)
    o = o_scr[...]
    gg = gg_ref[...]
    for h in range(GLA_HEADS):
        dv = slice(h * GLA_DV, (h + 1) * GLA_DV)
        o_ref[:, dv] = _rms(o[:, dv], nw_ref[...]) * _silu(gg[:, dv])


def _gla_step(xa, sm, state, wg2, bgate, normw, *, nb):
    n = xa.shape[0]

    def seg(width, off):
        return pl.BlockSpec((nb, width), lambda i: (i, off // width))

    def full(shape):
        return pl.BlockSpec(shape, lambda i: (0,) * len(shape))

    st = pl.BlockSpec((nb, GLA_HEADS, GLA_DK, GLA_DV), lambda i: (i, 0, 0, 0))
    return pl.pallas_call(
        functools.partial(_gla_dec_body, nb=nb),
        grid=(n // nb,),
        in_specs=[seg(512, GLA_GQ), seg(512, GLA_GK), seg(1024, GLA_GV), seg(1024, GLA_GG), seg(LANES, 0),
                  full((LANES, 512)), full((1, 512)), full((1, GLA_DV)), st],
        out_specs=[pl.BlockSpec((nb, 1024), lambda i: (i, 0)), st],
        out_shape=[jax.ShapeDtypeStruct((n, 1024), F32), jax.ShapeDtypeStruct(state.shape, F32)],
        scratch_shapes=[pltpu.VMEM((nb, 1024), F32)],
        compiler_params=_params(("parallel",)),
        name="gla_step",
    )(xa, xa, xa, xa, sm, wg2, bgate, normw, state)


def _ssd_dec_body(z_ref, x_ref, b_ref, c_ref, sm_ref, cst_ref, cw, cbias, dtb_ref, alog_ref, e_ref, dvec_ref,
                  nw_ref, s_in, y_ref, s_out, cst_out, yt_scr, *, nb):
    xs = _silu(_conv_step(cst_ref, cst_out, x_ref[...], cw, cbias, 0, 2048))
    bm = _silu(_conv_step(cst_ref, cst_out, b_ref[...], cw, cbias, 2048, 2560))
    cm = _silu(_conv_step(cst_ref, cst_out, c_ref[...], cw, cbias, 2560, 3072)).astype(BF16)
    dt = _softplus(sm_ref[...] + dtb_ref[...])
    decay = jnp.exp(dt * (-jnp.exp(alog_ref[...])))
    dtx = (_dot_x01(dt, e_ref[...]) * xs).astype(BF16)
    lane = lax.broadcasted_iota(jnp.int32, (SSD_GROUP_W, nb), 1)
    seq = lax.broadcasted_iota(jnp.int32, (nb, 1), 0)
    tile = (SSD_HEAD_DIM, SSD_D_STATE)
    for g in range(SSD_GROUPS):
        gn = slice(g * SSD_D_STATE, (g + 1) * SSD_D_STATE)
        gw = slice(g * SSD_GROUP_W, (g + 1) * SSD_GROUP_W)
        rows = pl.ds(g * SSD_GROUP_W, SSD_GROUP_W)
        ycol = jnp.zeros((SSD_GROUP_W, nb), F32)
        for b in range(nb):
            b_only = jnp.where(seq == b, bm[:, gn], 0.0).astype(BF16)
            dec = jnp.concatenate([jnp.broadcast_to(decay[b:b + 1, h:h + 1], tile)
                                   for h in range(g * SSD_HPG, (g + 1) * SSD_HPG)], axis=0)
            sn = dec * s_in[b, rows, :] + _dot_tn(dtx[:, gw], b_only)
            s_out[b, rows, :] = sn
            ycol = jnp.where(lane == b, _dot_nt(sn.astype(BF16), cm[:, gn]), ycol)
        yt_scr[rows, :] = ycol
    y = yt_scr[...].T + dvec_ref[...] * xs
    y = y * _silu(z_ref[...])
    for g in range(SSD_GROUPS):
        gw = slice(g * SSD_GROUP_W, (g + 1) * SSD_GROUP_W)
        y_ref[:, gw] = _rms(y[:, gw], nw_ref[:, gw])


def _ssd_step(xb, sm, state, cstate, cw, cb, dtb, alog, emat, dvec, normw, *, nb):
    n = xb.shape[0]

    def seg(width, off):
        return pl.BlockSpec((nb, width), lambda i: (i, off // width))

    def full(shape):
        return pl.BlockSpec(shape, lambda i: (0,) * len(shape))

    st = pl.BlockSpec((nb, SSD_D_INNER, SSD_D_STATE), lambda i: (i, 0, 0))
    cst = pl.BlockSpec((nb, CONV_W - 1, 3072), lambda i: (i, 0, 0))
    return pl.pallas_call(
        functools.partial(_ssd_dec_body, nb=nb),
        grid=(n // nb,),
        in_specs=[seg(2048, SSD_Z), seg(2048, SSD_X), seg(512, SSD_B), seg(512, SSD_C), seg(LANES, 0), cst,
                  full((CONV_W, 3072)), full((1, 3072)), full((1, LANES)), full((1, LANES)), full((LANES, 2048)),
                  full((1, 2048)), full((1, 2048)), st],
        out_specs=[pl.BlockSpec((nb, 2048), lambda i: (i, 0)), st, cst],
        out_shape=[jax.ShapeDtypeStruct((n, 2048), F32), jax.ShapeDtypeStruct(state.shape, F32),
                   jax.ShapeDtypeStruct(cstate.shape, F32)],
        scratch_shapes=[pltpu.VMEM((SSD_D_INNER, nb), F32)],
        compiler_params=_params(("parallel",)),
        name="ssd_step",
    )(xb, xb, xb, xb, sm, cstate, cw, cb, dtb, alog, emat, dvec, normw, state)


def _lru_dec_body(rg_ref, rx_ref, cst_ref, cw, cbias, wa_ref, ba_ref, wx_ref, bx_ref, lam_ref, h_in,
                  o_ref, h_out, cst_out):
    xc = _conv_step(cst_ref, cst_out, rx_ref[...], cw, cbias, 0, LRU_WIDTH)
    a, b = _lru_gates(0.5 * xc, wa_ref, ba_ref, wx_ref, bx_ref, lam_ref)
    h = a * h_in[...] + b
    h_out[...] = h
    o_ref[...] = _gelu_tanh(rg_ref[...]) * h


def _lru_step(xl, hstate, cstate, cw, cbias, wa, ba, wx, bx, lam):
    n = xl.shape[0]

    def full(shape):
        return pl.BlockSpec(shape, lambda i: (0,) * len(shape))

    return pl.pallas_call(
        _lru_dec_body,
        grid=(1,),
        in_specs=[pl.BlockSpec((n, LRU_WIDTH), lambda i: (0, LRU_RG // LRU_WIDTH)),
                  pl.BlockSpec((n, LRU_WIDTH), lambda i: (0, LRU_RX // LRU_WIDTH)),
                  full((n, CONV_W - 1, LRU_WIDTH)), full((CONV_W, LRU_WIDTH)), full((1, LRU_WIDTH)),
                  full((LRU_BLOCKS, LRU_BLOCK_W, LRU_BLOCK_W)), full((1, LRU_WIDTH)),
                  full((LRU_BLOCKS, LRU_BLOCK_W, LRU_BLOCK_W)), full((1, LRU_WIDTH)), full((1, LRU_WIDTH)),
                  full((n, LRU_WIDTH))],
        out_specs=[full((n, LRU_WIDTH)), full((n, LRU_WIDTH)), full((n, CONV_W - 1, LRU_WIDTH))],
        out_shape=[jax.ShapeDtypeStruct((n, LRU_WIDTH), F32), jax.ShapeDtypeStruct((n, LRU_WIDTH), F32),
                   jax.ShapeDtypeStruct(cstate.shape, F32)],
        compiler_params=_params(("arbitrary",)),
        name="lru_step",
    )(xl, xl, cstate, cw, cbias, wa, ba, wx, bx, lam, hstate)


def _gdn_dec_body(q_ref, k_ref, v_ref, z_ref, sm_ref, cst_ref, cw, alog_ref, dtb_ref, nw_ref, s_in,
                  o_ref, s_out, cst_out, o_scr, *, nb):
    q = _silu(_conv_step(cst_ref, cst_out, q_ref[...], cw, None, 0, 1024))
    k = _silu(_conv_step(cst_ref, cst_out, k_ref[...], cw, None, 1024, 2048))
    v = _silu(_conv_step(cst_ref, cst_out, v_ref[...], cw, None, 2048, 3072))
    sm = sm_ref[...]
    eg_all = jnp.exp(-jnp.exp(alog_ref[...]) * _softplus(sm + dtb_ref[...]))
    beta_all = _sigmoid(sm)
    qn = jnp.concatenate([_l2norm(q[:, h * GDN_DK:(h + 1) * GDN_DK]) for h in range(GDN_HEADS)], axis=-1)
    qn = qn * (GDN_DK ** -0.5)
    kn = jnp.concatenate([_l2norm(k[:, h * GDN_DK:(h + 1) * GDN_DK]) for h in range(GDN_HEADS)], axis=-1)
    kq = jnp.concatenate([kn, qn], axis=0).astype(BF16)
    seq = lax.broadcasted_iota(jnp.int32, (nb, 1), 0)
    pairs = [(b, h) for b in range(nb) for h in range(GDN_HEADS)]
    ksqs = [_dot(kq[:, h * GDN_DK:(h + 1) * GDN_DK], s_in[b, h].astype(BF16)) for b, h in pairs]
    for (b, h), kq_s in zip(pairs, ksqs):
        hd = slice(h * GDN_DK, (h + 1) * GDN_DK)
        eg = eg_all[b:b + 1, h:h + 1]
        beta = beta_all[b:b + 1, GDN_HEADS + h:GDN_HEADS + h + 1]
        delta = beta * (v[b:b + 1, hd] - eg * kq_s[b:b + 1])
        qk = jnp.sum(qn[b:b + 1, hd] * kn[b:b + 1, hd], axis=-1, keepdims=True)
        o_scr[b:b + 1, hd] = eg * kq_s[nb + b:nb + b + 1] + qk * delta
        d_only = jnp.where(seq == b, jnp.broadcast_to(delta, (nb, GDN_DV)), 0.0).astype(BF16)
        s_out[b, h] = eg * s_in[b, h] + _dot_tn(kq[:nb, hd], d_only)
    o = o_scr[...]
    z = z_ref[...]
    for h in range(GDN_HEADS):
        hd = slice(h * GDN_DK, (h + 1) * GDN_DK)
        o_ref[:, hd] = _rms(o[:, hd], nw_ref[...]) * _silu(z[:, hd])


def _gdn_step(xg, sm, state, cstate, cw, alog, dtb, normw, *, nb):
    n = xg.shape[0]

    def seg(width, off):
        return pl.BlockSpec((nb, width), lambda i: (i, off // width))

    def full(shape):
        return pl.BlockSpec(shape, lambda i: (0,) * len(shape))

    st = pl.BlockSpec((nb, GDN_HEADS, GDN_DK, GDN_DV), lambda i: (i, 0, 0, 0))
    cst = pl.BlockSpec((nb, CONV_W - 1, 3072), lambda i: (i, 0, 0))
    return pl.pallas_call(
        functools.partial(_gdn_dec_body, nb=nb),
        grid=(n // nb,),
        in_specs=[seg(1024, GDN_Q), seg(1024, GDN_K), seg(1024, GDN_V), seg(1024, GDN_Z), seg(LANES, 0), cst,
                  full((CONV_W, 3072)), full((1, LANES)), full((1, LANES)), full((1, GDN_DV)), st],
        out_specs=[pl.BlockSpec((nb, 1024), lambda i: (i, 0)), st, cst],
        out_shape=[jax.ShapeDtypeStruct((n, 1024), F32), jax.ShapeDtypeStruct(state.shape, F32),
                   jax.ShapeDtypeStruct(cstate.shape, F32)],
        scratch_shapes=[pltpu.VMEM((nb, 1024), F32)],
        compiler_params=_params(("parallel",)),
        name="gdn_step",
    )(xg, xg, xg, xg, sm, cstate, cw, alog, dtb, normw, state)


def _trunk_sample(x, s_gla, s_ssd, s_ssd_conv, s_lru, s_lru_conv, s_gdn, s_gdn_conv, norms, final_norm, w, *, nb=8):
    n = x.shape[0]
    f3 = (F32, F32, F32)
    h = x.reshape(n, D_MODEL)
    h = _ffn(h, norms[0, 0], w["ffn_in"], w["ffn_out"], (0, 0), tm=n)
    xa, xb, sm = _proj(h, norms[0, 1], w["l0_in"], L0_WIDTHS, f3, tm=n)
    o_a, n_gla = _gla_step(xa, sm, s_gla, w["gla_wg2"], w["gla_bg"], w["gla_norm"], nb=nb)
    y, n_ssd, n_ssd_conv = _ssd_step(xb, sm, s_ssd.reshape(n, SSD_D_INNER, SSD_D_STATE), s_ssd_conv, w["ssd_cw"],
                                     w["ssd_cb"], w["ssd_dtb"], w["ssd_alog"], w["ssd_e"], w["ssd_dvec"],
                                     w["ssd_norm"], nb=nb)
    h = _ffn(h, norms[0, 2], w["ffn_in"], w["ffn_out"], (0, 1), pre=(o_a, y, w["l0_out"]), tm=n)
    h = _ffn(h, norms[1, 0], w["ffn_in"], w["ffn_out"], (1, 0), tm=n)
    xg, xl, sm = _proj(h, norms[1, 1], w["l1_in"], L1_WIDTHS, f3, tm=n)
    o_c, n_lru, n_lru_conv = _lru_step(xl, s_lru, s_lru_conv, w["lru_cw"], w["lru_cb"], w["lru_wa"], w["lru_ba"],
                                       w["lru_wx"], w["lru_bx"], w["lru_lam"])
    o_d, n_gdn, n_gdn_conv = _gdn_step(xg, sm, s_gdn, s_gdn_conv, w["gdn_cw"], w["gdn_alog"], w["gdn_dtb"],
                                       w["gdn_norm"], nb=nb)
    h = _ffn(h, norms[1, 2], w["ffn_in"], w["ffn_out"], (1, 1), final_norm,
             pre=(o_c, o_d, w["l1_out"]), tm=n)
    return (h.reshape(n, 1, D_MODEL), n_gla, n_ssd.reshape(s_ssd.shape), n_ssd_conv, n_lru, n_lru_conv, n_gdn,
            n_gdn_conv)


def kernel(x_prompt, x_sample, state_gla, state_ssd, state_ssd_conv, state_lru, state_lru_conv, state_gdn,
           state_gdn_conv, norms, final_norm, ffn_w_in, ffn_w_out, l0_w_in, l0_w_out, gla_w_gate2, gla_b_gate,
           gla_norm, ssd_conv_w, ssd_conv_b, ssd_dt_bias, ssd_a_log, ssd_d, ssd_norm, l1_w_in, l1_w_out,
           lru_conv_w, lru_conv_b, lru_w_a, lru_b_a, lru_w_x, lru_b_x, lru_lambda, gdn_conv_w, gdn_a_log,
           gdn_dt_bias, gdn_norm):
    w = _prep(ffn_w_in, ffn_w_out, l0_w_in, l0_w_out, gla_w_gate2, gla_b_gate, gla_norm, ssd_conv_w, ssd_conv_b,
              ssd_dt_bias, ssd_a_log, ssd_d, ssd_norm, l1_w_in, l1_w_out, lru_conv_w, lru_conv_b, lru_w_a, lru_b_a,
              lru_w_x, lru_b_x, lru_lambda, gdn_conv_w, gdn_a_log, gdn_dt_bias, gdn_norm)
    prompt = _trunk_prompt(x_prompt, norms, final_norm, w, tm=512, cs=128)
    sample = _trunk_sample(x_sample, state_gla, state_ssd, state_ssd_conv, state_lru, state_lru_conv, state_gdn,
                           state_gdn_conv, norms, final_norm, w)
    return (prompt[0], sample[0]) + tuple(prompt[1:]) + tuple(sample[1:])
```
